```python
import jax, jax.numpy as jnp
from jax import lax
import numpy as np

D_MODEL = 1024
BATCH = 8
SEQ = 4096
DEPTH = 2

ATT_HEADS = 8
ATT_KV_HEADS = 2
ATT_HEAD_DIM = 64
ATT_GROUP = ATT_HEADS // ATT_KV_HEADS
WINDOW = 128
ATT_BLOCK = 128
ROPE_DIM = ATT_HEAD_DIM // 4
ROPE_THETA = 500000.0
M_HEADS = 4
M_QK_DIM = 64
M_V_DIM = 128
M_CHUNK = 64
D_FF = 4 * D_MODEL
EPS = 1e-6

ATT_Q_W = ATT_HEADS * ATT_HEAD_DIM
ATT_KV_W = ATT_KV_HEADS * ATT_HEAD_DIM
M_QK_W = M_HEADS * M_QK_DIM
M_V_W = M_HEADS * M_V_DIM
IN_WIDTHS = (ATT_Q_W, ATT_KV_W, ATT_KV_W, M_QK_W, M_QK_W, M_V_W, M_V_W, 2 * M_HEADS, 2 * D_MODEL)
D_IN = sum(IN_WIDTHS)

kernel_name = "hybrid_swa_mlstm_gated_block"


def rmsnorm(x, gain):
    x32 = x.astype(jnp.float32)
    inv = lax.rsqrt(jnp.mean(x32 * x32, axis=-1, keepdims=True) + EPS)
    return (x32 * inv).astype(x.dtype) * gain


def rope_tables(seq_len):
    pos = jnp.arange(seq_len, dtype=jnp.float32)
    inv_freq = ROPE_THETA ** (-jnp.arange(0, ROPE_DIM, 2, dtype=jnp.float32) / ROPE_DIM)
    ang = pos[:, None] * inv_freq[None, :]
    return jnp.cos(ang), jnp.sin(ang)


def partial_rope(x, cos, sin):
    half = ROPE_DIM // 2
    x1 = x[..., :half].astype(jnp.float32)
    x2 = x[..., half:ROPE_DIM].astype(jnp.float32)
    c = cos[None, :, None, :]
    s = sin[None, :, None, :]
    return jnp.concatenate([(x1 * c - x2 * s).astype(x.dtype),
                            (x2 * c + x1 * s).astype(x.dtype),
                            x[..., ROPE_DIM:]], axis=-1)


def sliding_window_attention(q, k, v, sinks):
    B, S, _, hd = q.shape
    nb = S // ATT_BLOCK
    qb = q.reshape(B, nb, ATT_BLOCK, ATT_KV_HEADS, ATT_GROUP, hd)

    def with_prev(t):
        tb = t.reshape(B, nb, ATT_BLOCK, ATT_KV_HEADS, hd)
        prev = jnp.pad(tb, ((0, 0), (1, 0), (0, 0), (0, 0), (0, 0)))[:, :-1]
        return jnp.concatenate([prev, tb], axis=2)

    kb, vb = with_prev(k), with_prev(v)
    scores = jnp.einsum('bnqhgd,bnkhd->bnhgqk', qb, kb).astype(jnp.float32) * (hd ** -0.5)
    blk = jnp.arange(nb)[:, None, None]
    qpos = blk * ATT_BLOCK + jnp.arange(ATT_BLOCK)[None, :, None]
    kpos = (blk - 1) * ATT_BLOCK + jnp.arange(2 * ATT_BLOCK)[None, None, :]
    valid = (kpos >= 0) & (kpos <= qpos) & (qpos - kpos < WINDOW)
    scores = jnp.where(valid[None, :, None, None], scores, -jnp.inf)
    sink = sinks.astype(jnp.float32).reshape(ATT_KV_HEADS, ATT_GROUP)[None, None, :, :, None, None]
    m = jnp.maximum(jnp.max(scores, axis=-1, keepdims=True), sink)
    p = jnp.exp(scores - m)
    probs = (p / (jnp.sum(p, axis=-1, keepdims=True) + jnp.exp(sink - m))).astype(v.dtype)
    out = jnp.einsum('bnhgqk,bnkhd->bnqhgd', probs, vb)
    return out.reshape(B, S, ATT_HEADS * hd)


def mlstm_chunkwise(q, k, v, i_pre, f_pre):
    f32 = jnp.float32
    B, S, H, dk = q.shape
    dv = v.shape[-1]
    L = M_CHUNK
    nc = S // L

    def chunks(t):
        return t.astype(f32).reshape(B, nc, L, H, -1).transpose(0, 1, 3, 2, 4)

    q = chunks(q) * (dk ** -0.5)
    k = chunks(k)
    v = chunks(v)
    log_f = jax.nn.log_sigmoid(f_pre.astype(f32)).reshape(B, nc, L, H).transpose(0, 1, 3, 2)
    log_i = i_pre.astype(f32).reshape(B, nc, L, H).transpose(0, 1, 3, 2)
    b = jnp.cumsum(log_f, axis=-1)
    g = b[..., -1]

    w_end = g[..., None] - b + log_i
    m_loc = jnp.max(w_end, axis=-1)
    e = jnp.exp(w_end - m_loc[..., None])
    dC = jnp.einsum('bnhlv,bnhlk->bnhvk', v * e[..., None], k)
    dn = jnp.einsum('bnhl,bnhlk->bnhk', e, k)

    def step(carry, inp):
        C, n, m = carry
        dC_c, dn_c, g_c, m_loc_c = inp
        m_new = jnp.maximum(g_c + m, m_loc_c)
        a = jnp.exp(g_c + m - m_new)
        s = jnp.exp(m_loc_c - m_new)
        C_new = a[..., None, None] * C + s[..., None, None] * dC_c
        n_new = a[..., None] * n + s[..., None] * dn_c
        return (C_new, n_new, m_new), (C, n, m)

    init = (jnp.zeros((B, H, dv, dk), f32), jnp.zeros((B, H, dk), f32), jnp.zeros((B, H), f32))
    xs = (jnp.moveaxis(dC, 1, 0), jnp.moveaxis(dn, 1, 0), jnp.moveaxis(g, 1, 0), jnp.moveaxis(m_loc, 1, 0))
    _, (C_prev, n_prev, m_prev) = lax.scan(step, init, xs)
    C_prev = jnp.moveaxis(C_prev, 0, 1)
    n_prev = jnp.moveaxis(n_prev, 0, 1)
    m_prev = jnp.moveaxis(m_prev, 0, 1)

    causal = jnp.tril(jnp.ones((L, L), dtype=bool))
    log_d = jnp.where(causal, b[..., :, None] - b[..., None, :] + log_i[..., None, :], -jnp.inf)
    log_inter = b + m_prev[..., None]
    m_t = jnp.maximum(log_inter, jnp.max(log_d, axis=-1))
    w = jnp.exp(log_d - m_t[..., None]) * jnp.einsum('bnhlk,bnhsk->bnhls', q, k)
    a_inter = jnp.exp(log_inter - m_t)
    num = jnp.einsum('bnhls,bnhsv->bnhlv', w, v) + a_inter[..., None] * jnp.einsum('bnhvk,bnhlk->bnhlv', C_prev, q)
    den = jnp.sum(w, axis=-1) + a_inter * jnp.einsum('bnhk,bnhlk->bnhl', n_prev, q)
    h = num / jnp.maximum(jnp.abs(den), jnp.exp(-m_t))[..., None]
    return h.transpose(0, 1, 3, 2, 4).reshape(B, S, H, dv)


def hybrid_layer(x, cos, sin, norm_mix, w_in, att_q_norm, att_k_norm, att_sinks, m_gate_bias,
                 m_head_norm, w_att_branch, w_m_branch, w_out, norm_ffn, w_ff1, w_ff2):
    B, S, _ = x.shape
    h = rmsnorm(x, norm_mix)
    z = h @ w_in
    splits = [int(s) for s in np.cumsum(IN_WIDTHS)[:-1]]
    q_a, k_a, v_a, q_m, k_m, v_m, o_m, if_m, gates = jnp.split(z, splits, axis=-1)

    q_a = partial_rope(rmsnorm(q_a.reshape(B, S, ATT_HEADS, ATT_HEAD_DIM), att_q_norm), cos, sin)
    k_a = partial_rope(rmsnorm(k_a.reshape(B, S, ATT_KV_HEADS, ATT_HEAD_DIM), att_k_norm), cos, sin)
    v_a = v_a.reshape(B, S, ATT_KV_HEADS, ATT_HEAD_DIM)
    att = sliding_window_attention(q_a, k_a, v_a, att_sinks)

    i_pre, f_pre = jnp.split(if_m + m_gate_bias, 2, axis=-1)
    hm = mlstm_chunkwise(q_m.reshape(B, S, M_HEADS, M_QK_DIM),
                         k_m.reshape(B, S, M_HEADS, M_QK_DIM),
                         v_m.reshape(B, S, M_HEADS, M_V_DIM), i_pre, f_pre).astype(x.dtype)
    hm = rmsnorm(hm, m_head_norm.reshape(M_HEADS, M_V_DIM)).reshape(B, S, M_V_W)
    hm = hm * jax.nn.sigmoid(o_m)

    g_a, g_m = jnp.split(gates, 2, axis=-1)
    mixed = jax.nn.sigmoid(g_a) * (att @ w_att_branch) + jax.nn.sigmoid(g_m) * (hm @ w_m_branch)
    x = x + mixed @ w_out

    u = jax.nn.relu(rmsnorm(x, norm_ffn) @ w_ff1)
    return x + (u * u) @ w_ff2


def setup_inputs(seed: int = 0) -> dict:
    key = jax.random.key(seed)
    ks = jax.random.split(key, 16)
    nrm = jax.random.normal
    f32 = jnp.float32
    x = nrm(ks[0], (BATCH, SEQ, D_MODEL), f32)
    norm_mix = 1.0 + 0.02 * nrm(ks[1], (DEPTH, D_MODEL), f32)
    w_in = nrm(ks[2], (DEPTH, D_MODEL, D_IN), f32) * D_MODEL ** -0.5
    att_q_norm = 1.0 + 0.02 * nrm(ks[3], (DEPTH, ATT_HEAD_DIM), f32)
    att_k_norm = 1.0 + 0.02 * nrm(ks[4], (DEPTH, ATT_HEAD_DIM), f32)
    att_sinks = 0.5 * nrm(ks[5], (DEPTH, ATT_HEADS), f32)
    i_bias = -1.0 + 0.1 * nrm(ks[6], (DEPTH, M_HEADS), f32)
    f_bias = 3.0 + 0.5 * nrm(ks[7], (DEPTH, M_HEADS), f32)
    m_gate_bias = jnp.concatenate([i_bias, f_bias], axis=-1)
    m_head_norm = 1.0 + 0.02 * nrm(ks[8], (DEPTH, M_V_W), f32)
    w_att_branch = nrm(ks[9], (DEPTH, ATT_Q_W, D_MODEL), f32) * ATT_Q_W ** -0.5
    w_m_branch = nrm(ks[10], (DEPTH, M_V_W, D_MODEL), f32) * M_V_W ** -0.5
    w_out = nrm(ks[11], (DEPTH, D_MODEL, D_MODEL), f32) * D_MODEL ** -0.5
    norm_ffn = 1.0 + 0.02 * nrm(ks[12], (DEPTH, D_MODEL), f32)
    w_ff1 = nrm(ks[13], (DEPTH, D_MODEL, D_FF), f32) * D_MODEL ** -0.5
    w_ff2 = nrm(ks[14], (DEPTH, D_FF, D_MODEL), f32) * D_FF ** -0.5
    return {"x": x, "norm_mix": norm_mix, "w_in": w_in, "att_q_norm": att_q_norm,
            "att_k_norm": att_k_norm, "att_sinks": att_sinks, "m_gate_bias": m_gate_bias,
            "m_head_norm": m_head_norm, "w_att_branch": w_att_branch, "w_m_branch": w_m_branch,
            "w_out": w_out, "norm_ffn": norm_ffn, "w_ff1": w_ff1, "w_ff2": w_ff2}


def reference(x, norm_mix, w_in, att_q_norm, att_k_norm, att_sinks, m_gate_bias, m_head_norm,
              w_att_branch, w_m_branch, w_out, norm_ffn, w_ff1, w_ff2):
    cos, sin = rope_tables(x.shape[1])
    for layer in range(DEPTH):
        x = hybrid_layer(x, cos, sin, norm_mix[layer], w_in[layer], att_q_norm[layer],
                         att_k_norm[layer], att_sinks[layer], m_gate_bias[layer],
                         m_head_norm[layer], w_att_branch[layer], w_m_branch[layer],
                         w_out[layer], norm_ffn[layer], w_ff1[layer], w_ff2[layer])
    return x
```

```python
import functools

import jax
import jax.numpy as jnp
import numpy as np
from jax import lax
from jax.experimental import pallas as pl
from jax.experimental.pallas import tpu as pltpu

D_MODEL = 1024
ATT_HEADS = 8
ATT_KV_HEADS = 2
ATT_HEAD_DIM = 64
ATT_BLOCK = 128
ROPE_DIM = ATT_HEAD_DIM // 4
ROPE_THETA = 500000.0
M_HEADS = 4
M_QK_DIM = 64
M_V_DIM = 128
M_CHUNK = 128
D_FF = 4 * D_MODEL
EPS = 1e-6

ATT_Q_W = ATT_HEADS * ATT_HEAD_DIM
ATT_KV_W = ATT_KV_HEADS * ATT_HEAD_DIM
M_QK_W = M_HEADS * M_QK_DIM
M_V_W = M_HEADS * M_V_DIM
IN_WIDTHS = (ATT_Q_W, ATT_KV_W, ATT_KV_W, M_QK_W, M_QK_W, M_V_W, M_V_W, 2 * M_HEADS, 2 * D_MODEL)

LANES = 128
NEG_BIG = -1e30
VMEM_LIMIT = 56 * 1024 * 1024

BF16 = jnp.bfloat16
F32 = jnp.float32


def _dot(a, b):
    return jnp.dot(a, b, preferred_element_type=F32)


def _dot_nt(a, b):
    return lax.dot_general(a, b, (((1,), (1,)), ((), ())), preferred_element_type=F32)


def _dot_tn(a, b):
    return lax.dot_general(a, b, (((0,), (0,)), ((), ())), preferred_element_type=F32)


def _sigmoid(x):
    return 1.0 / (1.0 + jnp.exp(-x))


def _log_sigmoid(x):
    return jnp.minimum(x, 0.0) - jnp.log(1.0 + jnp.exp(-jnp.abs(x)))


def _const_spec(shape):
    return pl.BlockSpec(shape, lambda *_: (0,) * len(shape))


def _in_proj_kernel(x_ref, gain_ref, w_att_ref, w_m_ref, w_g_ref, w_if_ref, bias_ref,
                    qgain_ref, kgain_ref, rc_ref, rs1_ref, rs2_ref,
                    qa_ref, kk_ref, vv_ref, qm_ref, km_ref, vm_ref, om_ref, sg_ref,
                    gcol_ref, grow_ref):
    x = x_ref[...]
    inv = lax.rsqrt(jnp.mean(x * x, axis=-1, keepdims=True) + EPS)
    h = ((x * inv) * gain_ref[...]).astype(BF16)

    tm = x.shape[0]
    lane = lax.broadcasted_iota(jnp.int32, (tm, LANES), 1)
    low = lane < ATT_HEAD_DIM
    rc, rs1, rs2 = rc_ref[...], rs1_ref[...], rs2_ref[...]

    def head_norm_rope(zc, gain):
        s = zc * zc
        r0 = jnp.sum(jnp.where(low, s, 0.0), axis=-1, keepdims=True)
        r1 = jnp.sum(jnp.where(low, 0.0, s), axis=-1, keepdims=True)
        inv_h = lax.rsqrt(jnp.where(low, r0, r1) * (1.0 / ATT_HEAD_DIM) + EPS)
        y = (zc * inv_h) * gain
        up = pltpu.roll(y, LANES - ROPE_DIM // 2, 1)
        dn = pltpu.roll(y, ROPE_DIM // 2, 1)
        return y * rc + up * rs1 + dn * rs2

    za = _dot(h, w_att_ref[...])
    qgain, kgain = qgain_ref[...], kgain_ref[...]
    scale = ATT_HEAD_DIM ** -0.5
    for c in range(ATT_Q_W // LANES):
        qc = head_norm_rope(za[:, c * LANES:(c + 1) * LANES], qgain) * scale
        qa_ref[:, c * LANES:(c + 1) * LANES] = qc.astype(BF16)
    for c in range(2):
        o = ATT_Q_W + c * LANES
        kk_ref[:, c * LANES:(c + 1) * LANES] = head_norm_rope(za[:, o:o + LANES], kgain).astype(BF16)
    vv_ref[...] = za[:, ATT_Q_W + 2 * LANES:].astype(BF16)

    zm = _dot(h, w_m_ref[...])
    qm_ref[...] = (zm[:, :M_QK_W] * (M_QK_DIM ** -0.5)).astype(BF16)
    km_ref[...] = zm[:, M_QK_W:2 * M_QK_W].astype(BF16)
    vm_ref[...] = zm[:, 2 * M_QK_W:2 * M_QK_W + M_V_W].astype(BF16)
    om_ref[...] = _sigmoid(zm[:, 2 * M_QK_W + M_V_W:]).astype(BF16)

    sg_ref[...] = _sigmoid(_dot(h, w_g_ref[...])).astype(BF16)

    zi = _dot(h, w_if_ref[...]) + bias_ref[...]
    g = jnp.where(lane < M_HEADS, zi, jnp.where(lane < 2 * M_HEADS, _log_sigmoid(zi), 0.0))
    gcol_ref[...] = g
    grow_ref[...] = g.T[:2 * M_HEADS, :]


def _in_proj(x, gain, w_att, w_m, w_g, w_if, bias, qgain, kgain, rc, rs1, rs2, *, seq, tm):
    T, D = x.shape
    n_pos = seq // tm
    row = lambda w: pl.BlockSpec((tm, w), lambda i: (i, 0))
    pos = pl.BlockSpec((tm, LANES), lambda i: (i % n_pos, 0))
    out_shapes = (
        jax.ShapeDtypeStruct((T, ATT_Q_W), BF16),
        jax.ShapeDtypeStruct((T, 2 * ATT_KV_W), BF16),
        jax.ShapeDtypeStruct((T, 2 * ATT_KV_W), BF16),
        jax.ShapeDtypeStruct((T, M_QK_W), BF16),
        jax.ShapeDtypeStruct((T, M_QK_W), BF16),
        jax.ShapeDtypeStruct((T, M_V_W), BF16),
        jax.ShapeDtypeStruct((T, M_V_W), BF16),
        jax.ShapeDtypeStruct((T, 2 * D), BF16),
        jax.ShapeDtypeStruct((T, LANES), F32),
        jax.ShapeDtypeStruct((2 * M_HEADS, T), F32),
    )
    out_specs = (row(ATT_Q_W), row(2 * ATT_KV_W), row(2 * ATT_KV_W), row(M_QK_W), row(M_QK_W),
                 row(M_V_W), row(M_V_W), row(2 * D), row(LANES),
                 pl.BlockSpec((2 * M_HEADS, tm), lambda i: (0, i)))
    in_specs = [row(D), _const_spec(gain.shape), _const_spec(w_att.shape), _const_spec(w_m.shape),
                _const_spec(w_g.shape), _const_spec(w_if.shape), _const_spec(bias.shape),
                _const_spec(qgain.shape), _const_spec(kgain.shape), pos, pos, pos]
    return pl.pallas_call(
        _in_proj_kernel, grid=(T // tm,), in_specs=in_specs, out_specs=out_specs, out_shape=out_shapes,
        compiler_params=pltpu.CompilerParams(dimension_semantics=("parallel",), vmem_limit_bytes=VMEM_LIMIT),
        name="in_proj",
    )(x, gain, w_att, w_m, w_g, w_if, bias, qgain, kgain, rc, rs1, rs2)


def _attention_kernel(sinks_ref, q_ref, kk_ref, kkp_ref, vv_ref, vvp_ref, o_ref, *, nb):
    i = pl.program_id(0)
    first = (i % nb) == 0
    blk = ATT_BLOCK
    kk = jnp.concatenate([kkp_ref[...], kk_ref[...]], axis=0)
    vv = jnp.concatenate([vvp_ref[...], vv_ref[...]], axis=0)
    lane = lax.broadcasted_iota(jnp.int32, (2 * blk, LANES), 1)
    low = lane < ATT_HEAD_DIM
    zero = jnp.zeros((2 * blk, LANES), BF16)

    qq = lax.broadcasted_iota(jnp.int32, (blk, 2 * blk), 0)
    kpos = lax.broadcasted_iota(jnp.int32, (blk, 2 * blk), 1)
    valid = (kpos > qq) & (kpos <= qq + blk) & (kpos >= jnp.where(first, blk, 0))

    for kv in range(ATT_KV_HEADS):
        k_lo = kk[:, kv * LANES:(kv + 1) * LANES]
        k_hi = kk[:, (1 - kv) * LANES:(2 - kv) * LANES]
        v_lo = vv[:, kv * LANES:(kv + 1) * LANES]
        v_hi = vv[:, (1 - kv) * LANES:(2 - kv) * LANES]
        kmat = jnp.concatenate([jnp.where(low, k_lo, zero), jnp.where(low, zero, k_hi)], axis=0)
        vmat = jnp.concatenate([jnp.where(low, v_lo, zero), jnp.where(low, zero, v_hi)], axis=0)
        for j in range(2):
            col = 2 * kv + j
            s = _dot_nt(q_ref[:, col * LANES:(col + 1) * LANES], kmat)
            probs = []
            for p in range(2):
                sink = sinks_ref[2 * col + p]
                sp = jnp.where(valid, s[:, p * 2 * blk:(p + 1) * 2 * blk], NEG_BIG)
                m = jnp.maximum(jnp.max(sp, axis=-1, keepdims=True), sink)
                e = jnp.exp(sp - m)
                den = jnp.sum(e, axis=-1, keepdims=True) + jnp.exp(sink - m)
                probs.append((e / den).astype(BF16))
            pm = jnp.concatenate(probs, axis=1)
            o_ref[:, col * LANES:(col + 1) * LANES] = _dot(pm, vmat).astype(BF16)


def _attention(sinks, qa, kk, vv, *, seq):
    T = qa.shape[0]
    nb = seq // ATT_BLOCK
    own = lambda w: pl.BlockSpec((ATT_BLOCK, w), lambda i, *_: (i, 0))
    prev = lambda w: pl.BlockSpec((ATT_BLOCK, w), lambda i, *_: (jnp.maximum(i - 1, 0), 0))
    grid_spec = pltpu.PrefetchScalarGridSpec(
        num_scalar_prefetch=1, grid=(T // ATT_BLOCK,),
        in_specs=[own(ATT_Q_W), own(2 * ATT_KV_W), prev(2 * ATT_KV_W), own(2 * ATT_KV_W), prev(2 * ATT_KV_W)],
        out_specs=own(ATT_Q_W))
    return pl.pallas_call(
        functools.partial(_attention_kernel, nb=nb), grid_spec=grid_spec,
        out_shape=jax.ShapeDtypeStruct((T, ATT_Q_W), BF16),
        compiler_params=pltpu.CompilerParams(dimension_semantics=("parallel",), vmem_limit_bytes=VMEM_LIMIT),
        name="attention",
    )(sinks, qa, kk, kk, vv, vv)


def _split3(x):
    h1 = x.astype(BF16)
    r1 = x - h1.astype(F32)
    h2 = r1.astype(BF16)
    h3 = (r1 - h2.astype(F32)).astype(BF16)
    return h1, h2, h3


def _mlstm_kernel(q_ref, k_ref, v_ref, gcol_ref, grow_ref, osig_ref, gain_ref, o_ref, state_ref, m_ref):
    L = M_CHUNK
    PAIR_V = 2 * M_V_DIM
    SW = PAIR_V + LANES

    @pl.when(pl.program_id(1) == 0)
    def _():
        state_ref[...] = jnp.zeros(state_ref.shape, F32)
        m_ref[...] = jnp.zeros(m_ref.shape, F32)

    row = lax.broadcasted_iota(jnp.int32, (L, L), 0)
    col = lax.broadcasted_iota(jnp.int32, (L, L), 1)
    causal = col <= row
    tril = jnp.where(causal, 1.0, 0.0).astype(BF16)
    triu = jnp.where(row <= col, 1.0, 0.0).astype(BF16)
    gcol = gcol_ref[...]
    grow = grow_ref[...]
    bc = sum(_dot(tril, t) for t in _split3(gcol))
    br = sum(_dot(t, triu) for t in _split3(grow))

    lane_v = lax.broadcasted_iota(jnp.int32, (L, PAIR_V), 1)
    lane_s = lax.broadcasted_iota(jnp.int32, (L, SW), 1)
    row_s = lax.broadcasted_iota(jnp.int32, (L, SW), 0)
    first_v = lane_v < M_V_DIM
    low = col < M_QK_DIM
    rows_first = row_s < M_QK_DIM
    head_of_col = jnp.where(lane_s < PAIR_V, lane_s // M_V_DIM, lane_s - PAIR_V)
    keep = head_of_col == jnp.where(rows_first, 0, 1)
    zero_k = jnp.zeros((L, LANES), BF16)
    zero_v = jnp.zeros((L, PAIR_V), BF16)
    m_all = m_ref[...]

    for j in range(M_HEADS // 2):
        q2 = q_ref[:, j * LANES:(j + 1) * LANES]
        k2 = k_ref[:, j * LANES:(j + 1) * LANES]
        v2 = v_ref[:, j * PAIR_V:(j + 1) * PAIR_V]
        kmat = jnp.concatenate([jnp.where(low, k2, zero_k), jnp.where(low, zero_k, k2)], axis=0)
        vmat = jnp.concatenate([jnp.where(first_v, v2, zero_v), jnp.where(first_v, zero_v, v2)], axis=0)
        s_qk = _dot_nt(q2, kmat)
        state = state_ref[j]
        inter = _dot(q2, state.astype(BF16))

        ws, a_ints, m_ts, e_cols, a_rows, s_rows = [], [], [], [], [], []
        for hh in range(2):
            h = 2 * j + hh
            b_col = bc[:, M_HEADS + h:M_HEADS + h + 1]
            li_col = gcol[:, h:h + 1]
            b_row = br[M_HEADS + h:M_HEADS + h + 1, :]
            li_row = grow[h:h + 1, :]
            m_prev = m_all[h:h + 1, 0:1]
            log_d = jnp.where(causal, b_col - b_row + li_row, NEG_BIG)
            m_t = jnp.maximum(b_col + m_prev, jnp.max(log_d, axis=-1, keepdims=True))
            ws.append(jnp.exp(log_d - m_t) * s_qk[:, hh * L:(hh + 1) * L])
            a_ints.append(jnp.exp(b_col + m_prev - m_t))
            m_ts.append(m_t)
            g = b_col[L - 1:L, :]
            m_loc = jnp.max(g - b_row + li_row, axis=-1, keepdims=True)
            e_cols.append(jnp.exp(g - b_col + li_col - m_loc))
            m_new = jnp.maximum(g + m_prev, m_loc)
            a_rows.append(jnp.exp(g + m_prev - m_new))
            s_rows.append(jnp.exp(m_loc - m_new))
            m_ref[h:h + 1, :] = jnp.broadcast_to(m_new, (1, LANES))

        num = _dot(jnp.concatenate(ws, axis=1).astype(BF16), vmat)
        for hh in range(2):
            h = 2 * j + hh
            num_h = num[:, hh * M_V_DIM:(hh + 1) * M_V_DIM] + a_ints[hh] * inter[:, hh * M_V_DIM:(hh + 1) * M_V_DIM]
            qn = inter[:, PAIR_V + hh:PAIR_V + hh + 1]
            den = jnp.sum(ws[hh], axis=-1, keepdims=True) + a_ints[hh] * qn
            hv = num_h / jnp.maximum(jnp.abs(den), jnp.exp(-m_ts[hh]))
            inv = lax.rsqrt(jnp.mean(hv * hv, axis=-1, keepdims=True) + EPS)
            sl = slice(h * M_V_DIM, (h + 1) * M_V_DIM)
            o_ref[:, sl] = ((hv * inv) * gain_ref[:, sl] * osig_ref[:, sl].astype(F32)).astype(BF16)

        ve = (v2.astype(F32) * jnp.where(first_v, e_cols[0], e_cols[1])).astype(BF16)
        e_pad = jnp.where(col == 0, e_cols[0], jnp.where(col == 1, e_cols[1], 0.0)).astype(BF16)
        d_state = _dot_tn(k2, jnp.concatenate([ve, e_pad], axis=1))
        a_r = jnp.where(rows_first, a_rows[0], a_rows[1])
        s_r = jnp.where(rows_first, s_rows[0], s_rows[1])
        state_ref[j] = a_r * state + jnp.where(keep, s_r * d_state, 0.0)


def _mlstm(qm, km, vm, gcol, grow, osig, gain, *, seq):
    T = qm.shape[0]
    L = M_CHUNK
    nc = seq // L
    row = lambda w: pl.BlockSpec((L, w), lambda b, c: (b * nc + c, 0))
    return pl.pallas_call(
        _mlstm_kernel, grid=(T // seq, nc),
        in_specs=[row(M_QK_W), row(M_QK_W), row(M_V_W), row(LANES),
                  pl.BlockSpec((2 * M_HEADS, L), lambda b, c: (0, b * nc + c)),
                  row(M_V_W), _const_spec(gain.shape)],
        out_specs=row(M_V_W),
        out_shape=jax.ShapeDtypeStruct((T, M_V_W), BF16),
        scratch_shapes=[pltpu.VMEM((M_HEADS // 2, 2 * M_QK_DIM, 2 * M_V_DIM + LANES), F32),
                        pltpu.VMEM((2 * M_HEADS, LANES), F32)],
        compiler_params=pltpu.CompilerParams(dimension_semantics=("parallel", "arbitrary"),
                                             vmem_limit_bytes=VMEM_LIMIT),
        name="mlstm",
    )(qm, km, vm, gcol, grow, osig, gain)


def _merge_kernel(x_ref, att_ref, hm_ref, sg_ref, wa_ref, wm_ref, wo_ref, o_ref):
    D = x_ref.shape[1]
    a = _dot(att_ref[...], wa_ref[...])
    m = _dot(hm_ref[...], wm_ref[...])
    mixed = sg_ref[:, :D].astype(F32) * a + sg_ref[:, D:].astype(F32) * m
    o_ref[...] = x_ref[...] + _dot(mixed.astype(BF16), wo_ref[...])


def _merge(x, att, hm, sg, wa, wm, wo, *, tm):
    T, D = x.shape
    row = lambda w: pl.BlockSpec((tm, w), lambda i: (i, 0))
    return pl.pallas_call(
        _merge_kernel, grid=(T // tm,),
        in_specs=[row(D), row(ATT_Q_W), row(M_V_W), row(2 * D),
                  _const_spec(wa.shape), _const_spec(wm.shape), _const_spec(wo.shape)],
        out_specs=row(D), out_shape=jax.ShapeDtypeStruct((T, D), F32),
        compiler_params=pltpu.CompilerParams(dimension_semantics=("parallel",), vmem_limit_bytes=VMEM_LIMIT),
        name="merge",
    )(x, att, hm, sg, wa, wm, wo)


def _ffn_kernel(x_ref, gain_ref, w1_ref, w2_ref, o_ref, *, n_chunks):
    x = x_ref[...]
    inv = lax.rsqrt(jnp.mean(x * x, axis=-1, keepdims=True) + EPS)
    h = ((x * inv) * gain_ref[...]).astype(BF16)
    fc = w1_ref.shape[1] // n_chunks
    acc = x
    for c in range(n_chunks):
        u = jnp.maximum(_dot(h, w1_ref[:, c * fc:(c + 1) * fc]), 0.0)
        acc = acc + _dot((u * u).astype(BF16), w2_ref[c * fc:(c + 1) * fc, :])
    o_ref[...] = acc


def _ffn(x, gain, w1, w2, *, tm):
    T, D = x.shape
    row = pl.BlockSpec((tm, D), lambda i: (i, 0))
    return pl.pallas_call(
        functools.partial(_ffn_kernel, n_chunks=4), grid=(T // tm,),
        in_specs=[row, _const_spec(gain.shape), _const_spec(w1.shape), _const_spec(w2.shape)],
        out_specs=row, out_shape=jax.ShapeDtypeStruct((T, D), F32),
        compiler_params=pltpu.CompilerParams(dimension_semantics=("parallel",), vmem_limit_bytes=VMEM_LIMIT),
        name="ffn",
    )(x, gain, w1, w2)


def _rope_tables(seq):
    half = ROPE_DIM // 2
    pos = jnp.arange(seq, dtype=F32)
    inv_freq = ROPE_THETA ** (-jnp.arange(0, ROPE_DIM, 2, dtype=F32) / ROPE_DIM)
    ang = pos[:, None] * inv_freq[None, :]
    cos, sin = jnp.cos(ang), jnp.sin(ang)
    pad = jnp.zeros((seq, ATT_HEAD_DIM - ROPE_DIM), F32)
    zeros = jnp.zeros((seq, half), F32)
    rc = jnp.concatenate([cos, cos, pad + 1.0], axis=1)
    rs1 = jnp.concatenate([-sin, zeros, pad], axis=1)
    rs2 = jnp.concatenate([zeros, sin, pad], axis=1)
    tile = lambda t: jnp.concatenate([t, t], axis=1)
    return tile(rc), tile(rs1), tile(rs2)


def _split_w_in(w_in):
    offs = np.concatenate([[0], np.cumsum(IN_WIDTHS)])
    seg = lambda i: w_in[:, offs[i]:offs[i + 1]]
    q_a, k_a, v_a, q_m, k_m, v_m, o_m, if_m, gates = (seg(i) for i in range(len(IN_WIDTHS)))
    swap = lambda w: jnp.concatenate([w[:, ATT_HEAD_DIM:], w[:, :ATT_HEAD_DIM]], axis=1)
    w_att = jnp.concatenate([q_a, k_a, swap(k_a), v_a, swap(v_a)], axis=1).astype(BF16)
    w_m = jnp.concatenate([q_m, k_m, v_m, o_m], axis=1).astype(BF16)
    w_if = jnp.pad(if_m, ((0, 0), (0, LANES - 2 * M_HEADS))).astype(BF16)
    return w_att, w_m, gates.astype(BF16), w_if


def kernel(x, norm_mix, w_in, att_q_norm, att_k_norm, att_sinks, m_gate_bias, m_head_norm,
           w_att_branch, w_m_branch, w_out, norm_ffn, w_ff1, w_ff2):
    B, S, D = x.shape
    T = B * S
    depth = w_in.shape[0]
    tm = min(512, S)
    rc, rs1, rs2 = _rope_tables(S)
    xt = x.reshape(T, D)
    for l in range(depth):
        w_att, w_m, w_g, w_if = _split_w_in(w_in[l])
        bias = jnp.pad(m_gate_bias[l], (0, LANES - 2 * M_HEADS)).reshape(1, LANES)
        qgain = jnp.tile(att_q_norm[l], 2).reshape(1, LANES)
        kgain = jnp.tile(att_k_norm[l], 2).reshape(1, LANES)
        qa, kk, vv, qm, km, vm, osig, sg, gcol, grow = _in_proj(
            xt, norm_mix[l].reshape(1, D), w_att, w_m, w_g, w_if, bias, qgain, kgain, rc, rs1, rs2, seq=S, tm=tm)
        att = _attention(att_sinks[l], qa, kk, vv, seq=S)
        hm = _mlstm(qm, km, vm, gcol, grow, osig, m_head_norm[l].reshape(1, M_V_W), seq=S)
        xt = _merge(xt, att, hm, sg, w_att_branch[l].astype(BF16), w_m_branch[l].astype(BF16),
                    w_out[l].astype(BF16), tm=tm)
        xt = _ffn(xt, norm_ffn[l].reshape(1, D), w_ff1[l].astype(BF16), w_ff2[l].astype(BF16), tm=tm)
    return xt.reshape(B, S, D)
```

```python
import functools

import jax
import jax.numpy as jnp
import numpy as np
from jax import lax
from jax.experimental import pallas as pl
from jax.experimental.pallas import tpu as pltpu

D_MODEL = 1024
ATT_HEADS = 8
ATT_KV_HEADS = 2
ATT_GROUP = ATT_HEADS // ATT_KV_HEADS
ATT_HEAD_DIM = 64
ATT_BLOCK = 128
ROPE_DIM = ATT_HEAD_DIM // 4
ROPE_HALF = ROPE_DIM // 2
ROPE_THETA = 500000.0
M_HEADS = 4
M_QK_DIM = 64
M_V_DIM = 128
M_CHUNK = 128
D_FF = 4 * D_MODEL
EPS = 1e-6

ATT_Q_W = ATT_HEADS * ATT_HEAD_DIM
ATT_KV_W = ATT_KV_HEADS * ATT_HEAD_DIM
M_QK_W = M_HEADS * M_QK_DIM
M_V_W = M_HEADS * M_V_DIM
IN_WIDTHS = (ATT_Q_W, ATT_KV_W, ATT_KV_W, M_QK_W, M_QK_W, M_V_W, M_V_W, 2 * M_HEADS, 2 * D_MODEL)
FM_WIDTHS = (ATT_Q_W, ATT_KV_W, M_QK_W, M_V_W, M_V_W)
FM_OFFS = tuple(int(o) for o in np.concatenate([[0], np.cumsum(FM_WIDTHS)]))
LANES = 128
SUBLANES_BF16 = 16
TM_WIDTHS = (ATT_KV_W, M_QK_W, 2 * D_MODEL, LANES)
TM_OFFS = tuple(int(o) for o in np.concatenate([[0], np.cumsum(TM_WIDTHS)]))

NEG_BIG = -1e30
VMEM_LIMIT = 56 * 1024 * 1024
TOKEN_TILE = 512

BF16 = jnp.bfloat16
F32 = jnp.float32


def _dot(a, b):
    return jnp.dot(a, b, preferred_element_type=F32)


def _dot_nt(a, b):
    return lax.dot_general(a, b, (((1,), (1,)), ((), ())), preferred_element_type=F32)


def _dot_tn(a, b):
    return lax.dot_general(a, b, (((0,), (0,)), ((), ())), preferred_element_type=F32)


def _sigmoid(x):
    return 1.0 / (1.0 + jnp.exp(-x))


def _log_sigmoid(x):
    return jnp.minimum(x, 0.0) - jnp.log(1.0 + jnp.exp(-jnp.abs(x)))


def _const_spec(shape):
    return pl.BlockSpec(shape, lambda *_: (0,) * len(shape))


def _params(*sem):
    return pltpu.CompilerParams(dimension_semantics=sem, vmem_limit_bytes=VMEM_LIMIT)


def _in_proj_kernel(x_ref, gain_ref, w_fm_ref, w_tm_ref, bias_ref, qgain_ref, kgain_ref,
                    cos_ref, sin_ref, rc_ref, rs1_ref, rs2_ref,
                    qa_ref, va_ref, qm_ref, vm_ref, om_ref, ka_ref, km_ref, sg_ref, gcol_ref, grow_ref):
    x = x_ref[...]
    inv = lax.rsqrt(jnp.mean(x * x, axis=-1, keepdims=True) + EPS)
    h = ((x * inv) * gain_ref[...]).astype(BF16)
    tm = x.shape[0]

    zt = _dot_nt(w_fm_ref[...], h)
    cos, sin = cos_ref[...], sin_ref[...]
    qgain = qgain_ref[...]
    scale = ATT_HEAD_DIM ** -0.5
    for hd in range(ATT_HEADS):
        z = zt[hd * ATT_HEAD_DIM:(hd + 1) * ATT_HEAD_DIM, :]
        inv_h = lax.rsqrt(jnp.mean(z * z, axis=0, keepdims=True) + EPS)
        y = (z * inv_h) * (qgain * scale)
        x1, x2 = y[:ROPE_HALF], y[ROPE_HALF:ROPE_DIM]
        y = jnp.concatenate([x1 * cos - x2 * sin, x2 * cos + x1 * sin, y[ROPE_DIM:]], axis=0)
        qa_ref[hd * ATT_HEAD_DIM:(hd + 1) * ATT_HEAD_DIM, :] = y.astype(BF16)
    va_ref[...] = zt[FM_OFFS[1]:FM_OFFS[2], :].astype(BF16)
    qm_ref[...] = (zt[FM_OFFS[2]:FM_OFFS[3], :] * (M_QK_DIM ** -0.5)).astype(BF16)
    vm_ref[...] = zt[FM_OFFS[3]:FM_OFFS[4], :].astype(BF16)
    om_ref[...] = _sigmoid(zt[FM_OFFS[4]:FM_OFFS[5], :]).astype(BF16)

    zn = _dot(h, w_tm_ref[...])
    lane = lax.broadcasted_iota(jnp.int32, (tm, LANES), 1)
    low = lane < ATT_HEAD_DIM
    zk = zn[:, TM_OFFS[0]:TM_OFFS[1]]
    s = zk * zk
    r0 = jnp.sum(jnp.where(low, s, 0.0), axis=-1, keepdims=True)
    r1 = jnp.sum(jnp.where(low, 0.0, s), axis=-1, keepdims=True)
    inv_k = lax.rsqrt(jnp.where(low, r0, r1) * (1.0 / ATT_HEAD_DIM) + EPS)
    y = (zk * inv_k) * kgain_ref[...]
    up = pltpu.roll(y, LANES - ROPE_HALF, 1)
    dn = pltpu.roll(y, ROPE_HALF, 1)
    ka_ref[...] = (y * rc_ref[...] + up * rs1_ref[...] + dn * rs2_ref[...]).astype(BF16)
    km_ref[...] = zn[:, TM_OFFS[1]:TM_OFFS[2]].astype(BF16)
    sg_ref[...] = _sigmoid(zn[:, TM_OFFS[2]:TM_OFFS[3]]).astype(BF16)
    zi = zn[:, TM_OFFS[3]:TM_OFFS[4]] + bias_ref[...]
    g = jnp.where(lane < M_HEADS, zi, jnp.where(lane < 2 * M_HEADS, _log_sigmoid(zi), 0.0))
    gcol_ref[...] = g
    grow_ref[...] = g.T[:2 * M_HEADS, :]


def _in_proj(x, gain, w_fm, w_tm, bias, qgain, kgain, cos_t, sin_t, rc, rs1, rs2, *, seq, tm):
    T, D = x.shape
    n_pos = seq // tm
    row = lambda w: pl.BlockSpec((tm, w), lambda i: (i, 0))
    colm = lambda w: pl.BlockSpec((w, tm), lambda i: (0, i))
    pos = pl.BlockSpec((tm, LANES), lambda i: (i % n_pos, 0))
    pos_t = pl.BlockSpec((ROPE_HALF, tm), lambda i: (0, i % n_pos))
    fm = lambda w: jax.ShapeDtypeStruct((w, T), BF16)
    out_shapes = (fm(ATT_Q_W), fm(ATT_KV_W), fm(M_QK_W), fm(M_V_W), fm(M_V_W),
                  jax.ShapeDtypeStruct((T, ATT_KV_W), BF16),
                  jax.ShapeDtypeStruct((T, M_QK_W), BF16),
                  jax.ShapeDtypeStruct((T, 2 * D), BF16),
                  jax.ShapeDtypeStruct((T, LANES), F32),
                  jax.ShapeDtypeStruct((2 * M_HEADS, T), F32))
    out_specs = (colm(ATT_Q_W), colm(ATT_KV_W), colm(M_QK_W), colm(M_V_W), colm(M_V_W),
                 row(ATT_KV_W), row(M_QK_W), row(2 * D), row(LANES), colm(2 * M_HEADS))
    in_specs = [row(D), _const_spec(gain.shape), _const_spec(w_fm.shape), _const_spec(w_tm.shape),
                _const_spec(bias.shape), _const_spec(qgain.shape), _const_spec(kgain.shape),
                pos_t, pos_t, pos, pos, pos]
    return pl.pallas_call(
        _in_proj_kernel, grid=(T // tm,), in_specs=in_specs, out_specs=out_specs, out_shape=out_shapes,
        compiler_params=_params("parallel"), name="in_proj",
    )(x, gain, w_fm, w_tm, bias, qgain, kgain, cos_t, sin_t, rc, rs1, rs2)


def _attention_kernel(sinks_ref, q_ref, k_ref, kp_ref, v_ref, vp_ref, o_ref, *, steps_per_seq):
    first = (pl.program_id(0) % steps_per_seq) == 0
    blk, hd_dim = ATT_BLOCK, ATT_HEAD_DIM
    key = lax.broadcasted_iota(jnp.int32, (blk, blk), 0)
    qry = lax.broadcasted_iota(jnp.int32, (blk, blk), 1)
    from_prev = key > qry
    zeros = jnp.zeros((hd_dim, blk), BF16)
    n_blocks = q_ref.shape[1] // blk
    for b in range(n_blocks):
        cur = slice(b * blk, (b + 1) * blk)
        old = slice((b - 1) * blk, b * blk)
        k_cat = jnp.concatenate([kp_ref[...] if b == 0 else k_ref[old, :], k_ref[cur, :]], axis=0)
        v_prev = vp_ref[...] if b == 0 else v_ref[:, old]
        v_cur = v_ref[:, cur]
        for kv in range(ATT_KV_HEADS):
            rows = slice(kv * hd_dim, (kv + 1) * hd_dim)
            v_cat = jnp.concatenate([v_prev[rows, :], v_cur[rows, :]], axis=1)
            q_pad = []
            for g in range(ATT_GROUP):
                hd = kv * ATT_GROUP + g
                qh = q_ref[hd * hd_dim:(hd + 1) * hd_dim, cur]
                q_pad.append(jnp.concatenate([qh, zeros] if kv == 0 else [zeros, qh], axis=0))
            s_all = _dot(k_cat, jnp.concatenate(q_pad, axis=1))
            p_all, r_den = [], []
            for g in range(ATT_GROUP):
                sink = sinks_ref[kv * ATT_GROUP + g]
                s_prev = s_all[:blk, g * blk:(g + 1) * blk]
                if b == 0:
                    s_prev = s_prev + jnp.where(first, NEG_BIG, 0.0)
                s = jnp.where(from_prev, s_prev, s_all[blk:, g * blk:(g + 1) * blk])
                m = jnp.maximum(jnp.max(s, axis=0, keepdims=True), sink)
                p = jnp.exp(s - m)
                r_den.append(1.0 / (jnp.sum(p, axis=0, keepdims=True) + jnp.exp(sink - m)))
                p_all.append(jnp.concatenate([jnp.where(from_prev, p, 0.0), jnp.where(from_prev, 0.0, p)],
                                             axis=0).astype(BF16))
            o_all = _dot(v_cat, jnp.concatenate(p_all, axis=1))
            for g in range(ATT_GROUP):
                hd = kv * ATT_GROUP + g
                o_ref[hd * hd_dim:(hd + 1) * hd_dim, cur] = (o_all[:, g * blk:(g + 1) * blk] * r_den[g]).astype(BF16)


def _attention(sinks, qa_t, ka, va_t, *, seq, tq):
    T = ka.shape[0]
    per = tq // ATT_BLOCK
    prev_blk = lambda i: jnp.maximum(i * per - 1, 0)
    grid_spec = pltpu.PrefetchScalarGridSpec(
        num_scalar_prefetch=1, grid=(T // tq,),
        in_specs=[pl.BlockSpec((ATT_Q_W, tq), lambda i, *_: (0, i)),
                  pl.BlockSpec((tq, ATT_KV_W), lambda i, *_: (i, 0)),
                  pl.BlockSpec((ATT_BLOCK, ATT_KV_W), lambda i, *_: (prev_blk(i), 0)),
                  pl.BlockSpec((ATT_KV_W, tq), lambda i, *_: (0, i)),
                  pl.BlockSpec((ATT_KV_W, ATT_BLOCK), lambda i, *_: (0, prev_blk(i)))],
        out_specs=pl.BlockSpec((ATT_Q_W, tq), lambda i, *_: (0, i)))
    return pl.pallas_call(
        functools.partial(_attention_kernel, steps_per_seq=seq // tq), grid_spec=grid_spec,
        out_shape=jax.ShapeDtypeStruct((ATT_Q_W, T), BF16),
        compiler_params=_params("parallel"), name="attention",
    )(sinks, qa_t, ka, ka, va_t, va_t)


def _split3(x):
    h1 = x.astype(BF16)
    r1 = x - h1.astype(F32)
    h2 = r1.astype(BF16)
    h3 = (r1 - h2.astype(F32)).astype(BF16)
    return h1, h2, h3


def _mlstm_kernel(q_ref, k_ref, v_ref, osig_ref, gcol_ref, grow_ref, gain_ref, o_ref, state_ref, m_ref):
    L = M_CHUNK
    NROW = state_ref.shape[1]

    @pl.when(pl.program_id(1) == 0)
    def _():
        state_ref[...] = jnp.zeros(state_ref.shape, F32)
        m_ref[...] = jnp.zeros(m_ref.shape, F32)

    src = lax.broadcasted_iota(jnp.int32, (L, L), 0)
    dst = lax.broadcasted_iota(jnp.int32, (L, L), 1)
    causal = src <= dst
    tril = jnp.where(dst <= src, 1.0, 0.0).astype(BF16)
    triu = jnp.where(causal, 1.0, 0.0).astype(BF16)
    zeros = jnp.zeros((M_QK_DIM, L), BF16)

    for c in range(q_ref.shape[1] // L):
        cur = slice(c * L, (c + 1) * L)
        gcol = gcol_ref[cur, :]
        grow = grow_ref[:, cur]
        bc = sum(_dot(tril, t) for t in _split3(gcol))
        br = sum(_dot(t, triu) for t in _split3(grow))
        r_col = gcol - pltpu.roll(bc, LANES - M_HEADS, 1)
        for h in range(M_HEADS):
            k2 = k_ref[cur, (h // 2) * LANES:(h // 2 + 1) * LANES]
            qh = q_ref[h * M_QK_DIM:(h + 1) * M_QK_DIM, cur]
            q_pad = jnp.concatenate([qh, zeros] if h % 2 == 0 else [zeros, qh], axis=0)
            vh = v_ref[h * M_V_DIM:(h + 1) * M_V_DIM, cur]
            b_row = br[M_HEADS + h:M_HEADS + h + 1, :]
            li_row = grow[h:h + 1, :]
            m_prev = m_ref[h:h + 1, :]

            s_t = _dot(k2, q_pad)
            log_d = jnp.where(causal, jnp.broadcast_to(r_col[:, h:h + 1], (L, L)) + b_row, NEG_BIG)
            m_t = jnp.maximum(b_row + m_prev, jnp.max(log_d, axis=0, keepdims=True))
            w_t = jnp.exp(log_d - m_t) * s_t
            a_int = jnp.exp(b_row + m_prev - m_t)
            state = state_ref[h]
            inter = _dot(state.astype(BF16), q_pad)
            num = _dot(vh, w_t.astype(BF16)) + a_int * inter[:M_V_DIM]
            den = jnp.sum(w_t, axis=0, keepdims=True) + a_int * inter[M_V_DIM:M_V_DIM + 1]
            hv = num / jnp.maximum(jnp.abs(den), jnp.exp(-m_t))
            inv = lax.rsqrt(jnp.mean(hv * hv, axis=0, keepdims=True) + EPS)
            rows = slice(h * M_V_DIM, (h + 1) * M_V_DIM)
            o_ref[rows, cur] = ((hv * inv) * gain_ref[rows, :] * osig_ref[rows, cur].astype(F32)).astype(BF16)

            g = b_row[:, L - 1:L]
            w_end = g - b_row + li_row
            m_loc = jnp.max(w_end, axis=1, keepdims=True)
            e_row = jnp.exp(w_end - m_loc)
            m_new = jnp.maximum(g + m_prev, m_loc)
            lhs = jnp.concatenate([(vh.astype(F32) * e_row).astype(BF16),
                                   jnp.broadcast_to(e_row, (NROW - M_V_DIM, L)).astype(BF16)], axis=0)
            state_ref[h] = jnp.exp(g + m_prev - m_new) * state + jnp.exp(m_loc - m_new) * _dot(lhs, k2)
            m_ref[h:h + 1, :] = m_new


def _mlstm(qm_t, km, vm_t, osig_t, gcol, grow, gain_b, *, seq, tc):
    T = km.shape[0]
    n = seq // tc
    row = lambda w: pl.BlockSpec((tc, w), lambda b, c: (b * n + c, 0))
    colm = lambda w: pl.BlockSpec((w, tc), lambda b, c: (0, b * n + c))
    return pl.pallas_call(
        _mlstm_kernel, grid=(T // seq, n),
        in_specs=[colm(M_QK_W), row(M_QK_W), colm(M_V_W), colm(M_V_W), row(LANES), colm(2 * M_HEADS),
                  _const_spec(gain_b.shape)],
        out_specs=colm(M_V_W),
        out_shape=jax.ShapeDtypeStruct((M_V_W, T), BF16),
        scratch_shapes=[pltpu.VMEM((M_HEADS, M_V_DIM + SUBLANES_BF16, LANES), F32),
                        pltpu.VMEM((2 * M_HEADS, LANES), F32)],
        compiler_params=_params("parallel", "arbitrary"), name="mlstm",
    )(qm_t, km, vm_t, osig_t, gcol, grow, gain_b)


def _merge_kernel(x_ref, att_ref, hm_ref, sg_ref, wa_ref, wm_ref, wo_ref, o_ref):
    D = x_ref.shape[1]
    a = _dot_tn(att_ref[...], wa_ref[...])
    m = _dot_tn(hm_ref[...], wm_ref[...])
    mixed = sg_ref[:, :D].astype(F32) * a + sg_ref[:, D:].astype(F32) * m
    o_ref[...] = x_ref[...] + _dot(mixed.astype(BF16), wo_ref[...])


def _merge(x, att_t, hm_t, sg, wa, wm, wo, *, tm):
    T, D = x.shape
    row = lambda w: pl.BlockSpec((tm, w), lambda i: (i, 0))
    colm = lambda w: pl.BlockSpec((w, tm), lambda i: (0, i))
    return pl.pallas_call(
        _merge_kernel, grid=(T // tm,),
        in_specs=[row(D), colm(ATT_Q_W), colm(M_V_W), row(2 * D),
                  _const_spec(wa.shape), _const_spec(wm.shape), _const_spec(wo.shape)],
        out_specs=row(D), out_shape=jax.ShapeDtypeStruct((T, D), F32),
        compiler_params=_params("parallel"), name="merge",
    )(x, att_t, hm_t, sg, wa, wm, wo)


def _ffn_kernel(x_ref, gain_ref, w1_ref, w2_ref, o_ref, *, n_chunks):
    x = x_ref[...]
    inv = lax.rsqrt(jnp.mean(x * x, axis=-1, keepdims=True) + EPS)
    h = ((x * inv) * gain_ref[...]).astype(BF16)
    fc = w1_ref.shape[1] // n_chunks
    acc = x
    for c in range(n_chunks):
        u = jnp.maximum(_dot(h, w1_ref[:, c * fc:(c + 1) * fc]), 0.0)
        acc = acc + _dot((u * u).astype(BF16), w2_ref[c * fc:(c + 1) * fc, :])
    o_ref[...] = acc


def _ffn(x, gain, w1, w2, *, tm):
    T, D = x.shape
    row = pl.BlockSpec((tm, D), lambda i: (i, 0))
    return pl.pallas_call(
        functools.partial(_ffn_kernel, n_chunks=4), grid=(T // tm,),
        in_specs=[row, _const_spec(gain.shape), _const_spec(w1.shape), _const_spec(w2.shape)],
        out_specs=row, out_shape=jax.ShapeDtypeStruct((T, D), F32),
        compiler_params=_params("parallel"), name="ffn",
    )(x, gain, w1, w2)


def _rope_tables(seq):
    pos = jnp.arange(seq, dtype=F32)
    inv_freq = ROPE_THETA ** (-jnp.arange(0, ROPE_DIM, 2, dtype=F32) / ROPE_DIM)
    ang = pos[:, None] * inv_freq[None, :]
    cos, sin = jnp.cos(ang), jnp.sin(ang)
    pad = jnp.zeros((seq, ATT_HEAD_DIM - ROPE_DIM), F32)
    zeros = jnp.zeros((seq, ROPE_HALF), F32)
    rc = jnp.concatenate([cos, cos, pad + 1.0], axis=1)
    rs1 = jnp.concatenate([-sin, zeros, pad], axis=1)
    rs2 = jnp.concatenate([zeros, sin, pad], axis=1)
    tile = lambda t: jnp.concatenate([t, t], axis=1)
    return cos.T, sin.T, tile(rc), tile(rs1), tile(rs2)


def _split_w_in(w_in):
    offs = np.concatenate([[0], np.cumsum(IN_WIDTHS)])
    seg = lambda i: w_in[:, offs[i]:offs[i + 1]]
    q_a, k_a, v_a, q_m, k_m, v_m, o_m, if_m, gates = (seg(i) for i in range(len(IN_WIDTHS)))
    w_fm = jnp.concatenate([q_a, v_a, q_m, v_m, o_m], axis=1).T.astype(BF16)
    if_pad = jnp.pad(if_m, ((0, 0), (0, LANES - 2 * M_HEADS)))
    w_tm = jnp.concatenate([k_a, k_m, gates, if_pad], axis=1).astype(BF16)
    return w_fm, w_tm


def kernel(x, norm_mix, w_in, att_q_norm, att_k_norm, att_sinks, m_gate_bias, m_head_norm,
           w_att_branch, w_m_branch, w_out, norm_ffn, w_ff1, w_ff2):
    B, S, D = x.shape
    T = B * S
    depth = w_in.shape[0]
    tm = min(TOKEN_TILE, S)
    cos_t, sin_t, rc, rs1, rs2 = _rope_tables(S)
    xt = x.reshape(T, D)
    for l in range(depth):
        w_fm, w_tm = _split_w_in(w_in[l])
        bias = jnp.pad(m_gate_bias[l], (0, LANES - 2 * M_HEADS)).reshape(1, LANES)
        qgain = jnp.broadcast_to(att_q_norm[l][:, None], (ATT_HEAD_DIM, tm))
        kgain = jnp.tile(att_k_norm[l], 2).reshape(1, LANES)
        mgain = jnp.broadcast_to(m_head_norm[l][:, None], (M_V_W, LANES))
        qa_t, va_t, qm_t, vm_t, osig_t, ka, km, sg, gcol, grow = _in_proj(
            xt, norm_mix[l].reshape(1, D), w_fm, w_tm, bias, qgain, kgain, cos_t, sin_t, rc, rs1, rs2, seq=S, tm=tm)
        att_t = _attention(att_sinks[l], qa_t, ka, va_t, seq=S, tq=tm)
        hm_t = _mlstm(qm_t, km, vm_t, osig_t, gcol, grow, mgain, seq=S, tc=tm)
        xt = _merge(xt, att_t, hm_t, sg, w_att_branch[l].astype(BF16), w_m_branch[l].astype(BF16),
                    w_out[l].astype(BF16), tm=tm)
        xt = _ffn(xt, norm_ffn[l].reshape(1, D), w_ff1[l].astype(BF16), w_ff2[l].astype(BF16), tm=tm)
    return xt.reshape(B, S, D)
```

```python
import functools

import jax
import jax.numpy as jnp
import numpy as np
from jax import lax
from jax.experimental import pallas as pl
from jax.experimental.pallas import tpu as pltpu

D_MODEL = 1024
ATT_HEADS = 8
ATT_KV_HEADS = 2
ATT_GROUP = ATT_HEADS // ATT_KV_HEADS
ATT_HEAD_DIM = 64
ATT_BLOCK = 128
ROPE_DIM = ATT_HEAD_DIM // 4
ROPE_HALF = ROPE_DIM // 2
ROPE_THETA = 500000.0
M_HEADS = 4
M_QK_DIM = 64
M_V_DIM = 128
M_CHUNK = 128
D_FF = 4 * D_MODEL
EPS = 1e-6

ATT_Q_W = ATT_HEADS * ATT_HEAD_DIM
ATT_KV_W = ATT_KV_HEADS * ATT_HEAD_DIM
M_QK_W = M_HEADS * M_QK_DIM
M_V_W = M_HEADS * M_V_DIM
IN_WIDTHS = (ATT_Q_W, ATT_KV_W, ATT_KV_W, M_QK_W, M_QK_W, M_V_W, M_V_W, 2 * M_HEADS, 2 * D_MODEL)
IN_OFFS = tuple(int(o) for o in np.concatenate([[0], np.cumsum(IN_WIDTHS)]))
LANES = 128
SUBLANES_BF16 = 16
FM_WIDTHS = (M_V_W, ATT_Q_W, ATT_KV_W, M_QK_W, M_V_W)
FM_OFFS = tuple(int(o) for o in np.concatenate([[0], np.cumsum(FM_WIDTHS)]))
TM_WIDTHS = (2 * D_MODEL, ATT_KV_W, M_QK_W, LANES)
TM_OFFS = tuple(int(o) for o in np.concatenate([[0], np.cumsum(TM_WIDTHS)]))

NEG_BIG = -1e30
VMEM_LIMIT = 56 * 1024 * 1024
TOKEN_TILE = 512

BF16 = jnp.bfloat16
F32 = jnp.float32


def _dot(a, b):
    return jnp.dot(a, b, preferred_element_type=F32)


def _dot_nt(a, b):
    return lax.dot_general(a, b, (((1,), (1,)), ((), ())), preferred_element_type=F32)


def _dot_tn(a, b):
    return lax.dot_general(a, b, (((0,), (0,)), ((), ())), preferred_element_type=F32)


def _sigmoid(x):
    return 1.0 / (1.0 + jnp.exp(-x))


def _log_sigmoid(x):
    return jnp.minimum(x, 0.0) - jnp.log(1.0 + jnp.exp(-jnp.abs(x)))


def _const_spec(shape):
    return pl.BlockSpec(shape, lambda *_: (0,) * len(shape))


def _layer_spec(arr, layer):
    return pl.BlockSpec((None,) + arr.shape[1:], lambda *_: (layer,) + (0,) * (arr.ndim - 1))


def _params(*sem):
    return pltpu.CompilerParams(dimension_semantics=sem, vmem_limit_bytes=VMEM_LIMIT)


def _in_proj_kernel(x_ref, gain_ref, w_fm_ref, w_tm_ref, bias_ref, qgain_ref, kgain_ref,
                    cos_ref, sin_ref, rc_ref, rs1_ref, rs2_ref,
                    qa_ref, va_ref, qm_ref, vm_ref, om_ref, ka_ref, km_ref, sg_ref, gcol_ref, grow_ref):
    x = x_ref[...]
    inv = lax.rsqrt(jnp.mean(x * x, axis=-1, keepdims=True) + EPS)
    h = ((x * inv) * gain_ref[...]).astype(BF16)
    tm = x.shape[0]

    sg_ref[...] = _sigmoid(_dot(h, w_tm_ref[:, TM_OFFS[0]:TM_OFFS[1]])).astype(BF16)
    om_ref[...] = _sigmoid(_dot_nt(w_fm_ref[FM_OFFS[0]:FM_OFFS[1], :], h)).astype(BF16)

    zq = _dot_nt(w_fm_ref[FM_OFFS[1]:FM_OFFS[2], :], h)
    cos, sin = cos_ref[...], sin_ref[...]
    qgain = qgain_ref[...] * (ATT_HEAD_DIM ** -0.5)
    for hd in range(ATT_HEADS):
        z = zq[hd * ATT_HEAD_DIM:(hd + 1) * ATT_HEAD_DIM, :]
        inv_h = lax.rsqrt(jnp.mean(z * z, axis=0, keepdims=True) + EPS)
        y = (z * inv_h) * qgain
        x1, x2 = y[:ROPE_HALF], y[ROPE_HALF:ROPE_DIM]
        y = jnp.concatenate([x1 * cos - x2 * sin, x2 * cos + x1 * sin, y[ROPE_DIM:]], axis=0)
        qa_ref[hd * ATT_HEAD_DIM:(hd + 1) * ATT_HEAD_DIM, :] = y.astype(BF16)

    zn = _dot(h, w_tm_ref[:, TM_OFFS[1]:TM_OFFS[4]])
    zr = _dot_nt(w_fm_ref[FM_OFFS[2]:FM_OFFS[5], :], h)
    va_ref[...] = zr[:ATT_KV_W, :].astype(BF16)
    qm_ref[...] = (zr[ATT_KV_W:ATT_KV_W + M_QK_W, :] * (M_QK_DIM ** -0.5)).astype(BF16)
    vm_ref[...] = zr[ATT_KV_W + M_QK_W:, :].astype(BF16)

    lane = lax.broadcasted_iota(jnp.int32, (tm, LANES), 1)
    low = lane < ATT_HEAD_DIM
    zk = zn[:, :ATT_KV_W]
    s = zk * zk
    r0 = jnp.sum(jnp.where(low, s, 0.0), axis=-1, keepdims=True)
    r1 = jnp.sum(jnp.where(low, 0.0, s), axis=-1, keepdims=True)
    inv_k = lax.rsqrt(jnp.where(low, r0, r1) * (1.0 / ATT_HEAD_DIM) + EPS)
    y = (zk * inv_k) * kgain_ref[...]
    up = pltpu.roll(y, LANES - ROPE_HALF, 1)
    dn = pltpu.roll(y, ROPE_HALF, 1)
    ka_ref[...] = (y * rc_ref[...] + up * rs1_ref[...] + dn * rs2_ref[...]).astype(BF16)
    km_ref[...] = zn[:, ATT_KV_W:ATT_KV_W + M_QK_W].astype(BF16)
    zi = zn[:, ATT_KV_W + M_QK_W:] + bias_ref[...]
    g = jnp.where(lane < M_HEADS, zi, jnp.where(lane < 2 * M_HEADS, _log_sigmoid(zi), 0.0))
    gcol_ref[...] = g
    grow_ref[...] = g.T[:2 * M_HEADS, :]


def _in_proj(x, gain, w_fm, w_tm, bias, qgain, kgain, cos_t, sin_t, rc, rs1, rs2, *, layer, seq, tm):
    T, D = x.shape
    n_pos = seq // tm
    row = lambda w: pl.BlockSpec((tm, w), lambda i: (i, 0))
    colm = lambda w: pl.BlockSpec((w, tm), lambda i: (0, i))
    pos = pl.BlockSpec((tm, LANES), lambda i: (i % n_pos, 0))
    pos_t = pl.BlockSpec((ROPE_HALF, tm), lambda i: (0, i % n_pos))
    fm = lambda w: jax.ShapeDtypeStruct((w, T), BF16)
    out_shapes = (fm(ATT_Q_W), fm(ATT_KV_W), fm(M_QK_W), fm(M_V_W), fm(M_V_W),
                  jax.ShapeDtypeStruct((T, ATT_KV_W), BF16),
                  jax.ShapeDtypeStruct((T, M_QK_W), BF16),
                  jax.ShapeDtypeStruct((T, 2 * D), BF16),
                  jax.ShapeDtypeStruct((T, LANES), F32),
                  jax.ShapeDtypeStruct((2 * M_HEADS, T), F32))
    out_specs = (colm(ATT_Q_W), colm(ATT_KV_W), colm(M_QK_W), colm(M_V_W), colm(M_V_W),
                 row(ATT_KV_W), row(M_QK_W), row(2 * D), row(LANES), colm(2 * M_HEADS))
    in_specs = [row(D)] + [_layer_spec(a, layer) for a in (gain, w_fm, w_tm, bias, qgain, kgain)] + [
        pos_t, pos_t, pos, pos, pos]
    return pl.pallas_call(
        _in_proj_kernel, grid=(T // tm,), in_specs=in_specs, out_specs=out_specs, out_shape=out_shapes,
        compiler_params=_params("parallel"), name="in_proj",
    )(x, gain, w_fm, w_tm, bias, qgain, kgain, cos_t, sin_t, rc, rs1, rs2)


def _attention_kernel(sinks_ref, q_ref, k_ref, kp_ref, v_ref, vp_ref, o_ref, *, layer, steps_per_seq):
    first = (pl.program_id(0) % steps_per_seq) == 0
    blk, hd_dim = ATT_BLOCK, ATT_HEAD_DIM
    key = lax.broadcasted_iota(jnp.int32, (blk, blk), 0)
    qry = lax.broadcasted_iota(jnp.int32, (blk, blk), 1)
    from_prev = key > qry
    zeros = jnp.zeros((hd_dim, blk), BF16)
    n_blocks = q_ref.shape[1] // blk
    for b in range(n_blocks):
        cur = slice(b * blk, (b + 1) * blk)
        old = slice((b - 1) * blk, b * blk)
        k_cat = jnp.concatenate([kp_ref[...] if b == 0 else k_ref[old, :], k_ref[cur, :]], axis=0)
        v_prev = vp_ref[...] if b == 0 else v_ref[:, old]
        v_cur = v_ref[:, cur]
        for kv in range(ATT_KV_HEADS):
            rows = slice(kv * hd_dim, (kv + 1) * hd_dim)
            v_cat = jnp.concatenate([v_prev[rows, :], v_cur[rows, :]], axis=1)
            q_pad = []
            for g in range(ATT_GROUP):
                hd = kv * ATT_GROUP + g
                qh = q_ref[hd * hd_dim:(hd + 1) * hd_dim, cur]
                q_pad.append(jnp.concatenate([qh, zeros] if kv == 0 else [zeros, qh], axis=0))
            s_all = _dot(k_cat, jnp.concatenate(q_pad, axis=1))
            p_all, r_den = [], []
            for g in range(ATT_GROUP):
                sink = sinks_ref[layer * ATT_HEADS + kv * ATT_GROUP + g]
                s_prev = s_all[:blk, g * blk:(g + 1) * blk]
                if b == 0:
                    s_prev = s_prev + jnp.where(first, NEG_BIG, 0.0)
                s = jnp.where(from_prev, s_prev, s_all[blk:, g * blk:(g + 1) * blk])
                m = jnp.maximum(jnp.max(s, axis=0, keepdims=True), sink)
                p = jnp.exp(s - m)
                r_den.append(1.0 / (jnp.sum(p, axis=0, keepdims=True) + jnp.exp(sink - m)))
                p_all.append(jnp.concatenate([jnp.where(from_prev, p, 0.0), jnp.where(from_prev, 0.0, p)],
                                             axis=0).astype(BF16))
            o_all = _dot(v_cat, jnp.concatenate(p_all, axis=1))
            for g in range(ATT_GROUP):
                hd = kv * ATT_GROUP + g
                o_ref[hd * hd_dim:(hd + 1) * hd_dim, cur] = (o_all[:, g * blk:(g + 1) * blk] * r_den[g]).astype(BF16)


def _attention(sinks, qa_t, ka, va_t, *, layer, seq, tq):
    T = ka.shape[0]
    per = tq // ATT_BLOCK
    prev_blk = lambda i: jnp.maximum(i * per - 1, 0)
    grid_spec = pltpu.PrefetchScalarGridSpec(
        num_scalar_prefetch=1, grid=(T // tq,),
        in_specs=[pl.BlockSpec((ATT_Q_W, tq), lambda i, *_: (0, i)),
                  pl.BlockSpec((tq, ATT_KV_W), lambda i, *_: (i, 0)),
                  pl.BlockSpec((ATT_BLOCK, ATT_KV_W), lambda i, *_: (prev_blk(i), 0)),
                  pl.BlockSpec((ATT_KV_W, tq), lambda i, *_: (0, i)),
                  pl.BlockSpec((ATT_KV_W, ATT_BLOCK), lambda i, *_: (0, prev_blk(i)))],
        out_specs=pl.BlockSpec((ATT_Q_W, tq), lambda i, *_: (0, i)))
    return pl.pallas_call(
        functools.partial(_attention_kernel, layer=layer, steps_per_seq=seq // tq), grid_spec=grid_spec,
        out_shape=jax.ShapeDtypeStruct((ATT_Q_W, T), BF16),
        compiler_params=_params("parallel"), name="attention",
    )(sinks, qa_t, ka, ka, va_t, va_t)


def _split3(x):
    h1 = x.astype(BF16)
    r1 = x - h1.astype(F32)
    h2 = r1.astype(BF16)
    h3 = (r1 - h2.astype(F32)).astype(BF16)
    return h1, h2, h3


def _mlstm_kernel(q_ref, k_ref, v_ref, osig_ref, gcol_ref, grow_ref, gain_ref, o_ref, state_ref, m_ref):
    L = M_CHUNK
    NROW = state_ref.shape[1]

    @pl.when(pl.program_id(1) == 0)
    def _():
        state_ref[...] = jnp.zeros(state_ref.shape, F32)
        m_ref[...] = jnp.zeros(m_ref.shape, F32)

    src = lax.broadcasted_iota(jnp.int32, (L, L), 0)
    dst = lax.broadcasted_iota(jnp.int32, (L, L), 1)
    causal = src <= dst
    tril = jnp.where(dst <= src, 1.0, 0.0).astype(BF16)
    triu = jnp.where(causal, 1.0, 0.0).astype(BF16)
    zeros = jnp.zeros((M_QK_DIM, L), BF16)

    for c in range(q_ref.shape[1] // L):
        cur = slice(c * L, (c + 1) * L)
        gcol = gcol_ref[cur, :]
        grow = grow_ref[:, cur]
        bc = sum(_dot(tril, t) for t in _split3(gcol))
        br = sum(_dot(t, triu) for t in _split3(grow))
        r_col = gcol - pltpu.roll(bc, LANES - M_HEADS, 1)
        for h in range(M_HEADS):
            k2 = k_ref[cur, (h // 2) * LANES:(h // 2 + 1) * LANES]
            qh = q_ref[h * M_QK_DIM:(h + 1) * M_QK_DIM, cur]
            q_pad = jnp.concatenate([qh, zeros] if h % 2 == 0 else [zeros, qh], axis=0)
            vh = v_ref[h * M_V_DIM:(h + 1) * M_V_DIM, cur]
            b_row = br[M_HEADS + h:M_HEADS + h + 1, :]
            li_row = grow[h:h + 1, :]
            m_prev = m_ref[h:h + 1, :]

            s_t = _dot(k2, q_pad)
            log_d = jnp.where(causal, jnp.broadcast_to(r_col[:, h:h + 1], (L, L)) + b_row, NEG_BIG)
            m_t = jnp.maximum(b_row + m_prev, jnp.max(log_d, axis=0, keepdims=True))
            w_t = jnp.exp(log_d - m_t) * s_t
            a_int = jnp.exp(b_row + m_prev - m_t)
            state = state_ref[h]
            inter = _dot(state.astype(BF16), q_pad)
            num = _dot(vh, w_t.astype(BF16)) + a_int * inter[:M_V_DIM]
            den = jnp.sum(w_t, axis=0, keepdims=True) + a_int * inter[M_V_DIM:M_V_DIM + 1]
            hv = num / jnp.maximum(jnp.abs(den), jnp.exp(-m_t))
            inv = lax.rsqrt(jnp.mean(hv * hv, axis=0, keepdims=True) + EPS)
            rows = slice(h * M_V_DIM, (h + 1) * M_V_DIM)
            o_ref[rows, cur] = ((hv * inv) * gain_ref[rows, :] * osig_ref[rows, cur].astype(F32)).astype(BF16)

            g = b_row[:, L - 1:L]
            w_end = g - b_row + li_row
            m_loc = jnp.max(w_end, axis=1, keepdims=True)
            e_row = jnp.exp(w_end - m_loc)
            m_new = jnp.maximum(g + m_prev, m_loc)
            lhs = jnp.concatenate([(vh.astype(F32) * e_row).astype(BF16),
                                   jnp.broadcast_to(e_row, (NROW - M_V_DIM, L)).astype(BF16)], axis=0)
            state_ref[h] = jnp.exp(g + m_prev - m_new) * state + jnp.exp(m_loc - m_new) * _dot(lhs, k2)
            m_ref[h:h + 1, :] = m_new


def _mlstm(qm_t, km, vm_t, osig_t, gcol, grow, gain_b, *, layer, seq, tc):
    T = km.shape[0]
    n = seq // tc
    row = lambda w: pl.BlockSpec((tc, w), lambda b, c: (b * n + c, 0))
    colm = lambda w: pl.BlockSpec((w, tc), lambda b, c: (0, b * n + c))
    return pl.pallas_call(
        _mlstm_kernel, grid=(T // seq, n),
        in_specs=[colm(M_QK_W), row(M_QK_W), colm(M_V_W), colm(M_V_W), row(LANES), colm(2 * M_HEADS),
                  _layer_spec(gain_b, layer)],
        out_specs=colm(M_V_W),
        out_shape=jax.ShapeDtypeStruct((M_V_W, T), BF16),
        scratch_shapes=[pltpu.VMEM((M_HEADS, M_V_DIM + SUBLANES_BF16, LANES), F32),
                        pltpu.VMEM((2 * M_HEADS, LANES), F32)],
        compiler_params=_params("parallel", "arbitrary"), name="mlstm",
    )(qm_t, km, vm_t, osig_t, gcol, grow, gain_b)


def _merge_kernel(x_ref, att_ref, hm_ref, sg_ref, wa_ref, wm_ref, wo_ref, o_ref):
    D = x_ref.shape[1]
    a = _dot_tn(att_ref[...], wa_ref[...])
    m = _dot_tn(hm_ref[...], wm_ref[...])
    mixed = sg_ref[:, :D].astype(F32) * a + sg_ref[:, D:].astype(F32) * m
    o_ref[...] = x_ref[...] + _dot(mixed.astype(BF16), wo_ref[...])


def _merge(x, att_t, hm_t, sg, wa, wm, wo, *, layer, tm):
    T, D = x.shape
    row = lambda w: pl.BlockSpec((tm, w), lambda i: (i, 0))
    colm = lambda w: pl.BlockSpec((w, tm), lambda i: (0, i))
    return pl.pallas_call(
        _merge_kernel, grid=(T // tm,),
        in_specs=[row(D), colm(ATT_Q_W), colm(M_V_W), row(2 * D),
                  _layer_spec(wa, layer), _layer_spec(wm, layer), _layer_spec(wo, layer)],
        out_specs=row(D), out_shape=jax.ShapeDtypeStruct((T, D), F32),
        compiler_params=_params("parallel"), name="merge",
    )(x, att_t, hm_t, sg, wa, wm, wo)


def _ffn_kernel(x_ref, gain_ref, w1_ref, w2_ref, o_ref, *, n_chunks):
    x = x_ref[...]
    inv = lax.rsqrt(jnp.mean(x * x, axis=-1, keepdims=True) + EPS)
    h = ((x * inv) * gain_ref[...]).astype(BF16)
    fc = w1_ref.shape[1] // n_chunks
    acc = x
    for c in range(n_chunks):
        u = jnp.maximum(_dot(h, w1_ref[:, c * fc:(c + 1) * fc]), 0.0)
        acc = acc + _dot((u * u).astype(BF16), w2_ref[c * fc:(c + 1) * fc, :])
    o_ref[...] = acc


def _ffn(x, gain, w1, w2, *, layer, tm):
    T, D = x.shape
    row = pl.BlockSpec((tm, D), lambda i: (i, 0))
    return pl.pallas_call(
        functools.partial(_ffn_kernel, n_chunks=4), grid=(T // tm,),
        in_specs=[row, _layer_spec(gain, layer), _layer_spec(w1, layer), _layer_spec(w2, layer)],
        out_specs=row, out_shape=jax.ShapeDtypeStruct((T, D), F32),
        compiler_params=_params("parallel"), name="ffn",
    )(x, gain, w1, w2)


def _rope_tables(seq):
    pos = jnp.arange(seq, dtype=F32)
    inv_freq = ROPE_THETA ** (-jnp.arange(0, ROPE_DIM, 2, dtype=F32) / ROPE_DIM)
    ang = pos[:, None] * inv_freq[None, :]
    cos, sin = jnp.cos(ang), jnp.sin(ang)
    pad = jnp.zeros((seq, ATT_HEAD_DIM - ROPE_DIM), F32)
    zeros = jnp.zeros((seq, ROPE_HALF), F32)
    rc = jnp.concatenate([cos, cos, pad + 1.0], axis=1)
    rs1 = jnp.concatenate([-sin, zeros, pad], axis=1)
    rs2 = jnp.concatenate([zeros, sin, pad], axis=1)
    tile = lambda t: jnp.concatenate([t, t], axis=1)
    return cos.T, sin.T, tile(rc), tile(rs1), tile(rs2)


def _prep_w_in(w_in):
    wb = w_in.astype(BF16)
    seg = lambda i: wb[:, :, IN_OFFS[i]:IN_OFFS[i + 1]]
    q_a, k_a, v_a, q_m, k_m, v_m, o_m, if_m, gates = (seg(i) for i in range(len(IN_WIDTHS)))
    w_fm = jnp.swapaxes(jnp.concatenate([o_m, q_a, v_a, q_m, v_m], axis=2), 1, 2)
    if_pad = jnp.pad(if_m, ((0, 0), (0, 0), (0, LANES - 2 * M_HEADS)))
    w_tm = jnp.concatenate([gates, k_a, k_m, if_pad], axis=2)
    return w_fm, w_tm


def kernel(x, norm_mix, w_in, att_q_norm, att_k_norm, att_sinks, m_gate_bias, m_head_norm,
           w_att_branch, w_m_branch, w_out, norm_ffn, w_ff1, w_ff2):
    B, S, D = x.shape
    T = B * S
    depth = w_in.shape[0]
    tm = min(TOKEN_TILE, S)
    cos_t, sin_t, rc, rs1, rs2 = _rope_tables(S)
    w_fm, w_tm = _prep_w_in(w_in)
    wa, wm, wo = w_att_branch.astype(BF16), w_m_branch.astype(BF16), w_out.astype(BF16)
    w1, w2 = w_ff1.astype(BF16), w_ff2.astype(BF16)
    bias = jnp.pad(m_gate_bias, ((0, 0), (0, LANES - 2 * M_HEADS))).reshape(depth, 1, LANES)
    qgain = jnp.broadcast_to(att_q_norm[:, :, None], (depth, ATT_HEAD_DIM, tm))
    kgain = jnp.tile(att_k_norm, (1, 2)).reshape(depth, 1, LANES)
    mgain = jnp.broadcast_to(m_head_norm[:, :, None], (depth, M_V_W, LANES))
    gain_mix = norm_mix.reshape(depth, 1, D)
    gain_ffn = norm_ffn.reshape(depth, 1, D)
    sinks = att_sinks.reshape(depth * ATT_HEADS)
    xt = x.reshape(T, D)
    for l in range(depth):
        qa_t, va_t, qm_t, vm_t, osig_t, ka, km, sg, gcol, grow = _in_proj(
            xt, gain_mix, w_fm, w_tm, bias, qgain, kgain, cos_t, sin_t, rc, rs1, rs2, layer=l, seq=S, tm=tm)
        att_t = _attention(sinks, qa_t, ka, va_t, layer=l, seq=S, tq=tm)
        hm_t = _mlstm(qm_t, km, vm_t, osig_t, gcol, grow, mgain, layer=l, seq=S, tc=tm)
        xt = _merge(xt, att_t, hm_t, sg, wa, wm, wo, layer=l, tm=tm)
        xt = _ffn(xt, gain_ffn, w1, w2, layer=l, tm=tm)
    return xt.reshape(B, S, D)
```

```python
import functools

import jax
import jax.numpy as jnp
import numpy as np
from jax import lax
from jax.experimental import pallas as pl
from jax.experimental.pallas import tpu as pltpu

D_MODEL = 1024
ATT_HEADS = 8
ATT_KV_HEADS = 2
ATT_GROUP = ATT_HEADS // ATT_KV_HEADS
ATT_HEAD_DIM = 64
ATT_BLOCK = 128
ROPE_DIM = ATT_HEAD_DIM // 4
ROPE_HALF = ROPE_DIM // 2
ROPE_THETA = 500000.0
M_HEADS = 4
M_QK_DIM = 64
M_V_DIM = 128
M_CHUNK = 128
D_FF = 4 * D_MODEL
EPS = 1e-6

ATT_Q_W = ATT_HEADS * ATT_HEAD_DIM
ATT_KV_W = ATT_KV_HEADS * ATT_HEAD_DIM
M_QK_W = M_HEADS * M_QK_DIM
M_V_W = M_HEADS * M_V_DIM
IN_WIDTHS = (ATT_Q_W, ATT_KV_W, ATT_KV_W, M_QK_W, M_QK_W, M_V_W, M_V_W, 2 * M_HEADS, 2 * D_MODEL)
IN_OFFS = tuple(int(o) for o in np.concatenate([[0], np.cumsum(IN_WIDTHS)]))
LANES = 128
SUBLANES_BF16 = 16
FM_WIDTHS = (M_V_W, ATT_Q_W, ATT_KV_W, M_QK_W, M_V_W)
FM_OFFS = tuple(int(o) for o in np.concatenate([[0], np.cumsum(FM_WIDTHS)]))
TM_WIDTHS = (2 * D_MODEL, ATT_KV_W, M_QK_W, LANES)
TM_OFFS = tuple(int(o) for o in np.concatenate([[0], np.cumsum(TM_WIDTHS)]))

NEG_BIG = -1e30
VMEM_LIMIT = 56 * 1024 * 1024
TOKEN_TILE = 512

BF16 = jnp.bfloat16
F32 = jnp.float32


def _dot(a, b):
    return jnp.dot(a, b, preferred_element_type=F32)


def _dot_nt(a, b):
    return lax.dot_general(a, b, (((1,), (1,)), ((), ())), preferred_element_type=F32)


def _dot_tn(a, b):
    return lax.dot_general(a, b, (((0,), (0,)), ((), ())), preferred_element_type=F32)


def _sigmoid(x):
    return 1.0 / (1.0 + jnp.exp(-x))


def _log_sigmoid(x):
    return jnp.minimum(x, 0.0) - jnp.log(1.0 + jnp.exp(-jnp.abs(x)))


def _const_spec(shape):
    return pl.BlockSpec(shape, lambda *_: (0,) * len(shape))


def _layer_spec(arr, layer):
    return pl.BlockSpec((None,) + arr.shape[1:], lambda *_: (layer,) + (0,) * (arr.ndim - 1))


def _params(*sem):
    return pltpu.CompilerParams(dimension_semantics=sem, vmem_limit_bytes=VMEM_LIMIT)


def _in_proj_kernel(x_ref, gain_ref, w_fm_ref, w_tm_ref, bias_ref, qgain_ref, kgain_ref,
                    cos_ref, sin_ref, rc_ref, rs1_ref, rs2_ref,
                    qa_ref, va_ref, qm_ref, vm_ref, om_ref, ka_ref, km_ref, sg_ref, gcol_ref, grow_ref):
    x = x_ref[...]
    inv = lax.rsqrt(jnp.mean(x * x, axis=-1, keepdims=True) + EPS)
    h = ((x * inv) * gain_ref[...]).astype(BF16)
    tm = x.shape[0]

    sg_ref[...] = _sigmoid(_dot(h, w_tm_ref[:, TM_OFFS[0]:TM_OFFS[1]])).astype(BF16)
    om_ref[...] = _sigmoid(_dot_nt(w_fm_ref[FM_OFFS[0]:FM_OFFS[1], :], h)).astype(BF16)

    zq = _dot_nt(w_fm_ref[FM_OFFS[1]:FM_OFFS[2], :], h)
    cos, sin = cos_ref[...], sin_ref[...]
    qgain = qgain_ref[...] * (ATT_HEAD_DIM ** -0.5)
    for hd in range(ATT_HEADS):
        z = zq[hd * ATT_HEAD_DIM:(hd + 1) * ATT_HEAD_DIM, :]
        inv_h = lax.rsqrt(jnp.mean(z * z, axis=0, keepdims=True) + EPS)
        y = (z * inv_h) * qgain
        x1, x2 = y[:ROPE_HALF], y[ROPE_HALF:ROPE_DIM]
        y = jnp.concatenate([x1 * cos - x2 * sin, x2 * cos + x1 * sin, y[ROPE_DIM:]], axis=0)
        qa_ref[hd * ATT_HEAD_DIM:(hd + 1) * ATT_HEAD_DIM, :] = y.astype(BF16)

    zn = _dot(h, w_tm_ref[:, TM_OFFS[1]:TM_OFFS[4]])
    zr = _dot_nt(w_fm_ref[FM_OFFS[2]:FM_OFFS[5], :], h)
    va_ref[...] = zr[:ATT_KV_W, :].astype(BF16)
    qm_ref[...] = (zr[ATT_KV_W:ATT_KV_W + M_QK_W, :] * (M_QK_DIM ** -0.5)).astype(BF16)
    vm_ref[...] = zr[ATT_KV_W + M_QK_W:, :].astype(BF16)

    lane = lax.broadcasted_iota(jnp.int32, (tm, LANES), 1)
    low = lane < ATT_HEAD_DIM
    zk = zn[:, :ATT_KV_W]
    s = zk * zk
    r0 = jnp.sum(jnp.where(low, s, 0.0), axis=-1, keepdims=True)
    r1 = jnp.sum(jnp.where(low, 0.0, s), axis=-1, keepdims=True)
    inv_k = lax.rsqrt(jnp.where(low, r0, r1) * (1.0 / ATT_HEAD_DIM) + EPS)
    y = (zk * inv_k) * kgain_ref[...]
    up = pltpu.roll(y, LANES - ROPE_HALF, 1)
    dn = pltpu.roll(y, ROPE_HALF, 1)
    ka_ref[...] = (y * rc_ref[...] + up * rs1_ref[...] + dn * rs2_ref[...]).astype(BF16)
    km_ref[...] = zn[:, ATT_KV_W:ATT_KV_W + M_QK_W].astype(BF16)
    zi = zn[:, ATT_KV_W + M_QK_W:] + bias_ref[...]
    g = jnp.where(lane < M_HEADS, zi, jnp.where(lane < 2 * M_HEADS, _log_sigmoid(zi), 0.0))
    gcol_ref[...] = g
    grow_ref[...] = g.T[:2 * M_HEADS, :]


def _in_proj(x, gain, w_fm, w_tm, bias, qgain, kgain, cos_t, sin_t, rc, rs1, rs2, *, layer, seq, tm):
    T, D = x.shape
    n_pos = seq // tm
    row = lambda w: pl.BlockSpec((tm, w), lambda i: (i, 0))
    colm = lambda w: pl.BlockSpec((w, tm), lambda i: (0, i))
    pos = pl.BlockSpec((tm, LANES), lambda i: (i % n_pos, 0))
    pos_t = pl.BlockSpec((ROPE_HALF, tm), lambda i: (0, i % n_pos))
    fm = lambda w: jax.ShapeDtypeStruct((w, T), BF16)
    out_shapes = (fm(ATT_Q_W), fm(ATT_KV_W), fm(M_QK_W), fm(M_V_W), fm(M_V_W),
                  jax.ShapeDtypeStruct((T, ATT_KV_W), BF16),
                  jax.ShapeDtypeStruct((T, M_QK_W), BF16),
                  jax.ShapeDtypeStruct((T, 2 * D), BF16),
                  jax.ShapeDtypeStruct((T, LANES), F32),
                  jax.ShapeDtypeStruct((2 * M_HEADS, T), F32))
    out_specs = (colm(ATT_Q_W), colm(ATT_KV_W), colm(M_QK_W), colm(M_V_W), colm(M_V_W),
                 row(ATT_KV_W), row(M_QK_W), row(2 * D), row(LANES), colm(2 * M_HEADS))
    in_specs = [row(D)] + [_layer_spec(a, layer) for a in (gain, w_fm, w_tm, bias, qgain, kgain)] + [
        pos_t, pos_t, pos, pos, pos]
    return pl.pallas_call(
        _in_proj_kernel, grid=(T // tm,), in_specs=in_specs, out_specs=out_specs, out_shape=out_shapes,
        compiler_params=_params("parallel"), name="in_proj",
    )(x, gain, w_fm, w_tm, bias, qgain, kgain, cos_t, sin_t, rc, rs1, rs2)


def _attention_blocks(blocks, first, sinks_ref, q_ref, k_ref, kp_ref, v_ref, vp_ref, o_ref, *, layer):
    blk, hd_dim = ATT_BLOCK, ATT_HEAD_DIM
    key = lax.broadcasted_iota(jnp.int32, (blk, blk), 0)
    qry = lax.broadcasted_iota(jnp.int32, (blk, blk), 1)
    from_prev = key > qry
    zeros = jnp.zeros((hd_dim, blk), BF16)
    for b in blocks:
        cur = slice(b * blk, (b + 1) * blk)
        old = slice((b - 1) * blk, b * blk)
        k_cat = jnp.concatenate([kp_ref[...] if b == 0 else k_ref[old, :], k_ref[cur, :]], axis=0)
        v_prev = vp_ref[...] if b == 0 else v_ref[:, old]
        v_cur = v_ref[:, cur]
        for kv in range(ATT_KV_HEADS):
            rows = slice(kv * hd_dim, (kv + 1) * hd_dim)
            v_cat = jnp.concatenate([v_prev[rows, :], v_cur[rows, :]], axis=1)
            q_pad = []
            for g in range(ATT_GROUP):
                hd = kv * ATT_GROUP + g
                qh = q_ref[hd * hd_dim:(hd + 1) * hd_dim, cur]
                q_pad.append(jnp.concatenate([qh, zeros] if kv == 0 else [zeros, qh], axis=0))
            s_all = _dot(k_cat, jnp.concatenate(q_pad, axis=1))
            p_all, r_den = [], []
            for g in range(ATT_GROUP):
                sink = sinks_ref[layer * ATT_HEADS + kv * ATT_GROUP + g]
                s_prev = s_all[:blk, g * blk:(g + 1) * blk]
                if b == 0:
                    s_prev = s_prev + jnp.where(first, NEG_BIG, 0.0)
                s = jnp.where(from_prev, s_prev, s_all[blk:, g * blk:(g + 1) * blk])
                m = jnp.maximum(jnp.max(s, axis=0, keepdims=True), sink)
                p = jnp.exp(s - m)
                r_den.append(1.0 / (jnp.sum(p, axis=0, keepdims=True) + jnp.exp(sink - m)))
                p_all.append(jnp.concatenate([jnp.where(from_prev, p, 0.0), jnp.where(from_prev, 0.0, p)],
                                             axis=0).astype(BF16))
            o_all = _dot(v_cat, jnp.concatenate(p_all, axis=1))
            for g in range(ATT_GROUP):
                hd = kv * ATT_GROUP + g
                o_ref[hd * hd_dim:(hd + 1) * hd_dim, cur] = (o_all[:, g * blk:(g + 1) * blk] * r_den[g]).astype(BF16)


def _split3(x):
    h1 = x.astype(BF16)
    r1 = x - h1.astype(F32)
    h2 = r1.astype(BF16)
    h3 = (r1 - h2.astype(F32)).astype(BF16)
    return h1, h2, h3


def _mlstm_chunks(chunks, q_ref, k_ref, v_ref, osig_ref, gcol_ref, grow_ref, gain_ref, o_ref, state_ref, m_ref):
    L = M_CHUNK
    NROW = state_ref.shape[1]
    src = lax.broadcasted_iota(jnp.int32, (L, L), 0)
    dst = lax.broadcasted_iota(jnp.int32, (L, L), 1)
    causal = src <= dst
    tril = jnp.where(dst <= src, 1.0, 0.0).astype(BF16)
    triu = jnp.where(causal, 1.0, 0.0).astype(BF16)
    zeros = jnp.zeros((M_QK_DIM, L), BF16)

    for c in chunks:
        cur = slice(c * L, (c + 1) * L)
        gcol = gcol_ref[cur, :]
        grow = grow_ref[:, cur]
        bc = sum(_dot(tril, t) for t in _split3(gcol))
        br = sum(_dot(t, triu) for t in _split3(grow))
        r_col = gcol - pltpu.roll(bc, LANES - M_HEADS, 1)
        for h in range(M_HEADS):
            k2 = k_ref[cur, (h // 2) * LANES:(h // 2 + 1) * LANES]
            qh = q_ref[h * M_QK_DIM:(h + 1) * M_QK_DIM, cur]
            q_pad = jnp.concatenate([qh, zeros] if h % 2 == 0 else [zeros, qh], axis=0)
            vh = v_ref[h * M_V_DIM:(h + 1) * M_V_DIM, cur]
            b_row = br[M_HEADS + h:M_HEADS + h + 1, :]
            li_row = grow[h:h + 1, :]
            m_prev = m_ref[h:h + 1, :]

            s_t = _dot(k2, q_pad)
            log_d = jnp.where(causal, jnp.broadcast_to(r_col[:, h:h + 1], (L, L)) + b_row, NEG_BIG)
            m_t = jnp.maximum(b_row + m_prev, jnp.max(log_d, axis=0, keepdims=True))
            w_t = jnp.exp(log_d - m_t) * s_t
            a_int = jnp.exp(b_row + m_prev - m_t)
            state = state_ref[h]
            inter = _dot(state.astype(BF16), q_pad)
            num = _dot(vh, w_t.astype(BF16)) + a_int * inter[:M_V_DIM]
            den = jnp.sum(w_t, axis=0, keepdims=True) + a_int * inter[M_V_DIM:M_V_DIM + 1]
            hv = num / jnp.maximum(jnp.abs(den), jnp.exp(-m_t))
            inv = lax.rsqrt(jnp.mean(hv * hv, axis=0, keepdims=True) + EPS)
            rows = slice(h * M_V_DIM, (h + 1) * M_V_DIM)
            o_ref[rows, cur] = ((hv * inv) * gain_ref[rows, :] * osig_ref[rows, cur].astype(F32)).astype(BF16)

            g = b_row[:, L - 1:L]
            w_end = g - b_row + li_row
            m_loc = jnp.max(w_end, axis=1, keepdims=True)
            e_row = jnp.exp(w_end - m_loc)
            m_new = jnp.maximum(g + m_prev, m_loc)
            lhs = jnp.concatenate([(vh.astype(F32) * e_row).astype(BF16),
                                   jnp.broadcast_to(e_row, (NROW - M_V_DIM, L)).astype(BF16)], axis=0)
            state_ref[h] = jnp.exp(g + m_prev - m_new) * state + jnp.exp(m_loc - m_new) * _dot(lhs, k2)
            m_ref[h:h + 1, :] = m_new


def _mixers_kernel(sinks_ref, qa_ref, ka_ref, kap_ref, va_ref, vap_ref,
                   qm_ref, km_ref, vm_ref, osig_ref, gcol_ref, grow_ref, gain_ref,
                   att_ref, hm_ref, state_ref, m_ref, *, layer):
    first = pl.program_id(1) == 0

    @pl.when(first)
    def _():
        state_ref[...] = jnp.zeros(state_ref.shape, F32)
        m_ref[...] = jnp.zeros(m_ref.shape, F32)

    for c in range(qa_ref.shape[1] // ATT_BLOCK):
        _attention_blocks([c], first, sinks_ref, qa_ref, ka_ref, kap_ref, va_ref, vap_ref, att_ref, layer=layer)
        _mlstm_chunks([c], qm_ref, km_ref, vm_ref, osig_ref, gcol_ref, grow_ref, gain_ref, hm_ref, state_ref, m_ref)


def _mixers(sinks, qa_t, ka, va_t, qm_t, km, vm_t, osig_t, gcol, grow, gain_b, *, layer, seq, tc):
    T = km.shape[0]
    n = seq // tc
    per = tc // ATT_BLOCK
    tile = lambda b, c: b * n + c
    prev_blk = lambda b, c: jnp.maximum(tile(b, c) * per - 1, 0)
    row = lambda w: pl.BlockSpec((tc, w), lambda b, c, *_: (tile(b, c), 0))
    colm = lambda w: pl.BlockSpec((w, tc), lambda b, c, *_: (0, tile(b, c)))
    grid_spec = pltpu.PrefetchScalarGridSpec(
        num_scalar_prefetch=1, grid=(T // seq, n),
        in_specs=[colm(ATT_Q_W), row(ATT_KV_W),
                  pl.BlockSpec((ATT_BLOCK, ATT_KV_W), lambda b, c, *_: (prev_blk(b, c), 0)),
                  colm(ATT_KV_W),
                  pl.BlockSpec((ATT_KV_W, ATT_BLOCK), lambda b, c, *_: (0, prev_blk(b, c))),
                  colm(M_QK_W), row(M_QK_W), colm(M_V_W), colm(M_V_W), row(LANES), colm(2 * M_HEADS),
                  _layer_spec(gain_b, layer)],
        out_specs=(colm(ATT_Q_W), colm(M_V_W)),
        scratch_shapes=[pltpu.VMEM((M_HEADS, M_V_DIM + SUBLANES_BF16, LANES), F32),
                        pltpu.VMEM((2 * M_HEADS, LANES), F32)])
    return pl.pallas_call(
        functools.partial(_mixers_kernel, layer=layer), grid_spec=grid_spec,
        out_shape=(jax.ShapeDtypeStruct((ATT_Q_W, T), BF16), jax.ShapeDtypeStruct((M_V_W, T), BF16)),
        compiler_params=_params("parallel", "arbitrary"), name="mixers",
    )(sinks, qa_t, ka, ka, va_t, va_t, qm_t, km, vm_t, osig_t, gcol, grow, gain_b)


def _merge_kernel(x_ref, att_ref, hm_ref, sg_ref, wa_ref, wm_ref, wo_ref, o_ref):
    D = x_ref.shape[1]
    a = _dot_tn(att_ref[...], wa_ref[...])
    m = _dot_tn(hm_ref[...], wm_ref[...])
    mixed = sg_ref[:, :D].astype(F32) * a + sg_ref[:, D:].astype(F32) * m
    o_ref[...] = x_ref[...] + _dot(mixed.astype(BF16), wo_ref[...])


def _merge(x, att_t, hm_t, sg, wa, wm, wo, *, layer, tm):
    T, D = x.shape
    row = lambda w: pl.BlockSpec((tm, w), lambda i: (i, 0))
    colm = lambda w: pl.BlockSpec((w, tm), lambda i: (0, i))
    return pl.pallas_call(
        _merge_kernel, grid=(T // tm,),
        in_specs=[row(D), colm(ATT_Q_W), colm(M_V_W), row(2 * D),
                  _layer_spec(wa, layer), _layer_spec(wm, layer), _layer_spec(wo, layer)],
        out_specs=row(D), out_shape=jax.ShapeDtypeStruct((T, D), F32),
        compiler_params=_params("parallel"), name="merge",
    )(x, att_t, hm_t, sg, wa, wm, wo)


def _ffn_kernel(x_ref, gain_ref, w1_ref, w2_ref, o_ref, *, n_chunks):
    x = x_ref[...]
    inv = lax.rsqrt(jnp.mean(x * x, axis=-1, keepdims=True) + EPS)
    h = ((x * inv) * gain_ref[...]).astype(BF16)
    fc = w1_ref.shape[1] // n_chunks
    acc = x
    for c in range(n_chunks):
        u = jnp.maximum(_dot(h, w1_ref[:, c * fc:(c + 1) * fc]), 0.0)
        acc = acc + _dot((u * u).astype(BF16), w2_ref[c * fc:(c + 1) * fc, :])
    o_ref[...] = acc


def _ffn(x, gain, w1, w2, *, layer, tm):
    T, D = x.shape
    row = pl.BlockSpec((tm, D), lambda i: (i, 0))
    return pl.pallas_call(
        functools.partial(_ffn_kernel, n_chunks=4), grid=(T // tm,),
        in_specs=[row, _layer_spec(gain, layer), _layer_spec(w1, layer), _layer_spec(w2, layer)],
        out_specs=row, out_shape=jax.ShapeDtypeStruct((T, D), F32),
        compiler_params=_params("parallel"), name="ffn",
    )(x, gain, w1, w2)


def _rope_tables(seq):
    pos = jnp.arange(seq, dtype=F32)
    inv_freq = ROPE_THETA ** (-jnp.arange(0, ROPE_DIM, 2, dtype=F32) / ROPE_DIM)
    ang = pos[:, None] * inv_freq[None, :]
    cos, sin = jnp.cos(ang), jnp.sin(ang)
    pad = jnp.zeros((seq, ATT_HEAD_DIM - ROPE_DIM), F32)
    zeros = jnp.zeros((seq, ROPE_HALF), F32)
    rc = jnp.concatenate([cos, cos, pad + 1.0], axis=1)
    rs1 = jnp.concatenate([-sin, zeros, pad], axis=1)
    rs2 = jnp.concatenate([zeros, sin, pad], axis=1)
    tile = lambda t: jnp.concatenate([t, t], axis=1)
    return cos.T, sin.T, tile(rc), tile(rs1), tile(rs2)


def _prep_w_in(w_in):
    wb = w_in.astype(BF16)
    seg = lambda i: wb[:, :, IN_OFFS[i]:IN_OFFS[i + 1]]
    q_a, k_a, v_a, q_m, k_m, v_m, o_m, if_m, gates = (seg(i) for i in range(len(IN_WIDTHS)))
    w_fm = jnp.swapaxes(jnp.concatenate([o_m, q_a, v_a, q_m, v_m], axis=2), 1, 2)
    if_pad = jnp.pad(if_m, ((0, 0), (0, 0), (0, LANES - 2 * M_HEADS)))
    w_tm = jnp.concatenate([gates, k_a, k_m, if_pad], axis=2)
    return w_fm, w_tm


def kernel(x, norm_mix, w_in, att_q_norm, att_k_norm, att_sinks, m_gate_bias, m_head_norm,
           w_att_branch, w_m_branch, w_out, norm_ffn, w_ff1, w_ff2):
    B, S, D = x.shape
    T = B * S
    depth = w_in.shape[0]
    tm = min(TOKEN_TILE, S)
    cos_t, sin_t, rc, rs1, rs2 = _rope_tables(S)
    w_fm, w_tm = _prep_w_in(w_in)
    wa, wm, wo = w_att_branch.astype(BF16), w_m_branch.astype(BF16), w_out.astype(BF16)
    w1, w2 = w_ff1.astype(BF16), w_ff2.astype(BF16)
    bias = jnp.pad(m_gate_bias, ((0, 0), (0, LANES - 2 * M_HEADS))).reshape(depth, 1, LANES)
    qgain = jnp.broadcast_to(att_q_norm[:, :, None], (depth, ATT_HEAD_DIM, tm))
    kgain = jnp.tile(att_k_norm, (1, 2)).reshape(depth, 1, LANES)
    mgain = jnp.broadcast_to(m_head_norm[:, :, None], (depth, M_V_W, LANES))
    gain_mix = norm_mix.reshape(depth, 1, D)
    gain_ffn = norm_ffn.reshape(depth, 1, D)
    sinks = att_sinks.reshape(depth * ATT_HEADS)
    xt = x.reshape(T, D)
    for l in range(depth):
        qa_t, va_t, qm_t, vm_t, osig_t, ka, km, sg, gcol, grow = _in_proj(
            xt, gain_mix, w_fm, w_tm, bias, qgain, kgain, cos_t, sin_t, rc, rs1, rs2, layer=l, seq=S, tm=tm)
        att_t, hm_t = _mixers(sinks, qa_t, ka, va_t, qm_t, km, vm_t, osig_t, gcol, grow, mgain, layer=l, seq=S, tc=tm)
        xt = _merge(xt, att_t, hm_t, sg, wa, wm, wo, layer=l, tm=tm)
        xt = _ffn(xt, gain_ffn, w1, w2, layer=l, tm=tm)
    return xt.reshape(B, S, D)
```

```python
import functools

import jax
import jax.numpy as jnp
import numpy as np
from jax import lax
from jax.experimental import pallas as pl
from jax.experimental.pallas import tpu as pltpu

D_MODEL = 1024
ATT_HEADS = 8
ATT_KV_HEADS = 2
ATT_GROUP = ATT_HEADS // ATT_KV_HEADS
ATT_HEAD_DIM = 64
ATT_BLOCK = 128
ROPE_DIM = ATT_HEAD_DIM // 4
ROPE_HALF = ROPE_DIM // 2
ROPE_THETA = 500000.0
M_HEADS = 4
M_QK_DIM = 64
M_V_DIM = 128
M_CHUNK = 128
D_FF = 4 * D_MODEL
EPS = 1e-6

ATT_Q_W = ATT_HEADS * ATT_HEAD_DIM
ATT_KV_W = ATT_KV_HEADS * ATT_HEAD_DIM
M_QK_W = M_HEADS * M_QK_DIM
M_V_W = M_HEADS * M_V_DIM
IN_WIDTHS = (ATT_Q_W, ATT_KV_W, ATT_KV_W, M_QK_W, M_QK_W, M_V_W, M_V_W, 2 * M_HEADS, 2 * D_MODEL)
IN_OFFS = tuple(int(o) for o in np.concatenate([[0], np.cumsum(IN_WIDTHS)]))
LANES = 128
SUBLANES_BF16 = 16
FM_WIDTHS = (M_V_W, ATT_Q_W, ATT_KV_W, M_QK_W, M_V_W)
FM_OFFS = tuple(int(o) for o in np.concatenate([[0], np.cumsum(FM_WIDTHS)]))
TM_WIDTHS = (2 * D_MODEL, ATT_KV_W, M_QK_W, LANES)
TM_OFFS = tuple(int(o) for o in np.concatenate([[0], np.cumsum(TM_WIDTHS)]))

NEG_BIG = -1e30
LOG2_E = 1.4426950408889634
VMEM_LIMIT = 56 * 1024 * 1024
TOKEN_TILE = 512

BF16 = jnp.bfloat16
F32 = jnp.float32


def _dot(a, b):
    return jnp.dot(a, b, preferred_element_type=F32)


def _dot_nt(a, b):
    return lax.dot_general(a, b, (((1,), (1,)), ((), ())), preferred_element_type=F32)


def _dot_tn(a, b):
    return lax.dot_general(a, b, (((0,), (0,)), ((), ())), preferred_element_type=F32)


def _sigmoid(x):
    return 1.0 / (1.0 + jnp.exp(-x))


def _log_sigmoid(x):
    return jnp.minimum(x, 0.0) - jnp.log(1.0 + jnp.exp(-jnp.abs(x)))


def _const_spec(shape):
    return pl.BlockSpec(shape, lambda *_: (0,) * len(shape))


def _layer_spec(arr, layer):
    return pl.BlockSpec((None,) + arr.shape[1:], lambda *_: (layer,) + (0,) * (arr.ndim - 1))


def _params(*sem):
    return pltpu.CompilerParams(dimension_semantics=sem, vmem_limit_bytes=VMEM_LIMIT)


def _in_proj_kernel(x_ref, gain_ref, w_fm_ref, w_tm_ref, bias_ref, qgain_ref, kgain_ref,
                    cos_ref, sin_ref, rc_ref, rs1_ref, rs2_ref,
                    qa_ref, va_ref, qm_ref, vm_ref, om_ref, ka_ref, km_ref, sg_ref, gcol_ref, grow_ref):
    x = x_ref[...]
    inv = lax.rsqrt(jnp.mean(x * x, axis=-1, keepdims=True) + EPS)
    h = ((x * inv) * gain_ref[...]).astype(BF16)
    tm = x.shape[0]

    sg_ref[...] = _sigmoid(_dot(h, w_tm_ref[:, TM_OFFS[0]:TM_OFFS[1]])).astype(BF16)
    om_ref[...] = _sigmoid(_dot_nt(w_fm_ref[FM_OFFS[0]:FM_OFFS[1], :], h)).astype(BF16)

    zq = _dot_nt(w_fm_ref[FM_OFFS[1]:FM_OFFS[2], :], h)
    cos, sin = cos_ref[...], sin_ref[...]
    qgain = qgain_ref[...] * (ATT_HEAD_DIM ** -0.5 * LOG2_E)
    for hd in range(ATT_HEADS):
        z = zq[hd * ATT_HEAD_DIM:(hd + 1) * ATT_HEAD_DIM, :]
        inv_h = lax.rsqrt(jnp.mean(z * z, axis=0, keepdims=True) + EPS)
        y = (z * inv_h) * qgain
        x1, x2 = y[:ROPE_HALF], y[ROPE_HALF:ROPE_DIM]
        y = jnp.concatenate([x1 * cos - x2 * sin, x2 * cos + x1 * sin, y[ROPE_DIM:]], axis=0)
        qa_ref[hd * ATT_HEAD_DIM:(hd + 1) * ATT_HEAD_DIM, :] = y.astype(BF16)

    zn = _dot(h, w_tm_ref[:, TM_OFFS[1]:TM_OFFS[4]])
    zr = _dot_nt(w_fm_ref[FM_OFFS[2]:FM_OFFS[5], :], h)
    va_ref[...] = zr[:ATT_KV_W, :].astype(BF16)
    qm_ref[...] = (zr[ATT_KV_W:ATT_KV_W + M_QK_W, :] * (M_QK_DIM ** -0.5)).astype(BF16)
    vm_ref[...] = zr[ATT_KV_W + M_QK_W:, :].astype(BF16)

    lane = lax.broadcasted_iota(jnp.int32, (tm, LANES), 1)
    low = lane < ATT_HEAD_DIM
    zk = zn[:, :ATT_KV_W]
    s = zk * zk
    r0 = jnp.sum(jnp.where(low, s, 0.0), axis=-1, keepdims=True)
    r1 = jnp.sum(jnp.where(low, 0.0, s), axis=-1, keepdims=True)
    inv_k = lax.rsqrt(jnp.where(low, r0, r1) * (1.0 / ATT_HEAD_DIM) + EPS)
    y = (zk * inv_k) * kgain_ref[...]
    up = pltpu.roll(y, LANES - ROPE_HALF, 1)
    dn = pltpu.roll(y, ROPE_HALF, 1)
    ka_ref[...] = (y * rc_ref[...] + up * rs1_ref[...] + dn * rs2_ref[...]).astype(BF16)
    km_ref[...] = zn[:, ATT_KV_W:ATT_KV_W + M_QK_W].astype(BF16)
    zi = zn[:, ATT_KV_W + M_QK_W:] + bias_ref[...]
    g = jnp.where(lane < M_HEADS, zi, jnp.where(lane < 2 * M_HEADS, _log_sigmoid(zi), 0.0))
    gcol_ref[...] = g
    grow_ref[...] = g.T[:2 * M_HEADS, :]


def _in_proj(x, gain, w_fm, w_tm, bias, qgain, kgain, cos_t, sin_t, rc, rs1, rs2, *, layer, seq, tm):
    T, D = x.shape
    n_pos = seq // tm
    row = lambda w: pl.BlockSpec((tm, w), lambda i: (i, 0))
    colm = lambda w: pl.BlockSpec((w, tm), lambda i: (0, i))
    pos = pl.BlockSpec((tm, LANES), lambda i: (i % n_pos, 0))
    pos_t = pl.BlockSpec((ROPE_HALF, tm), lambda i: (0, i % n_pos))
    fm = lambda w: jax.ShapeDtypeStruct((w, T), BF16)
    out_shapes = (fm(ATT_Q_W), fm(ATT_KV_W), fm(M_QK_W), fm(M_V_W), fm(M_V_W),
                  jax.ShapeDtypeStruct((T, ATT_KV_W), BF16),
                  jax.ShapeDtypeStruct((T, M_QK_W), BF16),
                  jax.ShapeDtypeStruct((T, 2 * D), BF16),
                  jax.ShapeDtypeStruct((T, LANES), F32),
                  jax.ShapeDtypeStruct((2 * M_HEADS, T), F32))
    out_specs = (colm(ATT_Q_W), colm(ATT_KV_W), colm(M_QK_W), colm(M_V_W), colm(M_V_W),
                 row(ATT_KV_W), row(M_QK_W), row(2 * D), row(LANES), colm(2 * M_HEADS))
    in_specs = [row(D)] + [_layer_spec(a, layer) for a in (gain, w_fm, w_tm, bias, qgain, kgain)] + [
        pos_t, pos_t, pos, pos, pos]
    return pl.pallas_call(
        _in_proj_kernel, grid=(T // tm,), in_specs=in_specs, out_specs=out_specs, out_shape=out_shapes,
        compiler_params=_params("parallel"), name="in_proj",
    )(x, gain, w_fm, w_tm, bias, qgain, kgain, cos_t, sin_t, rc, rs1, rs2)


def _attention_units(q_ref, k_ref, kp_ref, v_ref, vp_ref):
    blk, hd_dim = ATT_BLOCK, ATT_HEAD_DIM
    units = []
    for b in range(q_ref.shape[1] // blk):
        cur = slice(b * blk, (b + 1) * blk)
        old = slice((b - 1) * blk, b * blk)
        k_cat = jnp.concatenate([kp_ref[...] if b == 0 else k_ref[old, :], k_ref[cur, :]], axis=0)
        v_prev = vp_ref[...] if b == 0 else v_ref[:, old]
        v_cur = v_ref[:, cur]
        for kv in range(ATT_KV_HEADS):
            rows = slice(kv * hd_dim, (kv + 1) * hd_dim)
            v_cat = jnp.concatenate([v_prev[rows, :], v_cur[rows, :]], axis=1)
            units.append((b, cur, kv, k_cat, v_cat))
    return units


def _attention_scores(unit, q_ref):
    b, cur, kv, k_cat, _ = unit
    hd_dim = ATT_HEAD_DIM
    zeros = jnp.zeros((hd_dim, ATT_BLOCK), BF16)
    q_pad = []
    for g in range(ATT_GROUP):
        hd = kv * ATT_GROUP + g
        qh = q_ref[hd * hd_dim:(hd + 1) * hd_dim, cur]
        q_pad.append(jnp.concatenate([qh, zeros] if kv == 0 else [zeros, qh], axis=0))
    return _dot(k_cat, jnp.concatenate(q_pad, axis=1))


def _attention_softmax(unit, s_all, first, sinks_ref, layer):
    b, cur, kv, _, _ = unit
    blk = ATT_BLOCK
    key = lax.broadcasted_iota(jnp.int32, (blk, blk), 0)
    qry = lax.broadcasted_iota(jnp.int32, (blk, blk), 1)
    from_prev = key > qry
    zero_p = jnp.zeros((blk, blk), BF16)
    p_all, r_den = [], []
    for g in range(ATT_GROUP):
        sink = sinks_ref[layer * ATT_HEADS + kv * ATT_GROUP + g] * LOG2_E
        s_prev = s_all[:blk, g * blk:(g + 1) * blk]
        if b == 0:
            s_prev = s_prev + jnp.where(first, NEG_BIG, 0.0)
        s = jnp.where(from_prev, s_prev, s_all[blk:, g * blk:(g + 1) * blk])
        m = jnp.maximum(jnp.max(s, axis=0, keepdims=True), sink)
        p = jnp.exp2(s - m)
        r_den.append(1.0 / (jnp.sum(p, axis=0, keepdims=True) + jnp.exp2(sink - m)))
        pb = p.astype(BF16)
        p_all.append(jnp.concatenate([jnp.where(from_prev, pb, zero_p), jnp.where(from_prev, zero_p, pb)], axis=0))
    return jnp.concatenate(p_all, axis=1), r_den


def _attention_output(unit, probs, r_den, o_ref):
    b, cur, kv, _, v_cat = unit
    blk, hd_dim = ATT_BLOCK, ATT_HEAD_DIM
    o_all = _dot(v_cat, probs)
    for g in range(ATT_GROUP):
        hd = kv * ATT_GROUP + g
        o_ref[hd * hd_dim:(hd + 1) * hd_dim, cur] = (o_all[:, g * blk:(g + 1) * blk] * r_den[g]).astype(BF16)


def _split3(x):
    h1 = x.astype(BF16)
    r1 = x - h1.astype(F32)
    h2 = r1.astype(BF16)
    h3 = (r1 - h2.astype(F32)).astype(BF16)
    return h1, h2, h3


def _mlstm_gates_and_scores(q_ref, k_ref, gcol_ref, grow_ref):
    L = M_CHUNK
    src = lax.broadcasted_iota(jnp.int32, (L, L), 0)
    dst = lax.broadcasted_iota(jnp.int32, (L, L), 1)
    tril = jnp.where(dst <= src, 1.0, 0.0).astype(BF16)
    triu = jnp.where(src <= dst, 1.0, 0.0).astype(BF16)
    zeros = jnp.zeros((M_QK_DIM, L), BF16)
    chunks = []
    for c in range(q_ref.shape[1] // L):
        cur = slice(c * L, (c + 1) * L)
        gcol = gcol_ref[cur, :]
        grow = grow_ref[:, cur]
        bc = sum(_dot(tril, t) for t in _split3(gcol))
        br = sum(_dot(t, triu) for t in _split3(grow))
        heads = []
        for h in range(M_HEADS):
            k2 = k_ref[cur, (h // 2) * LANES:(h // 2 + 1) * LANES]
            qh = q_ref[h * M_QK_DIM:(h + 1) * M_QK_DIM, cur]
            q_pad = jnp.concatenate([qh, zeros] if h % 2 == 0 else [zeros, qh], axis=0)
            heads.append(dict(k2=k2, q_pad=q_pad, s_t=_dot(k2, q_pad)))
        chunks.append(dict(cur=cur, gcol=gcol, grow=grow, bc=bc, br=br, heads=heads))
    return chunks


def _mlstm_weights(chunks, v_ref, m_ref, n_state_rows):
    L = M_CHUNK
    src = lax.broadcasted_iota(jnp.int32, (L, L), 0)
    dst = lax.broadcasted_iota(jnp.int32, (L, L), 1)
    causal = src <= dst
    m_run = [m_ref[h:h + 1, :] for h in range(M_HEADS)]
    for ch in chunks:
        cur, gcol, grow, bc, br = ch["cur"], ch["gcol"], ch["grow"], ch["bc"], ch["br"]
        r_col = gcol - pltpu.roll(bc, LANES - M_HEADS, 1)
        for h, hd in enumerate(ch["heads"]):
            vh = v_ref[h * M_V_DIM:(h + 1) * M_V_DIM, cur]
            b_row = br[M_HEADS + h:M_HEADS + h + 1, :]
            li_row = grow[h:h + 1, :]
            m_prev = m_run[h]
            log_d = jnp.where(causal, jnp.broadcast_to(r_col[:, h:h + 1], (L, L)) + b_row, NEG_BIG)
            m_t = jnp.maximum(b_row + m_prev, jnp.max(log_d, axis=0, keepdims=True))
            w_t = jnp.exp(log_d - m_t) * hd["s_t"]
            g = b_row[:, L - 1:L]
            w_end = g - b_row + li_row
            m_loc = jnp.max(w_end, axis=1, keepdims=True)
            e_row = jnp.exp(w_end - m_loc)
            m_new = jnp.maximum(g + m_prev, m_loc)
            hd.update(
                vh=vh, m_t=m_t, w_bf=w_t.astype(BF16), w_sum=jnp.sum(w_t, axis=0, keepdims=True),
                a_int=jnp.exp(b_row + m_prev - m_t),
                lhs=jnp.concatenate([(vh.astype(F32) * e_row).astype(BF16),
                                     jnp.broadcast_to(e_row, (n_state_rows - M_V_DIM, L)).astype(BF16)], axis=0),
                decay=jnp.exp(g + m_prev - m_new), gain=jnp.exp(m_loc - m_new))
            m_run[h] = m_new
    for h in range(M_HEADS):
        m_ref[h:h + 1, :] = m_run[h]


def _mlstm_intra(chunks):
    for ch in chunks:
        for hd in ch["heads"]:
            hd["num"] = _dot(hd["vh"], hd["w_bf"])
            hd["d_state"] = _dot(hd["lhs"], hd["k2"])


def _mlstm_states(chunks, state_ref):
    for h in range(M_HEADS):
        state = state_ref[h]
        for ch in chunks:
            hd = ch["heads"][h]
            hd["state_in"] = state.astype(BF16)
            state = hd["decay"] * state + hd["gain"] * hd["d_state"]
        state_ref[h] = state
    for ch in chunks:
        for hd in ch["heads"]:
            hd["inter"] = _dot(hd["state_in"], hd["q_pad"])


def _mlstm_outputs(chunks, osig_ref, gain_ref, o_ref):
    for ch in chunks:
        cur = ch["cur"]
        for h, hd in enumerate(ch["heads"]):
            inter, a_int = hd["inter"], hd["a_int"]
            num = hd["num"] + a_int * inter[:M_V_DIM]
            den = hd["w_sum"] + a_int * inter[M_V_DIM:M_V_DIM + 1]
            hv = num / jnp.maximum(jnp.abs(den), jnp.exp(-hd["m_t"]))
            inv = lax.rsqrt(jnp.mean(hv * hv, axis=0, keepdims=True) + EPS)
            rows = slice(h * M_V_DIM, (h + 1) * M_V_DIM)
            o_ref[rows, cur] = ((hv * inv) * gain_ref[rows, :] * osig_ref[rows, cur].astype(F32)).astype(BF16)


def _mixers_kernel(sinks_ref, qa_ref, ka_ref, kap_ref, va_ref, vap_ref,
                   qm_ref, km_ref, vm_ref, osig_ref, gcol_ref, grow_ref, gain_ref,
                   att_ref, hm_ref, state_ref, m_ref, *, layer):
    first = pl.program_id(1) == 0

    @pl.when(first)
    def _():
        state_ref[...] = jnp.zeros(state_ref.shape, F32)
        m_ref[...] = jnp.zeros(m_ref.shape, F32)

    units = _attention_units(qa_ref, ka_ref, kap_ref, va_ref, vap_ref)
    scores = [_attention_scores(u, qa_ref) for u in units]
    chunks = _mlstm_gates_and_scores(qm_ref, km_ref, gcol_ref, grow_ref)
    probs = [_attention_softmax(u, s, first, sinks_ref, layer) for u, s in zip(units, scores)]
    _mlstm_weights(chunks, vm_ref, m_ref, state_ref.shape[1])
    for u, (p, r_den) in zip(units, probs):
        _attention_output(u, p, r_den, att_ref)
    _mlstm_intra(chunks)
    _mlstm_states(chunks, state_ref)
    _mlstm_outputs(chunks, osig_ref, gain_ref, hm_ref)


def _mixers(sinks, qa_t, ka, va_t, qm_t, km, vm_t, osig_t, gcol, grow, gain_b, *, layer, seq, tc):
    T = km.shape[0]
    n = seq // tc
    per = tc // ATT_BLOCK
    tile = lambda b, c: b * n + c
    prev_blk = lambda b, c: jnp.maximum(tile(b, c) * per - 1, 0)
    row = lambda w: pl.BlockSpec((tc, w), lambda b, c, *_: (tile(b, c), 0))
    colm = lambda w: pl.BlockSpec((w, tc), lambda b, c, *_: (0, tile(b, c)))
    grid_spec = pltpu.PrefetchScalarGridSpec(
        num_scalar_prefetch=1, grid=(T // seq, n),
        in_specs=[colm(ATT_Q_W), row(ATT_KV_W),
                  pl.BlockSpec((ATT_BLOCK, ATT_KV_W), lambda b, c, *_: (prev_blk(b, c), 0)),
                  colm(ATT_KV_W),
                  pl.BlockSpec((ATT_KV_W, ATT_BLOCK), lambda b, c, *_: (0, prev_blk(b, c))),
                  colm(M_QK_W), row(M_QK_W), colm(M_V_W), colm(M_V_W), row(LANES), colm(2 * M_HEADS),
                  _layer_spec(gain_b, layer)],
        out_specs=(colm(ATT_Q_W), colm(M_V_W)),
        scratch_shapes=[pltpu.VMEM((M_HEADS, M_V_DIM + SUBLANES_BF16, LANES), F32),
                        pltpu.VMEM((2 * M_HEADS, LANES), F32)])
    return pl.pallas_call(
        functools.partial(_mixers_kernel, layer=layer), grid_spec=grid_spec,
        out_shape=(jax.ShapeDtypeStruct((ATT_Q_W, T), BF16), jax.ShapeDtypeStruct((M_V_W, T), BF16)),
        compiler_params=_params("parallel", "arbitrary"), name="mixers",
    )(sinks, qa_t, ka, ka, va_t, va_t, qm_t, km, vm_t, osig_t, gcol, grow, gain_b)


def _merge_kernel(x_ref, att_ref, hm_ref, sg_ref, wa_ref, wm_ref, wo_ref, o_ref):
    D = x_ref.shape[1]
    a = _dot_tn(att_ref[...], wa_ref[...])
    m = _dot_tn(hm_ref[...], wm_ref[...])
    mixed = sg_ref[:, :D].astype(F32) * a + sg_ref[:, D:].astype(F32) * m
    o_ref[...] = x_ref[...] + _dot(mixed.astype(BF16), wo_ref[...])


def _merge(x, att_t, hm_t, sg, wa, wm, wo, *, layer, tm):
    T, D = x.shape
    row = lambda w: pl.BlockSpec((tm, w), lambda i: (i, 0))
    colm = lambda w: pl.BlockSpec((w, tm), lambda i: (0, i))
    return pl.pallas_call(
        _merge_kernel, grid=(T // tm,),
        in_specs=[row(D), colm(ATT_Q_W), colm(M_V_W), row(2 * D),
                  _layer_spec(wa, layer), _layer_spec(wm, layer), _layer_spec(wo, layer)],
        out_specs=row(D), out_shape=jax.ShapeDtypeStruct((T, D), F32),
        compiler_params=_params("parallel"), name="merge",
    )(x, att_t, hm_t, sg, wa, wm, wo)


def _ffn_kernel(x_ref, gain_ref, w1_ref, w2_ref, o_ref, *, n_chunks):
    x = x_ref[...]
    inv = lax.rsqrt(jnp.mean(x * x, axis=-1, keepdims=True) + EPS)
    h = ((x * inv) * gain_ref[...]).astype(BF16)
    fc = w1_ref.shape[1] // n_chunks
    acc = x
    for c in range(n_chunks):
        u = jnp.maximum(_dot(h, w1_ref[:, c * fc:(c + 1) * fc]), 0.0)
        acc = acc + _dot((u * u).astype(BF16), w2_ref[c * fc:(c + 1) * fc, :])
    o_ref[...] = acc


def _ffn(x, gain, w1, w2, *, layer, tm):
    T, D = x.shape
    row = pl.BlockSpec((tm, D), lambda i: (i, 0))
    return pl.pallas_call(
        functools.partial(_ffn_kernel, n_chunks=4), grid=(T // tm,),
        in_specs=[row, _layer_spec(gain, layer), _layer_spec(w1, layer), _layer_spec(w2, layer)],
        out_specs=row, out_shape=jax.ShapeDtypeStruct((T, D), F32),
        compiler_params=_params("parallel"), name="ffn",
    )(x, gain, w1, w2)


def _rope_tables(seq):
    pos = jnp.arange(seq, dtype=F32)
    inv_freq = ROPE_THETA ** (-jnp.arange(0, ROPE_DIM, 2, dtype=F32) / ROPE_DIM)
    ang = pos[:, None] * inv_freq[None, :]
    cos, sin = jnp.cos(ang), jnp.sin(ang)
    pad = jnp.zeros((seq, ATT_HEAD_DIM - ROPE_DIM), F32)
    zeros = jnp.zeros((seq, ROPE_HALF), F32)
    rc = jnp.concatenate([cos, cos, pad + 1.0], axis=1)
    rs1 = jnp.concatenate([-sin, zeros, pad], axis=1)
    rs2 = jnp.concatenate([zeros, sin, pad], axis=1)
    tile = lambda t: jnp.concatenate([t, t], axis=1)
    return cos.T, sin.T, tile(rc), tile(rs1), tile(rs2)


def _prep_w_in(w_in):
    wb = w_in.astype(BF16)
    seg = lambda i: wb[:, :, IN_OFFS[i]:IN_OFFS[i + 1]]
    q_a, k_a, v_a, q_m, k_m, v_m, o_m, if_m, gates = (seg(i) for i in range(len(IN_WIDTHS)))
    w_fm = jnp.swapaxes(jnp.concatenate([o_m, q_a, v_a, q_m, v_m], axis=2), 1, 2)
    if_pad = jnp.pad(if_m, ((0, 0), (0, 0), (0, LANES - 2 * M_HEADS)))
    w_tm = jnp.concatenate([gates, k_a, k_m, if_pad], axis=2)
    return w_fm, w_tm


def kernel(x, norm_mix, w_in, att_q_norm, att_k_norm, att_sinks, m_gate_bias, m_head_norm,
           w_att_branch, w_m_branch, w_out, norm_ffn, w_ff1, w_ff2):
    B, S, D = x.shape
    T = B * S
    depth = w_in.shape[0]
    tm = min(TOKEN_TILE, S)
    cos_t, sin_t, rc, rs1, rs2 = _rope_tables(S)
    w_fm, w_tm = _prep_w_in(w_in)
    wa, wm, wo = w_att_branch.astype(BF16), w_m_branch.astype(BF16), w_out.astype(BF16)
    w1, w2 = w_ff1.astype(BF16), w_ff2.astype(BF16)
    bias = jnp.pad(m_gate_bias, ((0, 0), (0, LANES - 2 * M_HEADS))).reshape(depth, 1, LANES)
    qgain = jnp.broadcast_to(att_q_norm[:, :, None], (depth, ATT_HEAD_DIM, tm))
    kgain = jnp.tile(att_k_norm, (1, 2)).reshape(depth, 1, LANES)
    mgain = jnp.broadcast_to(m_head_norm[:, :, None], (depth, M_V_W, LANES))
    gain_mix = norm_mix.reshape(depth, 1, D)
    gain_ffn = norm_ffn.reshape(depth, 1, D)
    sinks = att_sinks.reshape(depth * ATT_HEADS)
    xt = x.reshape(T, D)
    for l in range(depth):
        qa_t, va_t, qm_t, vm_t, osig_t, ka, km, sg, gcol, grow = _in_proj(
            xt, gain_mix, w_fm, w_tm, bias, qgain, kgain, cos_t, sin_t, rc, rs1, rs2, layer=l, seq=S, tm=tm)
        att_t, hm_t = _mixers(sinks, qa_t, ka, va_t, qm_t, km, vm_t, osig_t, gcol, grow, mgain, layer=l, seq=S, tc=tm)
        xt = _merge(xt, att_t, hm_t, sg, wa, wm, wo, layer=l, tm=tm)
        xt = _ffn(xt, gain_ffn, w1, w2, layer=l, tm=tm)
    return xt.reshape(B, S, D)
```

```python
import functools

import jax
import jax.numpy as jnp
import numpy as np
from jax import lax
from jax.experimental import pallas as pl
from jax.experimental.pallas import tpu as pltpu

D_MODEL = 1024
ATT_HEADS = 8
ATT_KV_HEADS = 2
ATT_GROUP = ATT_HEADS // ATT_KV_HEADS
ATT_HEAD_DIM = 64
ATT_BLOCK = 128
ROPE_DIM = ATT_HEAD_DIM // 4
ROPE_HALF = ROPE_DIM // 2
ROPE_THETA = 500000.0
M_HEADS = 4
M_QK_DIM = 64
M_V_DIM = 128
M_CHUNK = 128
D_FF = 4 * D_MODEL
EPS = 1e-6

ATT_Q_W = ATT_HEADS * ATT_HEAD_DIM
ATT_KV_W = ATT_KV_HEADS * ATT_HEAD_DIM
M_QK_W = M_HEADS * M_QK_DIM
M_V_W = M_HEADS * M_V_DIM
IN_WIDTHS = (ATT_Q_W, ATT_KV_W, ATT_KV_W, M_QK_W, M_QK_W, M_V_W, M_V_W, 2 * M_HEADS, 2 * D_MODEL)
IN_OFFS = tuple(int(o) for o in np.concatenate([[0], np.cumsum(IN_WIDTHS)]))
LANES = 128
SUBLANES_BF16 = 16
FM_WIDTHS = (M_V_W, ATT_Q_W, ATT_KV_W, M_QK_W, M_V_W)
FM_OFFS = tuple(int(o) for o in np.concatenate([[0], np.cumsum(FM_WIDTHS)]))
TM_WIDTHS = (2 * D_MODEL, ATT_KV_W, M_QK_W, LANES)
TM_OFFS = tuple(int(o) for o in np.concatenate([[0], np.cumsum(TM_WIDTHS)]))

NEG_BIG = -1e30
LOG2_E = 1.4426950408889634
VMEM_LIMIT = 56 * 1024 * 1024
TOKEN_TILE = 512

BF16 = jnp.bfloat16
F32 = jnp.float32


def _dot(a, b):
    return jnp.dot(a, b, preferred_element_type=F32)


def _dot_nt(a, b):
    return lax.dot_general(a, b, (((1,), (1,)), ((), ())), preferred_element_type=F32)


def _dot_tn(a, b):
    return lax.dot_general(a, b, (((0,), (0,)), ((), ())), preferred_element_type=F32)


def _sigmoid(x):
    return 1.0 / (1.0 + jnp.exp(-x))


def _log_sigmoid(x):
    return jnp.minimum(x, 0.0) - jnp.log(1.0 + jnp.exp(-jnp.abs(x)))


def _const_spec(shape):
    return pl.BlockSpec(shape, lambda *_: (0,) * len(shape))


def _layer_spec(arr, layer, single_buffer=False):
    mode = dict(pipeline_mode=pl.Buffered(1)) if single_buffer else {}
    return pl.BlockSpec((None,) + arr.shape[1:], lambda *_: (layer,) + (0,) * (arr.ndim - 1), **mode)


def _params(*sem):
    return pltpu.CompilerParams(dimension_semantics=sem, vmem_limit_bytes=VMEM_LIMIT)


def _in_proj_kernel(x_ref, gain_ref, w_fm_ref, w_tm_ref, bias_ref, qgain_ref, kgain_ref,
                    cos_ref, sin_ref, rc_ref, rs1_ref, rs2_ref,
                    qa_ref, va_ref, qm_ref, vm_ref, om_ref, ka_ref, km_ref, sg_ref, gcol_ref, grow_ref):
    x = x_ref[...]
    inv = lax.rsqrt(jnp.mean(x * x, axis=-1, keepdims=True) + EPS)
    h = ((x * inv) * gain_ref[...]).astype(BF16)
    tm = x.shape[0]

    sg_ref[...] = _sigmoid(_dot(h, w_tm_ref[:, TM_OFFS[0]:TM_OFFS[1]])).astype(BF16)
    om_ref[...] = _sigmoid(_dot_nt(w_fm_ref[FM_OFFS[0]:FM_OFFS[1], :], h)).astype(BF16)

    zq = _dot_nt(w_fm_ref[FM_OFFS[1]:FM_OFFS[2], :], h)
    cos, sin = cos_ref[...], sin_ref[...]
    qgain = qgain_ref[...] * (ATT_HEAD_DIM ** -0.5 * LOG2_E)
    for hd in range(ATT_HEADS):
        z = zq[hd * ATT_HEAD_DIM:(hd + 1) * ATT_HEAD_DIM, :]
        inv_h = lax.rsqrt(jnp.mean(z * z, axis=0, keepdims=True) + EPS)
        y = (z * inv_h) * qgain
        x1, x2 = y[:ROPE_HALF], y[ROPE_HALF:ROPE_DIM]
        y = jnp.concatenate([x1 * cos - x2 * sin, x2 * cos + x1 * sin, y[ROPE_DIM:]], axis=0)
        qa_ref[hd * ATT_HEAD_DIM:(hd + 1) * ATT_HEAD_DIM, :] = y.astype(BF16)

    zn = _dot(h, w_tm_ref[:, TM_OFFS[1]:TM_OFFS[4]])
    zr = _dot_nt(w_fm_ref[FM_OFFS[2]:FM_OFFS[5], :], h)
    va_ref[...] = zr[:ATT_KV_W, :].astype(BF16)
    qm_ref[...] = (zr[ATT_KV_W:ATT_KV_W + M_QK_W, :] * (M_QK_DIM ** -0.5)).astype(BF16)
    vm_ref[...] = zr[ATT_KV_W + M_QK_W:, :].astype(BF16)

    lane = lax.broadcasted_iota(jnp.int32, (tm, LANES), 1)
    low = lane < ATT_HEAD_DIM
    zk = zn[:, :ATT_KV_W]
    s = zk * zk
    r0 = jnp.sum(jnp.where(low, s, 0.0), axis=-1, keepdims=True)
    r1 = jnp.sum(jnp.where(low, 0.0, s), axis=-1, keepdims=True)
    inv_k = lax.rsqrt(jnp.where(low, r0, r1) * (1.0 / ATT_HEAD_DIM) + EPS)
    y = (zk * inv_k) * kgain_ref[...]
    up = pltpu.roll(y, LANES - ROPE_HALF, 1)
    dn = pltpu.roll(y, ROPE_HALF, 1)
    ka_ref[...] = (y * rc_ref[...] + up * rs1_ref[...] + dn * rs2_ref[...]).astype(BF16)
    km_ref[...] = zn[:, ATT_KV_W:ATT_KV_W + M_QK_W].astype(BF16)
    zi = zn[:, ATT_KV_W + M_QK_W:] + bias_ref[...]
    g = jnp.where(lane < M_HEADS, zi, jnp.where(lane < 2 * M_HEADS, _log_sigmoid(zi), 0.0)) * LOG2_E
    gcol_ref[...] = g
    grow_ref[...] = g.T[:2 * M_HEADS, :]


def _in_proj(x, gain, w_fm, w_tm, bias, qgain, kgain, cos_t, sin_t, rc, rs1, rs2, *, layer, seq, tm):
    T, D = x.shape
    n_pos = seq // tm
    row = lambda w: pl.BlockSpec((tm, w), lambda i: (i, 0))
    colm = lambda w: pl.BlockSpec((w, tm), lambda i: (0, i))
    pos = pl.BlockSpec((tm, LANES), lambda i: (i % n_pos, 0))
    pos_t = pl.BlockSpec((ROPE_HALF, tm), lambda i: (0, i % n_pos))
    fm = lambda w: jax.ShapeDtypeStruct((w, T), BF16)
    out_shapes = (fm(ATT_Q_W), fm(ATT_KV_W), fm(M_QK_W), fm(M_V_W), fm(M_V_W),
                  jax.ShapeDtypeStruct((T, ATT_KV_W), BF16),
                  jax.ShapeDtypeStruct((T, M_QK_W), BF16),
                  jax.ShapeDtypeStruct((T, 2 * D), BF16),
                  jax.ShapeDtypeStruct((T, LANES), F32),
                  jax.ShapeDtypeStruct((2 * M_HEADS, T), F32))
    out_specs = (colm(ATT_Q_W), colm(ATT_KV_W), colm(M_QK_W), colm(M_V_W), colm(M_V_W),
                 row(ATT_KV_W), row(M_QK_W), row(2 * D), row(LANES), colm(2 * M_HEADS))
    in_specs = [row(D)] + [_layer_spec(a, layer) for a in (gain, w_fm, w_tm, bias, qgain, kgain)] + [
        pos_t, pos_t, pos, pos, pos]
    return pl.pallas_call(
        _in_proj_kernel, grid=(T // tm,), in_specs=in_specs, out_specs=out_specs, out_shape=out_shapes,
        compiler_params=_params("parallel"), name="in_proj",
    )(x, gain, w_fm, w_tm, bias, qgain, kgain, cos_t, sin_t, rc, rs1, rs2)


def _attention_units(q_ref, k_ref, kp_ref, v_ref, vp_ref):
    blk, hd_dim = ATT_BLOCK, ATT_HEAD_DIM
    units = []
    for b in range(q_ref.shape[1] // blk):
        cur = slice(b * blk, (b + 1) * blk)
        old = slice((b - 1) * blk, b * blk)
        k_cat = jnp.concatenate([kp_ref[...] if b == 0 else k_ref[old, :], k_ref[cur, :]], axis=0)
        v_prev = vp_ref[...] if b == 0 else v_ref[:, old]
        v_cur = v_ref[:, cur]
        for kv in range(ATT_KV_HEADS):
            rows = slice(kv * hd_dim, (kv + 1) * hd_dim)
            v_cat = jnp.concatenate([v_prev[rows, :], v_cur[rows, :]], axis=1)
            units.append((b, cur, kv, k_cat, v_cat))
    return units


def _attention_scores(unit, q_ref):
    b, cur, kv, k_cat, _ = unit
    hd_dim = ATT_HEAD_DIM
    zeros = jnp.zeros((hd_dim, ATT_BLOCK), BF16)
    q_pad = []
    for g in range(ATT_GROUP):
        hd = kv * ATT_GROUP + g
        qh = q_ref[hd * hd_dim:(hd + 1) * hd_dim, cur]
        q_pad.append(jnp.concatenate([qh, zeros] if kv == 0 else [zeros, qh], axis=0))
    return _dot(k_cat, jnp.concatenate(q_pad, axis=1))


def _attention_softmax(unit, s_all, first, sinks_ref, layer):
    b, cur, kv, _, _ = unit
    blk = ATT_BLOCK
    key = lax.broadcasted_iota(jnp.int32, (blk, blk), 0)
    qry = lax.broadcasted_iota(jnp.int32, (blk, blk), 1)
    from_prev = key > qry
    zero_p = jnp.zeros((blk, blk), BF16)
    p_all, r_den = [], []
    for g in range(ATT_GROUP):
        sink = sinks_ref[layer * ATT_HEADS + kv * ATT_GROUP + g] * LOG2_E
        s_prev = s_all[:blk, g * blk:(g + 1) * blk]
        if b == 0:
            s_prev = s_prev + jnp.where(first, NEG_BIG, 0.0)
        s = jnp.where(from_prev, s_prev, s_all[blk:, g * blk:(g + 1) * blk])
        m = jnp.maximum(jnp.max(s, axis=0, keepdims=True), sink)
        p = jnp.exp2(s - m)
        r_den.append(1.0 / (jnp.sum(p, axis=0, keepdims=True) + jnp.exp2(sink - m)))
        pb = p.astype(BF16)
        p_all.append(jnp.concatenate([jnp.where(from_prev, pb, zero_p), jnp.where(from_prev, zero_p, pb)], axis=0))
    return jnp.concatenate(p_all, axis=1), r_den


def _attention_output(unit, probs, r_den, o_ref):
    b, cur, kv, _, v_cat = unit
    blk, hd_dim = ATT_BLOCK, ATT_HEAD_DIM
    o_all = _dot(v_cat, probs)
    for g in range(ATT_GROUP):
        hd = kv * ATT_GROUP + g
        o_ref[hd * hd_dim:(hd + 1) * hd_dim, cur] = (o_all[:, g * blk:(g + 1) * blk] * r_den[g]).astype(BF16)


def _bf16_part(x):
    bits = pltpu.bitcast(x, jnp.uint32) & jnp.uint32(0xFFFF0000)
    return pltpu.bitcast(bits, F32)


def _cumsum_rows(tril, x):
    lane = lax.broadcasted_iota(jnp.int32, x.shape, 1)
    w = 2 * M_HEADS
    h1 = _bf16_part(x)
    r1 = x - h1
    h2 = _bf16_part(r1)
    h3 = r1 - h2
    packed = jnp.where(lane < w, h1, jnp.where(lane < 2 * w, pltpu.roll(h2, w, 1), pltpu.roll(h3, 2 * w, 1)))
    c = _dot(tril, packed.astype(BF16))
    return c + pltpu.roll(c, LANES - w, 1) + pltpu.roll(c, LANES - 2 * w, 1)


def _split3(x):
    h1 = _bf16_part(x)
    r1 = x - h1
    h2 = _bf16_part(r1)
    return h1.astype(BF16), h2.astype(BF16), (r1 - h2).astype(BF16)


def _mlstm_gates_and_scores(q_ref, k_ref, gcol_ref, grow_ref):
    L = M_CHUNK
    src = lax.broadcasted_iota(jnp.int32, (L, L), 0)
    dst = lax.broadcasted_iota(jnp.int32, (L, L), 1)
    tril = jnp.where(dst <= src, 1.0, 0.0).astype(BF16)
    triu = jnp.where(src <= dst, 1.0, 0.0).astype(BF16)
    zeros = jnp.zeros((M_QK_DIM, L), BF16)
    chunks = []
    for c in range(q_ref.shape[1] // L):
        cur = slice(c * L, (c + 1) * L)
        gcol = gcol_ref[cur, :]
        grow = grow_ref[:, cur]
        bc = _cumsum_rows(tril, gcol)
        br = sum(_dot(t, triu) for t in _split3(grow))
        pairs = []
        for j in range(M_HEADS // 2):
            k2 = k_ref[cur, j * LANES:(j + 1) * LANES]
            q2 = q_ref[j * LANES:(j + 1) * LANES, cur]
            q_pad = jnp.concatenate([jnp.concatenate([q2[:M_QK_DIM], zeros], axis=0),
                                     jnp.concatenate([zeros, q2[M_QK_DIM:]], axis=0)], axis=1)
            pairs.append(dict(k2=k2, q_pad=q_pad, s_t=_dot(k2, q_pad), heads=[{}, {}]))
        chunks.append(dict(cur=cur, gcol=gcol, grow=grow, bc=bc, br=br, pairs=pairs))
    return chunks


def _mlstm_weights(chunks, v_ref, m_ref, n_state_rows):
    L = M_CHUNK
    src = lax.broadcasted_iota(jnp.int32, (L, L), 0)
    dst = lax.broadcasted_iota(jnp.int32, (L, L), 1)
    causal = src <= dst
    zeros = jnp.zeros((L, L), BF16)
    m_run = [m_ref[h:h + 1, :] for h in range(M_HEADS)]
    for ch in chunks:
        cur, gcol, grow, bc, br = ch["cur"], ch["gcol"], ch["grow"], ch["bc"], ch["br"]
        r_col = gcol - pltpu.roll(bc, LANES - M_HEADS, 1)
        for j, pr in enumerate(ch["pairs"]):
            for hh, hd in enumerate(pr["heads"]):
                h = 2 * j + hh
                vh = v_ref[h * M_V_DIM:(h + 1) * M_V_DIM, cur]
                b_row = br[M_HEADS + h:M_HEADS + h + 1, :]
                li_row = grow[h:h + 1, :]
                m_prev = m_run[h]
                log_d = jnp.where(causal, jnp.broadcast_to(r_col[:, h:h + 1], (L, L)) + b_row, NEG_BIG)
                m_t = jnp.maximum(b_row + m_prev, jnp.max(log_d, axis=0, keepdims=True))
                w_t = jnp.exp2(log_d - m_t) * pr["s_t"][:, hh * L:(hh + 1) * L]
                g = b_row[:, L - 1:L]
                w_end = g - b_row + li_row
                m_loc = jnp.max(w_end, axis=1, keepdims=True)
                e_row = jnp.exp2(w_end - m_loc).astype(BF16)
                m_new = jnp.maximum(g + m_prev, m_loc)
                hd.update(
                    vh=vh, m_t=m_t, w_bf=w_t.astype(BF16), w_sum=jnp.sum(w_t, axis=0, keepdims=True),
                    a_int=jnp.exp2(b_row + m_prev - m_t),
                    lhs=jnp.concatenate([vh * e_row, jnp.broadcast_to(e_row, (n_state_rows - M_V_DIM, L))], axis=0),
                    decay=jnp.exp2(g + m_prev - m_new), gain=jnp.exp2(m_loc - m_new))
                m_run[h] = m_new
            ha, hb = pr["heads"]
            pr["v_cat"] = jnp.concatenate([ha["vh"], hb["vh"]], axis=1)
            pr["w_diag"] = jnp.concatenate([jnp.concatenate([ha["w_bf"], zeros], axis=1),
                                            jnp.concatenate([zeros, hb["w_bf"]], axis=1)], axis=0)
    for h in range(M_HEADS):
        m_ref[h:h + 1, :] = m_run[h]


def _mlstm_intra(chunks):
    for ch in chunks:
        for pr in ch["pairs"]:
            pr["num"] = _dot(pr["v_cat"], pr["w_diag"])
            for hd in pr["heads"]:
                hd["d_state"] = _dot(hd["lhs"], pr["k2"])


def _mlstm_states(chunks, state_ref):
    first_head = lax.broadcasted_iota(jnp.int32, state_ref.shape[1:], 1) < M_QK_DIM
    for j in range(M_HEADS // 2):
        state = state_ref[j]
        for ch in chunks:
            pr = ch["pairs"][j]
            ha, hb = pr["heads"]
            pr["state_in"] = state.astype(BF16)
            state = (jnp.where(first_head, ha["decay"], hb["decay"]) * state
                     + jnp.where(first_head, ha["gain"], hb["gain"]) * jnp.where(first_head, ha["d_state"], hb["d_state"]))
        state_ref[j] = state
    for ch in chunks:
        for pr in ch["pairs"]:
            pr["inter"] = _dot(pr["state_in"], pr["q_pad"])


def _mlstm_outputs(chunks, osig_ref, gain_ref, o_ref):
    L = M_CHUNK
    for ch in chunks:
        cur = ch["cur"]
        for j, pr in enumerate(ch["pairs"]):
            for hh, hd in enumerate(pr["heads"]):
                h = 2 * j + hh
                a_int = hd["a_int"]
                inter = pr["inter"][:, hh * L:(hh + 1) * L]
                num = pr["num"][:, hh * L:(hh + 1) * L] + a_int * inter[:M_V_DIM]
                den = hd["w_sum"] + a_int * inter[M_V_DIM:M_V_DIM + 1]
                d = jnp.maximum(jnp.abs(den), jnp.exp2(-hd["m_t"]))
                scale = lax.rsqrt(jnp.mean(num * num, axis=0, keepdims=True) + EPS * (d * d))
                rows = slice(h * M_V_DIM, (h + 1) * M_V_DIM)
                o_ref[rows, cur] = ((num * scale) * (gain_ref[rows, :] * osig_ref[rows, cur].astype(F32))).astype(BF16)


def _mixers_kernel(sinks_ref, qa_ref, ka_ref, kap_ref, va_ref, vap_ref,
                   qm_ref, km_ref, vm_ref, osig_ref, gcol_ref, grow_ref, gain_ref,
                   att_ref, hm_ref, state_ref, m_ref, *, layer):
    first = pl.program_id(1) == 0

    @pl.when(first)
    def _():
        state_ref[...] = jnp.zeros(state_ref.shape, F32)
        m_ref[...] = jnp.zeros(m_ref.shape, F32)

    units = _attention_units(qa_ref, ka_ref, kap_ref, va_ref, vap_ref)
    scores = [_attention_scores(u, qa_ref) for u in units]
    chunks = _mlstm_gates_and_scores(qm_ref, km_ref, gcol_ref, grow_ref)
    probs = [_attention_softmax(u, s, first, sinks_ref, layer) for u, s in zip(units, scores)]
    _mlstm_weights(chunks, vm_ref, m_ref, state_ref.shape[1])
    for u, (p, r_den) in zip(units, probs):
        _attention_output(u, p, r_den, att_ref)
    _mlstm_intra(chunks)
    _mlstm_states(chunks, state_ref)
    _mlstm_outputs(chunks, osig_ref, gain_ref, hm_ref)


def _mixers(sinks, qa_t, ka, va_t, qm_t, km, vm_t, osig_t, gcol, grow, gain_b, *, layer, seq, tc):
    T = km.shape[0]
    n = seq // tc
    per = tc // ATT_BLOCK
    tile = lambda b, c: b * n + c
    prev_blk = lambda b, c: jnp.maximum(tile(b, c) * per - 1, 0)
    row = lambda w: pl.BlockSpec((tc, w), lambda b, c, *_: (tile(b, c), 0))
    colm = lambda w: pl.BlockSpec((w, tc), lambda b, c, *_: (0, tile(b, c)))
    grid_spec = pltpu.PrefetchScalarGridSpec(
        num_scalar_prefetch=1, grid=(T // seq, n),
        in_specs=[colm(ATT_Q_W), row(ATT_KV_W),
                  pl.BlockSpec((ATT_BLOCK, ATT_KV_W), lambda b, c, *_: (prev_blk(b, c), 0)),
                  colm(ATT_KV_W),
                  pl.BlockSpec((ATT_KV_W, ATT_BLOCK), lambda b, c, *_: (0, prev_blk(b, c))),
                  colm(M_QK_W), row(M_QK_W), colm(M_V_W), colm(M_V_W), row(LANES), colm(2 * M_HEADS),
                  _layer_spec(gain_b, layer)],
        out_specs=(colm(ATT_Q_W), colm(M_V_W)),
        scratch_shapes=[pltpu.VMEM((M_HEADS // 2, M_V_DIM + SUBLANES_BF16, LANES), F32),
                        pltpu.VMEM((2 * M_HEADS, LANES), F32)])
    return pl.pallas_call(
        functools.partial(_mixers_kernel, layer=layer), grid_spec=grid_spec,
        out_shape=(jax.ShapeDtypeStruct((ATT_Q_W, T), BF16), jax.ShapeDtypeStruct((M_V_W, T), BF16)),
        compiler_params=_params("parallel", "arbitrary"), name="mixers",
    )(sinks, qa_t, ka, ka, va_t, va_t, qm_t, km, vm_t, osig_t, gcol, grow, gain_b)


def _post_kernel(x_ref, att_ref, hm_ref, sg_ref, wa_ref, wm_ref, wo_ref, gain_ref, w1_ref, w2_ref, o_ref, *, n_chunks):
    D = x_ref.shape[1]
    a = _dot_tn(att_ref[...], wa_ref[...])
    m = _dot_tn(hm_ref[...], wm_ref[...])
    mixed = sg_ref[:, :D].astype(F32) * a + sg_ref[:, D:].astype(F32) * m
    x = x_ref[...] + _dot(mixed.astype(BF16), wo_ref[...])
    inv = lax.rsqrt(jnp.mean(x * x, axis=-1, keepdims=True) + EPS)
    h = ((x * inv) * gain_ref[...]).astype(BF16)
    fc = w1_ref.shape[1] // n_chunks
    acc = x
    for c in range(n_chunks):
        u = jnp.maximum(_dot(h, w1_ref[:, c * fc:(c + 1) * fc]), 0.0)
        acc = acc + _dot((u * u).astype(BF16), w2_ref[c * fc:(c + 1) * fc, :])
    o_ref[...] = acc


def _post(x, att_t, hm_t, sg, wa, wm, wo, gain, w1, w2, *, layer, tm):
    T, D = x.shape
    row = lambda w: pl.BlockSpec((tm, w), lambda i: (i, 0))
    colm = lambda w: pl.BlockSpec((w, tm), lambda i: (0, i))
    return pl.pallas_call(
        functools.partial(_post_kernel, n_chunks=4), grid=(T // tm,),
        in_specs=[row(D), colm(ATT_Q_W), colm(M_V_W), row(2 * D)] + [
            _layer_spec(a, layer, single_buffer=True) for a in (wa, wm, wo, gain, w1, w2)],
        out_specs=row(D), out_shape=jax.ShapeDtypeStruct((T, D), F32),
        compiler_params=_params("parallel"), name="post",
    )(x, att_t, hm_t, sg, wa, wm, wo, gain, w1, w2)


def _rope_tables(seq):
    pos = jnp.arange(seq, dtype=F32)
    inv_freq = ROPE_THETA ** (-jnp.arange(0, ROPE_DIM, 2, dtype=F32) / ROPE_DIM)
    ang = pos[:, None] * inv_freq[None, :]
    cos, sin = jnp.cos(ang), jnp.sin(ang)
    pad = jnp.zeros((seq, ATT_HEAD_DIM - ROPE_DIM), F32)
    zeros = jnp.zeros((seq, ROPE_HALF), F32)
    rc = jnp.concatenate([cos, cos, pad + 1.0], axis=1)
    rs1 = jnp.concatenate([-sin, zeros, pad], axis=1)
    rs2 = jnp.concatenate([zeros, sin, pad], axis=1)
    tile = lambda t: jnp.concatenate([t, t], axis=1)
    return cos.T, sin.T, tile(rc), tile(rs1), tile(rs2)


def _prep_w_in(w_in):
    wb = w_in.astype(BF16)
    seg = lambda i: wb[:, :, IN_OFFS[i]:IN_OFFS[i + 1]]
    q_a, k_a, v_a, q_m, k_m, v_m, o_m, if_m, gates = (seg(i) for i in range(len(IN_WIDTHS)))
    w_fm = jnp.swapaxes(jnp.concatenate([o_m, q_a, v_a, q_m, v_m], axis=2), 1, 2)
    if_pad = jnp.pad(if_m, ((0, 0), (0, 0), (0, LANES - 2 * M_HEADS)))
    w_tm = jnp.concatenate([gates, k_a, k_m, if_pad], axis=2)
    return w_fm, w_tm


def kernel(x, norm_mix, w_in, att_q_norm, att_k_norm, att_sinks, m_gate_bias, m_head_norm,
           w_att_branch, w_m_branch, w_out, norm_ffn, w_ff1, w_ff2):
    B, S, D = x.shape
    T = B * S
    depth = w_in.shape[0]
    tm = min(TOKEN_TILE, S)
    cos_t, sin_t, rc, rs1, rs2 = _rope_tables(S)
    w_fm, w_tm = _prep_w_in(w_in)
    wa, wm, wo = w_att_branch.astype(BF16), w_m_branch.astype(BF16), w_out.astype(BF16)
    w1, w2 = w_ff1.astype(BF16), w_ff2.astype(BF16)
    bias = jnp.pad(m_gate_bias, ((0, 0), (0, LANES - 2 * M_HEADS))).reshape(depth, 1, LANES)
    qgain = jnp.broadcast_to(att_q_norm[:, :, None], (depth, ATT_HEAD_DIM, tm))
    kgain = jnp.tile(att_k_norm, (1, 2)).reshape(depth, 1, LANES)
    mgain = jnp.broadcast_to(m_head_norm[:, :, None], (depth, M_V_W, LANES))
    gain_mix = norm_mix.reshape(depth, 1, D)
    gain_ffn = norm_ffn.reshape(depth, 1, D)
    sinks = att_sinks.reshape(depth * ATT_HEADS)
    xt = x.reshape(T, D)
    for l in range(depth):
        qa_t, va_t, qm_t, vm_t, osig_t, ka, km, sg, gcol, grow = _in_proj(
            xt, gain_mix, w_fm, w_tm, bias, qgain, kgain, cos_t, sin_t, rc, rs1, rs2, layer=l, seq=S, tm=tm)
        att_t, hm_t = _mixers(sinks, qa_t, ka, va_t, qm_t, km, vm_t, osig_t, gcol, grow, mgain, layer=l, seq=S, tc=tm)
        xt = _post(xt, att_t, hm_t, sg, wa, wm, wo, gain_ffn, w1, w2, layer=l, tm=tm)
    return xt.reshape(B, S, D)
```

```python
import functools

import jax
import jax.numpy as jnp
import numpy as np
from jax import lax
from jax.experimental import pallas as pl
from jax.experimental.pallas import tpu as pltpu

D_MODEL = 1024
ATT_HEADS = 8
ATT_KV_HEADS = 2
ATT_GROUP = ATT_HEADS // ATT_KV_HEADS
ATT_HEAD_DIM = 64
ATT_BLOCK = 128
ROPE_DIM = ATT_HEAD_DIM // 4
ROPE_HALF = ROPE_DIM // 2
ROPE_THETA = 500000.0
M_HEADS = 4
M_QK_DIM = 64
M_V_DIM = 128
M_CHUNK = 128
D_FF = 4 * D_MODEL
EPS = 1e-6

ATT_Q_W = ATT_HEADS * ATT_HEAD_DIM
ATT_KV_W = ATT_KV_HEADS * ATT_HEAD_DIM
M_QK_W = M_HEADS * M_QK_DIM
M_V_W = M_HEADS * M_V_DIM
IN_WIDTHS = (ATT_Q_W, ATT_KV_W, ATT_KV_W, M_QK_W, M_QK_W, M_V_W, M_V_W, 2 * M_HEADS, 2 * D_MODEL)
IN_OFFS = tuple(int(o) for o in np.concatenate([[0], np.cumsum(IN_WIDTHS)]))
LANES = 128
SUBLANES_BF16 = 16
FM_WIDTHS = (M_V_W, ATT_Q_W, ATT_KV_W, M_QK_W, M_V_W)
FM_OFFS = tuple(int(o) for o in np.concatenate([[0], np.cumsum(FM_WIDTHS)]))
TM_WIDTHS = (2 * D_MODEL, ATT_KV_W, M_QK_W, LANES)
TM_OFFS = tuple(int(o) for o in np.concatenate([[0], np.cumsum(TM_WIDTHS)]))

NEG_BIG = -1e30
LOG2_E = 1.4426950408889634
VMEM_LIMIT = 56 * 1024 * 1024
TOKEN_TILE = 512
IN_PROJ_TILE = 1024
PREP_ROWS = 256

BF16 = jnp.bfloat16
F32 = jnp.float32


def _dot(a, b):
    return jnp.dot(a, b, preferred_element_type=F32)


def _dot_nt(a, b):
    return lax.dot_general(a, b, (((1,), (1,)), ((), ())), preferred_element_type=F32)


def _dot_tn(a, b):
    return lax.dot_general(a, b, (((0,), (0,)), ((), ())), preferred_element_type=F32)


def _sigmoid(x):
    return 1.0 / (1.0 + jnp.exp(-x))


def _log_sigmoid(x):
    return jnp.minimum(x, 0.0) - jnp.log(1.0 + jnp.exp(-jnp.abs(x)))


def _const_spec(shape):
    return pl.BlockSpec(shape, lambda *_: (0,) * len(shape))


def _layer_spec(arr, layer, single_buffer=False):
    mode = dict(pipeline_mode=pl.Buffered(1)) if single_buffer else {}
    return pl.BlockSpec((None,) + arr.shape[1:], lambda *_: (layer,) + (0,) * (arr.ndim - 1), **mode)


def _params(*sem):
    return pltpu.CompilerParams(dimension_semantics=sem, vmem_limit_bytes=VMEM_LIMIT)


def _in_proj_kernel(x_ref, gain_ref, w_fm_ref, w_tm_ref, bias_ref, qgain_ref, kgain_ref,
                    cos_ref, sin_ref, rc_ref, rs1_ref, rs2_ref,
                    qa_ref, va_ref, qm_ref, vm_ref, om_ref, ka_ref, km_ref, sg_ref, gcol_ref, grow_ref):
    x = x_ref[...]
    inv = lax.rsqrt(jnp.mean(x * x, axis=-1, keepdims=True) + EPS)
    h = ((x * inv) * gain_ref[...]).astype(BF16)
    tm = x.shape[0]

    sg_ref[...] = _sigmoid(_dot(h, w_tm_ref[:, TM_OFFS[0]:TM_OFFS[1]])).astype(BF16)
    om_ref[...] = _sigmoid(_dot_nt(w_fm_ref[FM_OFFS[0]:FM_OFFS[1], :], h)).astype(BF16)

    zq = _dot_nt(w_fm_ref[FM_OFFS[1]:FM_OFFS[2], :], h)
    cos, sin = cos_ref[...], sin_ref[...]
    qgain = qgain_ref[...] * (ATT_HEAD_DIM ** -0.5 * LOG2_E)
    for hd in range(ATT_HEADS):
        z = zq[hd * ATT_HEAD_DIM:(hd + 1) * ATT_HEAD_DIM, :]
        inv_h = lax.rsqrt(jnp.mean(z * z, axis=0, keepdims=True) + EPS)
        y = (z * inv_h) * qgain
        x1, x2 = y[:ROPE_HALF], y[ROPE_HALF:ROPE_DIM]
        y = jnp.concatenate([x1 * cos - x2 * sin, x2 * cos + x1 * sin, y[ROPE_DIM:]], axis=0)
        qa_ref[hd * ATT_HEAD_DIM:(hd + 1) * ATT_HEAD_DIM, :] = y.astype(BF16)

    zn = _dot(h, w_tm_ref[:, TM_OFFS[1]:TM_OFFS[4]])
    zr = _dot_nt(w_fm_ref[FM_OFFS[2]:FM_OFFS[5], :], h)
    va_ref[...] = zr[:ATT_KV_W, :].astype(BF16)
    qm_ref[...] = (zr[ATT_KV_W:ATT_KV_W + M_QK_W, :] * (M_QK_DIM ** -0.5)).astype(BF16)
    vm_ref[...] = zr[ATT_KV_W + M_QK_W:, :].astype(BF16)

    lane = lax.broadcasted_iota(jnp.int32, (tm, LANES), 1)
    low = lane < ATT_HEAD_DIM
    zk = zn[:, :ATT_KV_W]
    s = zk * zk
    r0 = jnp.sum(jnp.where(low, s, 0.0), axis=-1, keepdims=True)
    r1 = jnp.sum(jnp.where(low, 0.0, s), axis=-1, keepdims=True)
    inv_k = lax.rsqrt(jnp.where(low, r0, r1) * (1.0 / ATT_HEAD_DIM) + EPS)
    y = (zk * inv_k) * kgain_ref[...]
    up = pltpu.roll(y, LANES - ROPE_HALF, 1)
    dn = pltpu.roll(y, ROPE_HALF, 1)
    ka_ref[...] = (y * rc_ref[...] + up * rs1_ref[...] + dn * rs2_ref[...]).astype(BF16)
    km_ref[...] = zn[:, ATT_KV_W:ATT_KV_W + M_QK_W].astype(BF16)
    zi = zn[:, ATT_KV_W + M_QK_W:] + bias_ref[...]
    g = jnp.where(lane < M_HEADS, zi, jnp.where(lane < 2 * M_HEADS, _log_sigmoid(zi), 0.0)) * LOG2_E
    gcol_ref[...] = g
    grow_ref[...] = g.T[:2 * M_HEADS, :]


def _in_proj(x, gain, w_fm, w_tm, bias, qgain, kgain, cos_t, sin_t, rc, rs1, rs2, *, layer, seq, tm):
    T, D = x.shape
    n_pos = seq // tm
    row = lambda w: pl.BlockSpec((tm, w), lambda i: (i, 0))
    colm = lambda w: pl.BlockSpec((w, tm), lambda i: (0, i))
    pos = pl.BlockSpec((tm, LANES), lambda i: (i % n_pos, 0))
    pos_t = pl.BlockSpec((ROPE_HALF, tm), lambda i: (0, i % n_pos))
    fm = lambda w: jax.ShapeDtypeStruct((w, T), BF16)
    out_shapes = (fm(ATT_Q_W), fm(ATT_KV_W), fm(M_QK_W), fm(M_V_W), fm(M_V_W),
                  jax.ShapeDtypeStruct((T, ATT_KV_W), BF16),
                  jax.ShapeDtypeStruct((T, M_QK_W), BF16),
                  jax.ShapeDtypeStruct((T, 2 * D), BF16),
                  jax.ShapeDtypeStruct((T, LANES), F32),
                  jax.ShapeDtypeStruct((2 * M_HEADS, T), F32))
    out_specs = (colm(ATT_Q_W), colm(ATT_KV_W), colm(M_QK_W), colm(M_V_W), colm(M_V_W),
                 row(ATT_KV_W), row(M_QK_W), row(2 * D), row(LANES), colm(2 * M_HEADS))
    in_specs = [row(D)] + [_layer_spec(a, layer, single_buffer=True)
                           for a in (gain, w_fm, w_tm, bias, qgain, kgain)] + [
        pos_t, pos_t, pos, pos, pos]
    return pl.pallas_call(
        _in_proj_kernel, grid=(T // tm,), in_specs=in_specs, out_specs=out_specs, out_shape=out_shapes,
        compiler_params=_params("parallel"), name="in_proj",
    )(x, gain, w_fm, w_tm, bias, qgain, kgain, cos_t, sin_t, rc, rs1, rs2)


def _attention_units(q_ref, k_ref, kp_ref, v_ref, vp_ref):
    blk, hd_dim = ATT_BLOCK, ATT_HEAD_DIM
    units = []
    for b in range(q_ref.shape[1] // blk):
        cur = slice(b * blk, (b + 1) * blk)
        old = slice((b - 1) * blk, b * blk)
        k_cat = jnp.concatenate([kp_ref[...] if b == 0 else k_ref[old, :], k_ref[cur, :]], axis=0)
        v_prev = vp_ref[...] if b == 0 else v_ref[:, old]
        v_cur = v_ref[:, cur]
        for kv in range(ATT_KV_HEADS):
            rows = slice(kv * hd_dim, (kv + 1) * hd_dim)
            v_cat = jnp.concatenate([v_prev[rows, :], v_cur[rows, :]], axis=1)
            units.append((b, cur, kv, k_cat, v_cat))
    return units


def _attention_scores(unit, q_ref):
    b, cur, kv, k_cat, _ = unit
    hd_dim = ATT_HEAD_DIM
    zeros = jnp.zeros((hd_dim, ATT_BLOCK), BF16)
    q_pad = []
    for g in range(ATT_GROUP):
        hd = kv * ATT_GROUP + g
        qh = q_ref[hd * hd_dim:(hd + 1) * hd_dim, cur]
        q_pad.append(jnp.concatenate([qh, zeros] if kv == 0 else [zeros, qh], axis=0))
    return _dot(k_cat, jnp.concatenate(q_pad, axis=1))


def _attention_softmax(unit, s_all, first, sinks_ref, layer):
    b, cur, kv, _, _ = unit
    blk = ATT_BLOCK
    key = lax.broadcasted_iota(jnp.int32, (blk, blk), 0)
    qry = lax.broadcasted_iota(jnp.int32, (blk, blk), 1)
    from_prev = key > qry
    zero_p = jnp.zeros((blk, blk), BF16)
    p_all, r_den = [], []
    for g in range(ATT_GROUP):
        sink = sinks_ref[layer * ATT_HEADS + kv * ATT_GROUP + g] * LOG2_E
        s_prev = s_all[:blk, g * blk:(g + 1) * blk]
        if b == 0:
            s_prev = s_prev + jnp.where(first, NEG_BIG, 0.0)
        s = jnp.where(from_prev, s_prev, s_all[blk:, g * blk:(g + 1) * blk])
        m = jnp.maximum(jnp.max(s, axis=0, keepdims=True), sink)
        p = jnp.exp2(s - m)
        r_den.append(1.0 / (jnp.sum(p, axis=0, keepdims=True) + jnp.exp2(sink - m)))
        pb = p.astype(BF16)
        p_all.append(jnp.concatenate([jnp.where(from_prev, pb, zero_p), jnp.where(from_prev, zero_p, pb)], axis=0))
    return jnp.concatenate(p_all, axis=1), r_den


def _attention_output(unit, probs, r_den, o_ref):
    b, cur, kv, _, v_cat = unit
    blk, hd_dim = ATT_BLOCK, ATT_HEAD_DIM
    o_all = _dot(v_cat, probs)
    for g in range(ATT_GROUP):
        hd = kv * ATT_GROUP + g
        o_ref[hd * hd_dim:(hd + 1) * hd_dim, cur] = (o_all[:, g * blk:(g + 1) * blk] * r_den[g]).astype(BF16)


def _bf16_part(x):
    bits = pltpu.bitcast(x, jnp.uint32) & jnp.uint32(0xFFFF0000)
    return pltpu.bitcast(bits, F32)


def _cumsum_rows(tril, x):
    lane = lax.broadcasted_iota(jnp.int32, x.shape, 1)
    w = 2 * M_HEADS
    h1 = _bf16_part(x)
    r1 = x - h1
    h2 = _bf16_part(r1)
    h3 = r1 - h2
    packed = jnp.where(lane < w, h1, jnp.where(lane < 2 * w, pltpu.roll(h2, w, 1), pltpu.roll(h3, 2 * w, 1)))
    c = _dot(tril, packed.astype(BF16))
    return c + pltpu.roll(c, LANES - w, 1) + pltpu.roll(c, LANES - 2 * w, 1)


def _split3(x):
    h1 = _bf16_part(x)
    r1 = x - h1
    h2 = _bf16_part(r1)
    return h1.astype(BF16), h2.astype(BF16), (r1 - h2).astype(BF16)


def _mlstm_gates_and_scores(q_ref, k_ref, gcol_ref, grow_ref):
    L = M_CHUNK
    src = lax.broadcasted_iota(jnp.int32, (L, L), 0)
    dst = lax.broadcasted_iota(jnp.int32, (L, L), 1)
    tril = jnp.where(dst <= src, 1.0, 0.0).astype(BF16)
    triu = jnp.where(src <= dst, 1.0, 0.0).astype(BF16)
    zeros = jnp.zeros((M_QK_DIM, L), BF16)
    chunks = []
    for c in range(q_ref.shape[1] // L):
        cur = slice(c * L, (c + 1) * L)
        gcol = gcol_ref[cur, :]
        grow = grow_ref[:, cur]
        bc = _cumsum_rows(tril, gcol)
        br = sum(_dot(t, triu) for t in _split3(grow))
        pairs = []
        for j in range(M_HEADS // 2):
            k2 = k_ref[cur, j * LANES:(j + 1) * LANES]
            q2 = q_ref[j * LANES:(j + 1) * LANES, cur]
            q_pad = jnp.concatenate([jnp.concatenate([q2[:M_QK_DIM], zeros], axis=0),
                                     jnp.concatenate([zeros, q2[M_QK_DIM:]], axis=0)], axis=1)
            pairs.append(dict(k2=k2, q_pad=q_pad, s_t=_dot(k2, q_pad), heads=[{}, {}]))
        chunks.append(dict(cur=cur, gcol=gcol, grow=grow, bc=bc, br=br, pairs=pairs))
    return chunks


def _mlstm_weights(chunks, v_ref, m_ref, n_state_rows):
    L = M_CHUNK
    src = lax.broadcasted_iota(jnp.int32, (L, L), 0)
    dst = lax.broadcasted_iota(jnp.int32, (L, L), 1)
    causal = src <= dst
    zeros = jnp.zeros((L, L), BF16)
    m_run = [m_ref[h:h + 1, :] for h in range(M_HEADS)]
    for ch in chunks:
        cur, gcol, grow, bc, br = ch["cur"], ch["gcol"], ch["grow"], ch["bc"], ch["br"]
        r_col = gcol - pltpu.roll(bc, LANES - M_HEADS, 1)
        for j, pr in enumerate(ch["pairs"]):
            for hh, hd in enumerate(pr["heads"]):
                h = 2 * j + hh
                vh = v_ref[h * M_V_DIM:(h + 1) * M_V_DIM, cur]
                b_row = br[M_HEADS + h:M_HEADS + h + 1, :]
                li_row = grow[h:h + 1, :]
                m_prev = m_run[h]
                log_d = jnp.where(causal, jnp.broadcast_to(r_col[:, h:h + 1], (L, L)) + b_row, NEG_BIG)
                m_t = jnp.maximum(b_row + m_prev, jnp.max(log_d, axis=0, keepdims=True))
                w_t = jnp.exp2(log_d - m_t) * pr["s_t"][:, hh * L:(hh + 1) * L]
                g = b_row[:, L - 1:L]
                w_end = g - b_row + li_row
                m_loc = jnp.max(w_end, axis=1, keepdims=True)
                e_row = jnp.exp2(w_end - m_loc).astype(BF16)
                m_new = jnp.maximum(g + m_prev, m_loc)
                hd.update(
                    vh=vh, m_t=m_t, w_bf=w_t.astype(BF16), w_sum=jnp.sum(w_t, axis=0, keepdims=True),
                    a_int=jnp.exp2(b_row + m_prev - m_t),
                    lhs=jnp.concatenate([vh * e_row, jnp.broadcast_to(e_row, (n_state_rows - M_V_DIM, L))], axis=0),
                    decay=jnp.exp2(g + m_prev - m_new), gain=jnp.exp2(m_loc - m_new))
                m_run[h] = m_new
            ha, hb = pr["heads"]
            pr["v_cat"] = jnp.concatenate([ha["vh"], hb["vh"]], axis=1)
            pr["w_diag"] = jnp.concatenate([jnp.concatenate([ha["w_bf"], zeros], axis=1),
                                            jnp.concatenate([zeros, hb["w_bf"]], axis=1)], axis=0)
    for h in range(M_HEADS):
        m_ref[h:h + 1, :] = m_run[h]


def _mlstm_intra(chunks):
    for ch in chunks:
        for pr in ch["pairs"]:
            pr["num"] = _dot(pr["v_cat"], pr["w_diag"])
            for hd in pr["heads"]:
                hd["d_state"] = _dot(hd["lhs"], pr["k2"])


def _mlstm_states(chunks, state_ref):
    first_head = lax.broadcasted_iota(jnp.int32, state_ref.shape[1:], 1) < M_QK_DIM
    for j in range(M_HEADS // 2):
        state = state_ref[j]
        for ch in chunks:
            pr = ch["pairs"][j]
            ha, hb = pr["heads"]
            pr["state_in"] = state.astype(BF16)
            state = (jnp.where(first_head, ha["decay"], hb["decay"]) * state
                     + jnp.where(first_head, ha["gain"], hb["gain"]) * jnp.where(first_head, ha["d_state"], hb["d_state"]))
        state_ref[j] = state
    for ch in chunks:
        for pr in ch["pairs"]:
            pr["inter"] = _dot(pr["state_in"], pr["q_pad"])


def _mlstm_outputs(chunks, osig_ref, gain_ref, o_ref):
    L = M_CHUNK
    for ch in chunks:
        cur = ch["cur"]
        for j, pr in enumerate(ch["pairs"]):
            for hh, hd in enumerate(pr["heads"]):
                h = 2 * j + hh
                a_int = hd["a_int"]
                inter = pr["inter"][:, hh * L:(hh + 1) * L]
                num = pr["num"][:, hh * L:(hh + 1) * L] + a_int * inter[:M_V_DIM]
                den = hd["w_sum"] + a_int * inter[M_V_DIM:M_V_DIM + 1]
                d = jnp.maximum(jnp.abs(den), jnp.exp2(-hd["m_t"]))
                scale = lax.rsqrt(jnp.mean(num * num, axis=0, keepdims=True) + EPS * (d * d))
                rows = slice(h * M_V_DIM, (h + 1) * M_V_DIM)
                o_ref[rows, cur] = ((num * scale) * (gain_ref[rows, :] * osig_ref[rows, cur].astype(F32))).astype(BF16)


def _mixers_kernel(sinks_ref, qa_ref, ka_ref, kap_ref, va_ref, vap_ref,
                   qm_ref, km_ref, vm_ref, osig_ref, gcol_ref, grow_ref, gain_ref,
                   att_ref, hm_ref, state_ref, m_ref, *, layer):
    first = pl.program_id(1) == 0

    @pl.when(first)
    def _():
        state_ref[...] = jnp.zeros(state_ref.shape, F32)
        m_ref[...] = jnp.zeros(m_ref.shape, F32)

    units = _attention_units(qa_ref, ka_ref, kap_ref, va_ref, vap_ref)
    scores = [_attention_scores(u, qa_ref) for u in units]
    chunks = _mlstm_gates_and_scores(qm_ref, km_ref, gcol_ref, grow_ref)
    probs = [_attention_softmax(u, s, first, sinks_ref, layer) for u, s in zip(units, scores)]
    _mlstm_weights(chunks, vm_ref, m_ref, state_ref.shape[1])
    for u, (p, r_den) in zip(units, probs):
        _attention_output(u, p, r_den, att_ref)
    _mlstm_intra(chunks)
    _mlstm_states(chunks, state_ref)
    _mlstm_outputs(chunks, osig_ref, gain_ref, hm_ref)


def _mixers(sinks, qa_t, ka, va_t, qm_t, km, vm_t, osig_t, gcol, grow, gain_b, *, layer, seq, tc):
    T = km.shape[0]
    n = seq // tc
    per = tc // ATT_BLOCK
    tile = lambda b, c: b * n + c
    prev_blk = lambda b, c: jnp.maximum(tile(b, c) * per - 1, 0)
    row = lambda w: pl.BlockSpec((tc, w), lambda b, c, *_: (tile(b, c), 0))
    colm = lambda w: pl.BlockSpec((w, tc), lambda b, c, *_: (0, tile(b, c)))
    grid_spec = pltpu.PrefetchScalarGridSpec(
        num_scalar_prefetch=1, grid=(T // seq, n),
        in_specs=[colm(ATT_Q_W), row(ATT_KV_W),
                  pl.BlockSpec((ATT_BLOCK, ATT_KV_W), lambda b, c, *_: (prev_blk(b, c), 0)),
                  colm(ATT_KV_W),
                  pl.BlockSpec((ATT_KV_W, ATT_BLOCK), lambda b, c, *_: (0, prev_blk(b, c))),
                  colm(M_QK_W), row(M_QK_W), colm(M_V_W), colm(M_V_W), row(LANES), colm(2 * M_HEADS),
                  _layer_spec(gain_b, layer)],
        out_specs=(colm(ATT_Q_W), colm(M_V_W)),
        scratch_shapes=[pltpu.VMEM((M_HEADS // 2, M_V_DIM + SUBLANES_BF16, LANES), F32),
                        pltpu.VMEM((2 * M_HEADS, LANES), F32)])
    return pl.pallas_call(
        functools.partial(_mixers_kernel, layer=layer), grid_spec=grid_spec,
        out_shape=(jax.ShapeDtypeStruct((ATT_Q_W, T), BF16), jax.ShapeDtypeStruct((M_V_W, T), BF16)),
        compiler_params=_params("parallel", "arbitrary"), name="mixers",
    )(sinks, qa_t, ka, ka, va_t, va_t, qm_t, km, vm_t, osig_t, gcol, grow, gain_b)


def _post_kernel(x_ref, att_ref, hm_ref, sg_ref, wa_ref, wm_ref, wo_ref, gain_ref, w1_ref, w2_ref, o_ref, *, n_chunks):
    D = x_ref.shape[1]
    a = _dot_tn(att_ref[...], wa_ref[...])
    m = _dot_tn(hm_ref[...], wm_ref[...])
    mixed = sg_ref[:, :D].astype(F32) * a + sg_ref[:, D:].astype(F32) * m
    x = x_ref[...] + _dot(mixed.astype(BF16), wo_ref[...])
    inv = lax.rsqrt(jnp.mean(x * x, axis=-1, keepdims=True) + EPS)
    h = ((x * inv) * gain_ref[...]).astype(BF16)
    fc = w1_ref.shape[1] // n_chunks
    acc = x
    for c in range(n_chunks):
        u = jnp.maximum(_dot(h, w1_ref[:, c * fc:(c + 1) * fc]), 0.0)
        acc = acc + _dot((u * u).astype(BF16), w2_ref[c * fc:(c + 1) * fc, :])
    o_ref[...] = acc


def _post(x, att_t, hm_t, sg, wa, wm, wo, gain, w1, w2, *, layer, tm):
    T, D = x.shape
    row = lambda w: pl.BlockSpec((tm, w), lambda i: (i, 0))
    colm = lambda w: pl.BlockSpec((w, tm), lambda i: (0, i))
    return pl.pallas_call(
        functools.partial(_post_kernel, n_chunks=4), grid=(T // tm,),
        in_specs=[row(D), colm(ATT_Q_W), colm(M_V_W), row(2 * D)] + [
            _layer_spec(a, layer, single_buffer=True) for a in (wa, wm, wo, gain, w1, w2)],
        out_specs=row(D), out_shape=jax.ShapeDtypeStruct((T, D), F32),
        compiler_params=_params("parallel"), name="post",
    )(x, att_t, hm_t, sg, wa, wm, wo, gain, w1, w2)


def _rope_tables(seq):
    pos = jnp.arange(seq, dtype=F32)
    inv_freq = ROPE_THETA ** (-jnp.arange(0, ROPE_DIM, 2, dtype=F32) / ROPE_DIM)
    ang = pos[:, None] * inv_freq[None, :]
    cos, sin = jnp.cos(ang), jnp.sin(ang)
    pad = jnp.zeros((seq, ATT_HEAD_DIM - ROPE_DIM), F32)
    zeros = jnp.zeros((seq, ROPE_HALF), F32)
    rc = jnp.concatenate([cos, cos, pad + 1.0], axis=1)
    rs1 = jnp.concatenate([-sin, zeros, pad], axis=1)
    rs2 = jnp.concatenate([zeros, sin, pad], axis=1)
    tile = lambda t: jnp.concatenate([t, t], axis=1)
    return cos.T, sin.T, tile(rc), tile(rs1), tile(rs2)


def _prep_w_in_kernel(w_ref, fm_ref, tm_ref):
    w = w_ref[...]
    seg = lambda i: w[:, IN_OFFS[i]:IN_OFFS[i + 1]]
    q_a, k_a, v_a, q_m, k_m, v_m, o_m, _, gates = (seg(i) for i in range(len(IN_WIDTHS)))
    fm_ref[...] = jnp.concatenate([o_m, q_a, v_a, q_m, v_m], axis=1).T.astype(BF16)
    lane = lax.broadcasted_iota(jnp.int32, (w.shape[0], LANES), 1)
    if_pad = jnp.where(lane < 2 * M_HEADS, w[:, IN_OFFS[7]:IN_OFFS[7] + LANES], 0.0)
    tm_ref[...] = jnp.concatenate([gates, k_a, k_m, if_pad], axis=1).astype(BF16)


def _prep_w_in(w_in):
    depth, D, d_in = w_in.shape
    return pl.pallas_call(
        _prep_w_in_kernel, grid=(depth, D // PREP_ROWS),
        in_specs=[pl.BlockSpec((None, PREP_ROWS, d_in), lambda l, r: (l, r, 0))],
        out_specs=(pl.BlockSpec((None, FM_OFFS[-1], PREP_ROWS), lambda l, r: (l, 0, r)),
                   pl.BlockSpec((None, PREP_ROWS, TM_OFFS[-1]), lambda l, r: (l, r, 0))),
        out_shape=(jax.ShapeDtypeStruct((depth, FM_OFFS[-1], D), BF16),
                   jax.ShapeDtypeStruct((depth, D, TM_OFFS[-1]), BF16)),
        compiler_params=_params("parallel", "parallel"), name="prep_w_in",
    )(w_in)


def kernel(x, norm_mix, w_in, att_q_norm, att_k_norm, att_sinks, m_gate_bias, m_head_norm,
           w_att_branch, w_m_branch, w_out, norm_ffn, w_ff1, w_ff2):
    B, S, D = x.shape
    T = B * S
    depth = w_in.shape[0]
    tm = min(TOKEN_TILE, S)
    cos_t, sin_t, rc, rs1, rs2 = _rope_tables(S)
    w_fm, w_tm = _prep_w_in(w_in)
    wa, wm, wo = w_att_branch.astype(BF16), w_m_branch.astype(BF16), w_out.astype(BF16)
    w1, w2 = w_ff1.astype(BF16), w_ff2.astype(BF16)
    bias = jnp.pad(m_gate_bias, ((0, 0), (0, LANES - 2 * M_HEADS))).reshape(depth, 1, LANES)
    tm_in = min(IN_PROJ_TILE, S)
    qgain = jnp.broadcast_to(att_q_norm[:, :, None], (depth, ATT_HEAD_DIM, tm_in))
    kgain = jnp.tile(att_k_norm, (1, 2)).reshape(depth, 1, LANES)
    mgain = jnp.broadcast_to(m_head_norm[:, :, None], (depth, M_V_W, LANES))
    gain_mix = norm_mix.reshape(depth, 1, D)
    gain_ffn = norm_ffn.reshape(depth, 1, D)
    sinks = att_sinks.reshape(depth * ATT_HEADS)
    xt = x.reshape(T, D)
    for l in range(depth):
        qa_t, va_t, qm_t, vm_t, osig_t, ka, km, sg, gcol, grow = _in_proj(
            xt, gain_mix, w_fm, w_tm, bias, qgain, kgain, cos_t, sin_t, rc, rs1, rs2, layer=l, seq=S, tm=tm_in)
        att_t, hm_t = _mixers(sinks, qa_t, ka, va_t, qm_t, km, vm_t, osig_t, gcol, grow, mgain, layer=l, seq=S, tc=tm)
        xt = _post(xt, att_t, hm_t, sg, wa, wm, wo, gain_ffn, w1, w2, layer=l, tm=tm)
    return xt.reshape(B, S, D)
```

```python
import functools

import jax
import jax.numpy as jnp
import numpy as np
from jax import lax
from jax.experimental import pallas as pl
from jax.experimental.pallas import tpu as pltpu

D_MODEL = 1024
ATT_HEADS = 8
ATT_KV_HEADS = 2
ATT_GROUP = ATT_HEADS // ATT_KV_HEADS
ATT_HEAD_DIM = 64
ATT_BLOCK = 128
ROPE_DIM = ATT_HEAD_DIM // 4
ROPE_HALF = ROPE_DIM // 2
ROPE_THETA = 500000.0
M_HEADS = 4
M_QK_DIM = 64
M_V_DIM = 128
M_CHUNK = 128
D_FF = 4 * D_MODEL
EPS = 1e-6

ATT_Q_W = ATT_HEADS * ATT_HEAD_DIM
ATT_KV_W = ATT_KV_HEADS * ATT_HEAD_DIM
M_QK_W = M_HEADS * M_QK_DIM
M_V_W = M_HEADS * M_V_DIM
IN_WIDTHS = (ATT_Q_W, ATT_KV_W, ATT_KV_W, M_QK_W, M_QK_W, M_V_W, M_V_W, 2 * M_HEADS, 2 * D_MODEL)
IN_OFFS = tuple(int(o) for o in np.concatenate([[0], np.cumsum(IN_WIDTHS)]))
LANES = 128
SUBLANES_BF16 = 16
FM_WIDTHS = (M_V_W, ATT_Q_W, ATT_KV_W, M_QK_W, M_V_W)
FM_OFFS = tuple(int(o) for o in np.concatenate([[0], np.cumsum(FM_WIDTHS)]))
TM_WIDTHS = (2 * D_MODEL, ATT_KV_W, M_QK_W, LANES)
TM_OFFS = tuple(int(o) for o in np.concatenate([[0], np.cumsum(TM_WIDTHS)]))

NEG_BIG = -1e30
LOG2_E = 1.4426950408889634
VMEM_LIMIT = 56 * 1024 * 1024
TOKEN_TILE = 512
IN_PROJ_TILE = 1024
PREP_COLS = 256

BF16 = jnp.bfloat16
F32 = jnp.float32


def _dot(a, b):
    return jnp.dot(a, b, preferred_element_type=F32)


def _dot_nt(a, b):
    return lax.dot_general(a, b, (((1,), (1,)), ((), ())), preferred_element_type=F32)


def _dot_tn(a, b):
    return lax.dot_general(a, b, (((0,), (0,)), ((), ())), preferred_element_type=F32)


def _sigmoid(x):
    return 1.0 / (1.0 + jnp.exp(-x))


def _log_sigmoid(x):
    return jnp.minimum(x, 0.0) - jnp.log(1.0 + jnp.exp(-jnp.abs(x)))


def _const_spec(shape):
    return pl.BlockSpec(shape, lambda *_: (0,) * len(shape))


def _layer_spec(arr, layer, single_buffer=False):
    mode = dict(pipeline_mode=pl.Buffered(1)) if single_buffer else {}
    return pl.BlockSpec((None,) + arr.shape[1:], lambda *_: (layer,) + (0,) * (arr.ndim - 1), **mode)


def _params(*sem):
    return pltpu.CompilerParams(dimension_semantics=sem, vmem_limit_bytes=VMEM_LIMIT)


def _in_proj_kernel(x_ref, gain_ref, w_fm_ref, w_tm_ref, bias_ref, qgain_ref, kgain_ref,
                    cos_ref, sin_ref, rc_ref, rs1_ref, rs2_ref,
                    qa_ref, va_ref, qm_ref, vm_ref, om_ref, ka_ref, km_ref, sg_ref, gcol_ref, grow_ref):
    x = x_ref[...]
    inv = lax.rsqrt(jnp.mean(x * x, axis=-1, keepdims=True) + EPS)
    h = ((x * inv) * gain_ref[...]).astype(BF16)
    tm = x.shape[0]

    sg_ref[...] = _sigmoid(_dot(h, w_tm_ref[:, TM_OFFS[0]:TM_OFFS[1]])).astype(BF16)
    om_ref[...] = _sigmoid(_dot_nt(w_fm_ref[FM_OFFS[0]:FM_OFFS[1], :], h)).astype(BF16)

    zq = _dot_nt(w_fm_ref[FM_OFFS[1]:FM_OFFS[2], :], h)
    cos, sin = cos_ref[...], sin_ref[...]
    qgain = qgain_ref[...] * (ATT_HEAD_DIM ** -0.5 * LOG2_E)
    for hd in range(ATT_HEADS):
        z = zq[hd * ATT_HEAD_DIM:(hd + 1) * ATT_HEAD_DIM, :]
        inv_h = lax.rsqrt(jnp.mean(z * z, axis=0, keepdims=True) + EPS)
        y = (z * inv_h) * qgain
        x1, x2 = y[:ROPE_HALF], y[ROPE_HALF:ROPE_DIM]
        y = jnp.concatenate([x1 * cos - x2 * sin, x2 * cos + x1 * sin, y[ROPE_DIM:]], axis=0)
        qa_ref[hd * ATT_HEAD_DIM:(hd + 1) * ATT_HEAD_DIM, :] = y.astype(BF16)

    zn = _dot(h, w_tm_ref[:, TM_OFFS[1]:TM_OFFS[4]])
    zr = _dot_nt(w_fm_ref[FM_OFFS[2]:FM_OFFS[5], :], h)
    va_ref[...] = zr[:ATT_KV_W, :].astype(BF16)
    qm_ref[...] = (zr[ATT_KV_W:ATT_KV_W + M_QK_W, :] * (M_QK_DIM ** -0.5)).astype(BF16)
    vm_ref[...] = zr[ATT_KV_W + M_QK_W:, :].astype(BF16)

    lane = lax.broadcasted_iota(jnp.int32, (tm, LANES), 1)
    low = lane < ATT_HEAD_DIM
    zk = zn[:, :ATT_KV_W]
    s = zk * zk
    r0 = jnp.sum(jnp.where(low, s, 0.0), axis=-1, keepdims=True)
    r1 = jnp.sum(jnp.where(low, 0.0, s), axis=-1, keepdims=True)
    inv_k = lax.rsqrt(jnp.where(low, r0, r1) * (1.0 / ATT_HEAD_DIM) + EPS)
    y = (zk * inv_k) * kgain_ref[...]
    up = pltpu.roll(y, LANES - ROPE_HALF, 1)
    dn = pltpu.roll(y, ROPE_HALF, 1)
    ka_ref[...] = (y * rc_ref[...] + up * rs1_ref[...] + dn * rs2_ref[...]).astype(BF16)
    km_ref[...] = zn[:, ATT_KV_W:ATT_KV_W + M_QK_W].astype(BF16)
    zi = zn[:, ATT_KV_W + M_QK_W:] + bias_ref[...]
    g = jnp.where(lane < M_HEADS, zi, jnp.where(lane < 2 * M_HEADS, _log_sigmoid(zi), 0.0)) * LOG2_E
    gcol_ref[...] = g
    grow_ref[...] = g.T[:2 * M_HEADS, :]


def _in_proj(x, gain, w_fm, w_tm, bias, qgain, kgain, cos_t, sin_t, rc, rs1, rs2, *, layer, seq, tm):
    T, D = x.shape
    n_pos = seq // tm
    row = lambda w: pl.BlockSpec((tm, w), lambda i: (i, 0))
    colm = lambda w: pl.BlockSpec((w, tm), lambda i: (0, i))
    pos = pl.BlockSpec((tm, LANES), lambda i: (i % n_pos, 0))
    pos_t = pl.BlockSpec((ROPE_HALF, tm), lambda i: (0, i % n_pos))
    fm = lambda w: jax.ShapeDtypeStruct((w, T), BF16)
    out_shapes = (fm(ATT_Q_W), fm(ATT_KV_W), fm(M_QK_W), fm(M_V_W), fm(M_V_W),
                  jax.ShapeDtypeStruct((T, ATT_KV_W), BF16),
                  jax.ShapeDtypeStruct((T, M_QK_W), BF16),
                  jax.ShapeDtypeStruct((T, 2 * D), BF16),
                  jax.ShapeDtypeStruct((T, LANES), F32),
                  jax.ShapeDtypeStruct((2 * M_HEADS, T), F32))
    out_specs = (colm(ATT_Q_W), colm(ATT_KV_W), colm(M_QK_W), colm(M_V_W), colm(M_V_W),
                 row(ATT_KV_W), row(M_QK_W), row(2 * D), row(LANES), colm(2 * M_HEADS))
    in_specs = [row(D)] + [_layer_spec(a, layer, single_buffer=True)
                           for a in (gain, w_fm, w_tm, bias, qgain, kgain)] + [
        pos_t, pos_t, pos, pos, pos]
    return pl.pallas_call(
        _in_proj_kernel, grid=(T // tm,), in_specs=in_specs, out_specs=out_specs, out_shape=out_shapes,
        compiler_params=_params("parallel"), name="in_proj",
    )(x, gain, w_fm, w_tm, bias, qgain, kgain, cos_t, sin_t, rc, rs1, rs2)


def _attention_units(q_ref, k_ref, kp_ref, v_ref, vp_ref):
    blk, hd_dim = ATT_BLOCK, ATT_HEAD_DIM
    units = []
    for b in range(q_ref.shape[1] // blk):
        cur = slice(b * blk, (b + 1) * blk)
        old = slice((b - 1) * blk, b * blk)
        k_cat = jnp.concatenate([kp_ref[...] if b == 0 else k_ref[old, :], k_ref[cur, :]], axis=0)
        v_prev = vp_ref[...] if b == 0 else v_ref[:, old]
        v_cur = v_ref[:, cur]
        for kv in range(ATT_KV_HEADS):
            rows = slice(kv * hd_dim, (kv + 1) * hd_dim)
            v_cat = jnp.concatenate([v_prev[rows, :], v_cur[rows, :]], axis=1)
            units.append((b, cur, kv, k_cat, v_cat))
    return units


def _attention_scores(unit, q_ref):
    b, cur, kv, k_cat, _ = unit
    hd_dim = ATT_HEAD_DIM
    zeros = jnp.zeros((hd_dim, ATT_BLOCK), BF16)
    q_pad = []
    for g in range(ATT_GROUP):
        hd = kv * ATT_GROUP + g
        qh = q_ref[hd * hd_dim:(hd + 1) * hd_dim, cur]
        q_pad.append(jnp.concatenate([qh, zeros] if kv == 0 else [zeros, qh], axis=0))
    return _dot(k_cat, jnp.concatenate(q_pad, axis=1))


def _attention_softmax(unit, s_all, first, sinks_ref, layer):
    b, cur, kv, _, _ = unit
    blk = ATT_BLOCK
    key = lax.broadcasted_iota(jnp.int32, (blk, blk), 0)
    qry = lax.broadcasted_iota(jnp.int32, (blk, blk), 1)
    from_prev = key > qry
    zero_p = jnp.zeros((blk, blk), BF16)
    p_all, r_den = [], []
    for g in range(ATT_GROUP):
        sink = sinks_ref[layer * ATT_HEADS + kv * ATT_GROUP + g] * LOG2_E
        s_prev = s_all[:blk, g * blk:(g + 1) * blk]
        if b == 0:
            s_prev = s_prev + jnp.where(first, NEG_BIG, 0.0)
        s = jnp.where(from_prev, s_prev, s_all[blk:, g * blk:(g + 1) * blk])
        m = jnp.maximum(jnp.max(s, axis=0, keepdims=True), sink)
        p = jnp.exp2(s - m)
        r_den.append(1.0 / (jnp.sum(p, axis=0, keepdims=True) + jnp.exp2(sink - m)))
        pb = p.astype(BF16)
        p_all.append(jnp.concatenate([jnp.where(from_prev, pb, zero_p), jnp.where(from_prev, zero_p, pb)], axis=0))
    return jnp.concatenate(p_all, axis=1), r_den


def _attention_output(unit, probs, r_den, o_ref):
    b, cur, kv, _, v_cat = unit
    blk, hd_dim = ATT_BLOCK, ATT_HEAD_DIM
    o_all = _dot(v_cat, probs)
    for g in range(ATT_GROUP):
        hd = kv * ATT_GROUP + g
        o_ref[hd * hd_dim:(hd + 1) * hd_dim, cur] = (o_all[:, g * blk:(g + 1) * blk] * r_den[g]).astype(BF16)


def _bf16_part(x):
    bits = pltpu.bitcast(x, jnp.uint32) & jnp.uint32(0xFFFF0000)
    return pltpu.bitcast(bits, F32)


def _cumsum_rows(tril, x):
    lane = lax.broadcasted_iota(jnp.int32, x.shape, 1)
    w = 2 * M_HEADS
    h1 = _bf16_part(x)
    r1 = x - h1
    h2 = _bf16_part(r1)
    h3 = r1 - h2
    packed = jnp.where(lane < w, h1, jnp.where(lane < 2 * w, pltpu.roll(h2, w, 1), pltpu.roll(h3, 2 * w, 1)))
    c = _dot(tril, packed.astype(BF16))
    return c + pltpu.roll(c, LANES - w, 1) + pltpu.roll(c, LANES - 2 * w, 1)


def _split3(x):
    h1 = _bf16_part(x)
    r1 = x - h1
    h2 = _bf16_part(r1)
    return h1.astype(BF16), h2.astype(BF16), (r1 - h2).astype(BF16)


def _mlstm_gates_and_scores(q_ref, k_ref, gcol_ref, grow_ref):
    L = M_CHUNK
    src = lax.broadcasted_iota(jnp.int32, (L, L), 0)
    dst = lax.broadcasted_iota(jnp.int32, (L, L), 1)
    tril = jnp.where(dst <= src, 1.0, 0.0).astype(BF16)
    triu = jnp.where(src <= dst, 1.0, 0.0).astype(BF16)
    zeros = jnp.zeros((M_QK_DIM, L), BF16)
    chunks = []
    for c in range(q_ref.shape[1] // L):
        cur = slice(c * L, (c + 1) * L)
        gcol = gcol_ref[cur, :]
        grow = grow_ref[:, cur]
        bc = _cumsum_rows(tril, gcol)
        br = sum(_dot(t, triu) for t in _split3(grow))
        pairs = []
        for j in range(M_HEADS // 2):
            k2 = k_ref[cur, j * LANES:(j + 1) * LANES]
            q2 = q_ref[j * LANES:(j + 1) * LANES, cur]
            q_pad = jnp.concatenate([jnp.concatenate([q2[:M_QK_DIM], zeros], axis=0),
                                     jnp.concatenate([zeros, q2[M_QK_DIM:]], axis=0)], axis=1)
            pairs.append(dict(k2=k2, q_pad=q_pad, s_t=_dot(k2, q_pad), heads=[{}, {}]))
        chunks.append(dict(cur=cur, gcol=gcol, grow=grow, bc=bc, br=br, pairs=pairs))
    return chunks


def _mlstm_weights(chunks, v_ref, m_ref, n_state_rows):
    L = M_CHUNK
    src = lax.broadcasted_iota(jnp.int32, (L, L), 0)
    dst = lax.broadcasted_iota(jnp.int32, (L, L), 1)
    causal = src <= dst
    zeros = jnp.zeros((L, L), BF16)
    m_run = [m_ref[h:h + 1, :] for h in range(M_HEADS)]
    for ch in chunks:
        cur, gcol, grow, bc, br = ch["cur"], ch["gcol"], ch["grow"], ch["bc"], ch["br"]
        r_col = gcol - pltpu.roll(bc, LANES - M_HEADS, 1)
        for j, pr in enumerate(ch["pairs"]):
            for hh, hd in enumerate(pr["heads"]):
                h = 2 * j + hh
                vh = v_ref[h * M_V_DIM:(h + 1) * M_V_DIM, cur]
                b_row = br[M_HEADS + h:M_HEADS + h + 1, :]
                li_row = grow[h:h + 1, :]
                m_prev = m_run[h]
                log_d = jnp.where(causal, jnp.broadcast_to(r_col[:, h:h + 1], (L, L)) + b_row, NEG_BIG)
                m_t = jnp.maximum(b_row + m_prev, jnp.max(log_d, axis=0, keepdims=True))
                w_t = jnp.exp2(log_d - m_t) * pr["s_t"][:, hh * L:(hh + 1) * L]
                g = b_row[:, L - 1:L]
                w_end = g - b_row + li_row
                m_loc = jnp.max(w_end, axis=1, keepdims=True)
                e_row = jnp.exp2(w_end - m_loc).astype(BF16)
                m_new = jnp.maximum(g + m_prev, m_loc)
                hd.update(
                    vh=vh, m_t=m_t, w_bf=w_t.astype(BF16), w_sum=jnp.sum(w_t, axis=0, keepdims=True),
                    a_int=jnp.exp2(b_row + m_prev - m_t),
                    lhs=jnp.concatenate([vh * e_row, jnp.broadcast_to(e_row, (n_state_rows - M_V_DIM, L))], axis=0),
                    decay=jnp.exp2(g + m_prev - m_new), gain=jnp.exp2(m_loc - m_new))
                m_run[h] = m_new
            ha, hb = pr["heads"]
            pr["v_cat"] = jnp.concatenate([ha["vh"], hb["vh"]], axis=1)
            pr["w_diag"] = jnp.concatenate([jnp.concatenate([ha["w_bf"], zeros], axis=1),
                                            jnp.concatenate([zeros, hb["w_bf"]], axis=1)], axis=0)
    for h in range(M_HEADS):
        m_ref[h:h + 1, :] = m_run[h]


def _mlstm_intra(chunks):
    for ch in chunks:
        for pr in ch["pairs"]:
            pr["num"] = _dot(pr["v_cat"], pr["w_diag"])
            for hd in pr["heads"]:
                hd["d_state"] = _dot(hd["lhs"], pr["k2"])


def _mlstm_states(chunks, state_ref):
    first_head = lax.broadcasted_iota(jnp.int32, state_ref.shape[1:], 1) < M_QK_DIM
    for j in range(M_HEADS // 2):
        state = state_ref[j]
        for ch in chunks:
            pr = ch["pairs"][j]
            ha, hb = pr["heads"]
            pr["state_in"] = state.astype(BF16)
            state = (jnp.where(first_head, ha["decay"], hb["decay"]) * state
                     + jnp.where(first_head, ha["gain"], hb["gain"]) * jnp.where(first_head, ha["d_state"], hb["d_state"]))
        state_ref[j] = state
    for ch in chunks:
        for pr in ch["pairs"]:
            pr["inter"] = _dot(pr["state_in"], pr["q_pad"])


def _mlstm_outputs(chunks, osig_ref, gain_ref, o_ref):
    L = M_CHUNK
    for ch in chunks:
        cur = ch["cur"]
        for j, pr in enumerate(ch["pairs"]):
            for hh, hd in enumerate(pr["heads"]):
                h = 2 * j + hh
                a_int = hd["a_int"]
                inter = pr["inter"][:, hh * L:(hh + 1) * L]
                num = pr["num"][:, hh * L:(hh + 1) * L] + a_int * inter[:M_V_DIM]
                den = hd["w_sum"] + a_int * inter[M_V_DIM:M_V_DIM + 1]
                d = jnp.maximum(jnp.abs(den), jnp.exp2(-hd["m_t"]))
                scale = lax.rsqrt(jnp.mean(num * num, axis=0, keepdims=True) + EPS * (d * d))
                rows = slice(h * M_V_DIM, (h + 1) * M_V_DIM)
                o_ref[rows, cur] = ((num * scale) * (gain_ref[rows, :] * osig_ref[rows, cur].astype(F32))).astype(BF16)


def _mixers_kernel(sinks_ref, qa_ref, ka_ref, kap_ref, va_ref, vap_ref,
                   qm_ref, km_ref, vm_ref, osig_ref, gcol_ref, grow_ref, gain_ref,
                   att_ref, hm_ref, state_ref, m_ref, *, layer):
    first = pl.program_id(1) == 0

    @pl.when(first)
    def _():
        state_ref[...] = jnp.zeros(state_ref.shape, F32)
        m_ref[...] = jnp.zeros(m_ref.shape, F32)

    units = _attention_units(qa_ref, ka_ref, kap_ref, va_ref, vap_ref)
    scores = [_attention_scores(u, qa_ref) for u in units]
    chunks = _mlstm_gates_and_scores(qm_ref, km_ref, gcol_ref, grow_ref)
    probs = [_attention_softmax(u, s, first, sinks_ref, layer) for u, s in zip(units, scores)]
    _mlstm_weights(chunks, vm_ref, m_ref, state_ref.shape[1])
    for u, (p, r_den) in zip(units, probs):
        _attention_output(u, p, r_den, att_ref)
    _mlstm_intra(chunks)
    _mlstm_states(chunks, state_ref)
    _mlstm_outputs(chunks, osig_ref, gain_ref, hm_ref)


def _mixers(sinks, qa_t, ka, va_t, qm_t, km, vm_t, osig_t, gcol, grow, gain_b, *, layer, seq, tc):
    T = km.shape[0]
    n = seq // tc
    per = tc // ATT_BLOCK
    tile = lambda b, c: b * n + c
    prev_blk = lambda b, c: jnp.maximum(tile(b, c) * per - 1, 0)
    row = lambda w: pl.BlockSpec((tc, w), lambda b, c, *_: (tile(b, c), 0))
    colm = lambda w: pl.BlockSpec((w, tc), lambda b, c, *_: (0, tile(b, c)))
    grid_spec = pltpu.PrefetchScalarGridSpec(
        num_scalar_prefetch=1, grid=(T // seq, n),
        in_specs=[colm(ATT_Q_W), row(ATT_KV_W),
                  pl.BlockSpec((ATT_BLOCK, ATT_KV_W), lambda b, c, *_: (prev_blk(b, c), 0)),
                  colm(ATT_KV_W),
                  pl.BlockSpec((ATT_KV_W, ATT_BLOCK), lambda b, c, *_: (0, prev_blk(b, c))),
                  colm(M_QK_W), row(M_QK_W), colm(M_V_W), colm(M_V_W), row(LANES), colm(2 * M_HEADS),
                  _layer_spec(gain_b, layer)],
        out_specs=(colm(ATT_Q_W), colm(M_V_W)),
        scratch_shapes=[pltpu.VMEM((M_HEADS // 2, M_V_DIM + SUBLANES_BF16, LANES), F32),
                        pltpu.VMEM((2 * M_HEADS, LANES), F32)])
    return pl.pallas_call(
        functools.partial(_mixers_kernel, layer=layer), grid_spec=grid_spec,
        out_shape=(jax.ShapeDtypeStruct((ATT_Q_W, T), BF16), jax.ShapeDtypeStruct((M_V_W, T), BF16)),
        compiler_params=_params("parallel", "arbitrary"), name="mixers",
    )(sinks, qa_t, ka, ka, va_t, va_t, qm_t, km, vm_t, osig_t, gcol, grow, gain_b)


def _post_kernel(x_ref, att_ref, hm_ref, sg_ref, wa_ref, wm_ref, wo_ref, gain_ref, w1_ref, w2_ref, o_ref, *, n_chunks):
    D = x_ref.shape[1]
    a = _dot_tn(att_ref[...], wa_ref[...])
    m = _dot_tn(hm_ref[...], wm_ref[...])
    mixed = sg_ref[:, :D].astype(F32) * a + sg_ref[:, D:].astype(F32) * m
    x = x_ref[...] + _dot(mixed.astype(BF16), wo_ref[...])
    inv = lax.rsqrt(jnp.mean(x * x, axis=-1, keepdims=True) + EPS)
    h = ((x * inv) * gain_ref[...]).astype(BF16)
    fc = w1_ref.shape[1] // n_chunks
    acc = x
    for c in range(n_chunks):
        u = jnp.maximum(_dot(h, w1_ref[:, c * fc:(c + 1) * fc]), 0.0)
        acc = acc + _dot((u * u).astype(BF16), w2_ref[c * fc:(c + 1) * fc, :])
    o_ref[...] = acc


def _post(x, att_t, hm_t, sg, wa, wm, wo, gain, w1, w2, *, layer, tm):
    T, D = x.shape
    row = lambda w: pl.BlockSpec((tm, w), lambda i: (i, 0))
    colm = lambda w: pl.BlockSpec((w, tm), lambda i: (0, i))
    return pl.pallas_call(
        functools.partial(_post_kernel, n_chunks=4), grid=(T // tm,),
        in_specs=[row(D), colm(ATT_Q_W), colm(M_V_W), row(2 * D)] + [
            _layer_spec(a, layer, single_buffer=True) for a in (wa, wm, wo, gain, w1, w2)],
        out_specs=row(D), out_shape=jax.ShapeDtypeStruct((T, D), F32),
        compiler_params=_params("parallel"), name="post",
    )(x, att_t, hm_t, sg, wa, wm, wo, gain, w1, w2)


def _rope_tables(seq):
    pos = jnp.arange(seq, dtype=F32)
    inv_freq = ROPE_THETA ** (-jnp.arange(0, ROPE_DIM, 2, dtype=F32) / ROPE_DIM)
    ang_t = inv_freq[:, None] * pos[None, :]
    d = jnp.arange(LANES) % ATT_HEAD_DIM
    ang = pos[:, None] * inv_freq[d % ROPE_HALF][None, :]
    cos, sin = jnp.cos(ang), jnp.sin(ang)
    rc = jnp.where(d < ROPE_DIM, cos, 1.0)
    rs1 = jnp.where(d < ROPE_HALF, -sin, 0.0)
    rs2 = jnp.where((d >= ROPE_HALF) & (d < ROPE_DIM), sin, 0.0)
    return jnp.cos(ang_t), jnp.sin(ang_t), rc, rs1, rs2


def _prep_w_in_kernel(w_ref, fm_ref, tm_ref):
    w = w_ref[...]
    seg = lambda i: w[IN_OFFS[i]:IN_OFFS[i + 1], :]
    q_a, k_a, v_a, q_m, k_m, v_m, o_m, if_m, gates = (seg(i) for i in range(len(IN_WIDTHS)))
    fm_ref[...] = jnp.concatenate([o_m, q_a, v_a, q_m, v_m], axis=0).astype(BF16)
    if_pad = jnp.zeros((LANES - 2 * M_HEADS, w.shape[1]), F32)
    tm_ref[...] = jnp.concatenate([gates, k_a, k_m, if_m, if_pad], axis=0).T.astype(BF16)


def _prep_w_in(w_in):
    depth, D, d_in = w_in.shape
    return pl.pallas_call(
        _prep_w_in_kernel, grid=(depth, D // PREP_COLS),
        in_specs=[pl.BlockSpec((None, d_in, PREP_COLS), lambda l, c: (l, 0, c))],
        out_specs=(pl.BlockSpec((None, FM_OFFS[-1], PREP_COLS), lambda l, c: (l, 0, c)),
                   pl.BlockSpec((None, PREP_COLS, TM_OFFS[-1]), lambda l, c: (l, c, 0))),
        out_shape=(jax.ShapeDtypeStruct((depth, FM_OFFS[-1], D), BF16),
                   jax.ShapeDtypeStruct((depth, D, TM_OFFS[-1]), BF16)),
        compiler_params=_params("parallel", "parallel"), name="prep_w_in",
    )(jnp.swapaxes(w_in, 1, 2))


def kernel(x, norm_mix, w_in, att_q_norm, att_k_norm, att_sinks, m_gate_bias, m_head_norm,
           w_att_branch, w_m_branch, w_out, norm_ffn, w_ff1, w_ff2):
    B, S, D = x.shape
    T = B * S
    depth = w_in.shape[0]
    tm = min(TOKEN_TILE, S)
    cos_t, sin_t, rc, rs1, rs2 = _rope_tables(S)
    w_fm, w_tm = _prep_w_in(w_in)
    wa, wm, wo = w_att_branch.astype(BF16), w_m_branch.astype(BF16), w_out.astype(BF16)
    w1, w2 = w_ff1.astype(BF16), w_ff2.astype(BF16)
    bias = jnp.pad(m_gate_bias, ((0, 0), (0, LANES - 2 * M_HEADS))).reshape(depth, 1, LANES)
    tm_in = min(IN_PROJ_TILE, S)
    qgain = jnp.broadcast_to(att_q_norm[:, :, None], (depth, ATT_HEAD_DIM, tm_in))
    kgain = jnp.tile(att_k_norm, (1, 2)).reshape(depth, 1, LANES)
    mgain = jnp.broadcast_to(m_head_norm[:, :, None], (depth, M_V_W, LANES))
    gain_mix = norm_mix.reshape(depth, 1, D)
    gain_ffn = norm_ffn.reshape(depth, 1, D)
    sinks = att_sinks.reshape(depth * ATT_HEADS)
    xt = x.reshape(T, D)
    for l in range(depth):
        qa_t, va_t, qm_t, vm_t, osig_t, ka, km, sg, gcol, grow = _in_proj(
            xt, gain_mix, w_fm, w_tm, bias, qgain, kgain, cos_t, sin_t, rc, rs1, rs2, layer=l, seq=S, tm=tm_in)
        att_t, hm_t = _mixers(sinks, qa_t, ka, va_t, qm_t, km, vm_t, osig_t, gcol, grow, mgain, layer=l, seq=S, tc=tm)
        xt = _post(xt, att_t, hm_t, sg, wa, wm, wo, gain_ffn, w1, w2, layer=l, tm=tm)
    return xt.reshape(B, S, D)
```

```python
import functools

import jax
import jax.numpy as jnp
import numpy as np
from jax import lax
from jax.experimental import pallas as pl
from jax.experimental.pallas import tpu as pltpu

D_MODEL = 1024
ATT_HEADS = 8
ATT_KV_HEADS = 2
ATT_GROUP = ATT_HEADS // ATT_KV_HEADS
ATT_HEAD_DIM = 64
ATT_BLOCK = 128
ROPE_DIM = ATT_HEAD_DIM // 4
ROPE_HALF = ROPE_DIM // 2
ROPE_THETA = 500000.0
M_HEADS = 4
M_QK_DIM = 64
M_V_DIM = 128
M_CHUNK = 128
EPS = 1e-6

ATT_Q_W = ATT_HEADS * ATT_HEAD_DIM
ATT_KV_W = ATT_KV_HEADS * ATT_HEAD_DIM
M_QK_W = M_HEADS * M_QK_DIM
M_V_W = M_HEADS * M_V_DIM
IN_WIDTHS = (ATT_Q_W, ATT_KV_W, ATT_KV_W, M_QK_W, M_QK_W, M_V_W, M_V_W, 2 * M_HEADS, 2 * D_MODEL)
IN_OFFS = tuple(int(o) for o in np.concatenate([[0], np.cumsum(IN_WIDTHS)]))
LANES = 128
SUBLANES_BF16 = 16
FM_WIDTHS = (M_V_W, ATT_Q_W, ATT_KV_W, M_QK_W, M_V_W)
FM_OFFS = tuple(int(o) for o in np.concatenate([[0], np.cumsum(FM_WIDTHS)]))
TM_WIDTHS = (2 * D_MODEL, ATT_KV_W, M_QK_W, LANES)
TM_OFFS = tuple(int(o) for o in np.concatenate([[0], np.cumsum(TM_WIDTHS)]))

NEG_BIG = -1e30
LOG2_E = 1.4426950408889634
VMEM_LIMIT = 56 * 1024 * 1024
TOKEN_TILE = 512
IN_PROJ_TILE = 1024
PREP_COLS = 256

BF16 = jnp.bfloat16
F32 = jnp.float32


def _dot(a, b):
    return jnp.dot(a, b, preferred_element_type=F32)


def _dot_nt(a, b):
    return lax.dot_general(a, b, (((1,), (1,)), ((), ())), preferred_element_type=F32)


def _dot_tn(a, b):
    return lax.dot_general(a, b, (((0,), (0,)), ((), ())), preferred_element_type=F32)


def _sigmoid(x):
    return 0.5 * jnp.tanh(0.5 * x) + 0.5


def _log_sigmoid(x):
    return jnp.minimum(x, 0.0) - jnp.log(1.0 + jnp.exp(-jnp.abs(x)))


def _layer_spec(arr, layer, single_buffer=False):
    mode = dict(pipeline_mode=pl.Buffered(1)) if single_buffer else {}
    return pl.BlockSpec((None,) + arr.shape[1:], lambda *_: (layer,) + (0,) * (arr.ndim - 1), **mode)


def _params(*sem):
    return pltpu.CompilerParams(dimension_semantics=sem, vmem_limit_bytes=VMEM_LIMIT)


def _in_proj_kernel(x_ref, gain_ref, w_fm_ref, w_tm_ref, bias_ref, qgain_ref, kgain_ref,
                    cos_ref, sin_ref, rc_ref, rs1_ref, rs2_ref,
                    qa_ref, va_ref, qm_ref, vm_ref, om_ref, ka_ref, km_ref, sg_ref, gcol_ref, grow_ref):
    x = x_ref[...]
    inv = lax.rsqrt(jnp.mean(x * x, axis=-1, keepdims=True) + EPS)
    h = ((x * inv) * gain_ref[...]).astype(BF16)
    tm = x.shape[0]

    sg_ref[...] = _sigmoid(_dot(h, w_tm_ref[:, TM_OFFS[0]:TM_OFFS[1]])).astype(BF16)
    om_ref[...] = _sigmoid(_dot_nt(w_fm_ref[FM_OFFS[0]:FM_OFFS[1], :], h)).astype(BF16)

    zq = _dot_nt(w_fm_ref[FM_OFFS[1]:FM_OFFS[2], :], h)
    cos, sin = cos_ref[...], sin_ref[...]
    qgain = qgain_ref[...] * (ATT_HEAD_DIM ** -0.5 * LOG2_E)
    for hd in range(ATT_HEADS):
        z = zq[hd * ATT_HEAD_DIM:(hd + 1) * ATT_HEAD_DIM, :]
        inv_h = lax.rsqrt(jnp.mean(z * z, axis=0, keepdims=True) + EPS)
        y = (z * inv_h) * qgain
        x1, x2 = y[:ROPE_HALF], y[ROPE_HALF:ROPE_DIM]
        y = jnp.concatenate([x1 * cos - x2 * sin, x2 * cos + x1 * sin, y[ROPE_DIM:]], axis=0)
        qa_ref[hd * ATT_HEAD_DIM:(hd + 1) * ATT_HEAD_DIM, :] = y.astype(BF16)

    zn = _dot(h, w_tm_ref[:, TM_OFFS[1]:TM_OFFS[4]])
    zr = _dot_nt(w_fm_ref[FM_OFFS[2]:FM_OFFS[5], :], h)
    va_ref[...] = zr[:ATT_KV_W, :].astype(BF16)
    qm_ref[...] = (zr[ATT_KV_W:ATT_KV_W + M_QK_W, :] * (M_QK_DIM ** -0.5)).astype(BF16)
    vm_ref[...] = zr[ATT_KV_W + M_QK_W:, :].astype(BF16)

    lane = lax.broadcasted_iota(jnp.int32, (tm, LANES), 1)
    low = lane < ATT_HEAD_DIM
    zk = zn[:, :ATT_KV_W]
    s = zk * zk
    r0 = jnp.sum(jnp.where(low, s, 0.0), axis=-1, keepdims=True)
    r1 = jnp.sum(jnp.where(low, 0.0, s), axis=-1, keepdims=True)
    inv_k = lax.rsqrt(jnp.where(low, r0, r1) * (1.0 / ATT_HEAD_DIM) + EPS)
    y = (zk * inv_k) * kgain_ref[...]
    up = pltpu.roll(y, LANES - ROPE_HALF, 1)
    dn = pltpu.roll(y, ROPE_HALF, 1)
    ka_ref[...] = (y * rc_ref[...] + up * rs1_ref[...] + dn * rs2_ref[...]).astype(BF16)
    km_ref[...] = zn[:, ATT_KV_W:ATT_KV_W + M_QK_W].astype(BF16)
    zi = zn[:, ATT_KV_W + M_QK_W:] + bias_ref[...]
    g = jnp.where(lane < M_HEADS, zi, jnp.where(lane < 2 * M_HEADS, _log_sigmoid(zi), 0.0)) * LOG2_E
    gcol_ref[...] = g
    grow_ref[...] = g.T[:2 * M_HEADS, :]


def _in_proj(x, gain, w_fm, w_tm, bias, qgain, kgain, cos_t, sin_t, rc, rs1, rs2, *, layer, seq, tm):
    T, D = x.shape
    n_pos = seq // tm
    row = lambda w: pl.BlockSpec((tm, w), lambda i: (i, 0))
    colm = lambda w: pl.BlockSpec((w, tm), lambda i: (0, i))
    pos = pl.BlockSpec((tm, LANES), lambda i: (i % n_pos, 0))
    pos_t = pl.BlockSpec((ROPE_HALF, tm), lambda i: (0, i % n_pos))
    fm = lambda w: jax.ShapeDtypeStruct((w, T), BF16)
    out_shapes = (fm(ATT_Q_W), fm(ATT_KV_W), fm(M_QK_W), fm(M_V_W), fm(M_V_W),
                  jax.ShapeDtypeStruct((T, ATT_KV_W), BF16),
                  jax.ShapeDtypeStruct((T, M_QK_W), BF16),
                  jax.ShapeDtypeStruct((T, 2 * D), BF16),
                  jax.ShapeDtypeStruct((T, LANES), F32),
                  jax.ShapeDtypeStruct((2 * M_HEADS, T), F32))
    out_specs = (colm(ATT_Q_W), colm(ATT_KV_W), colm(M_QK_W), colm(M_V_W), colm(M_V_W),
                 row(ATT_KV_W), row(M_QK_W), row(2 * D), row(LANES), colm(2 * M_HEADS))
    in_specs = [row(D)] + [_layer_spec(a, layer, single_buffer=True)
                           for a in (gain, w_fm, w_tm, bias, qgain, kgain)] + [
        pos_t, pos_t, pos, pos, pos]
    return pl.pallas_call(
        _in_proj_kernel, grid=(T // tm,), in_specs=in_specs, out_specs=out_specs, out_shape=out_shapes,
        compiler_params=_params("parallel"), name="in_proj",
    )(x, gain, w_fm, w_tm, bias, qgain, kgain, cos_t, sin_t, rc, rs1, rs2)


def _attention_units(q_ref, k_ref, kp_ref, v_ref, vp_ref):
    blk, hd_dim = ATT_BLOCK, ATT_HEAD_DIM
    units = []
    for b in range(q_ref.shape[1] // blk):
        cur = slice(b * blk, (b + 1) * blk)
        old = slice((b - 1) * blk, b * blk)
        k_cat = jnp.concatenate([kp_ref[...] if b == 0 else k_ref[old, :], k_ref[cur, :]], axis=0)
        v_prev = vp_ref[...] if b == 0 else v_ref[:, old]
        v_cur = v_ref[:, cur]
        for kv in range(ATT_KV_HEADS):
            rows = slice(kv * hd_dim, (kv + 1) * hd_dim)
            v_cat = jnp.concatenate([v_prev[rows, :], v_cur[rows, :]], axis=1)
            units.append((b, cur, kv, k_cat, v_cat))
    return units


def _attention_scores(unit, q_ref):
    b, cur, kv, k_cat, _ = unit
    hd_dim = ATT_HEAD_DIM
    zeros = jnp.zeros((hd_dim, ATT_BLOCK), BF16)
    q_pad = []
    for g in range(ATT_GROUP):
        hd = kv * ATT_GROUP + g
        qh = q_ref[hd * hd_dim:(hd + 1) * hd_dim, cur]
        q_pad.append(jnp.concatenate([qh, zeros] if kv == 0 else [zeros, qh], axis=0))
    return _dot(k_cat, jnp.concatenate(q_pad, axis=1))


def _attention_softmax(unit, s_all, first, sinks_ref, layer):
    b, cur, kv, _, _ = unit
    blk = ATT_BLOCK
    key = lax.broadcasted_iota(jnp.int32, (blk, blk), 0)
    qry = lax.broadcasted_iota(jnp.int32, (blk, blk), 1)
    from_prev = key > qry
    zero_p = jnp.zeros((blk, blk), BF16)
    p_all, r_den = [], []
    for g in range(ATT_GROUP):
        sink = sinks_ref[layer * ATT_HEADS + kv * ATT_GROUP + g] * LOG2_E
        s_prev = s_all[:blk, g * blk:(g + 1) * blk]
        if b == 0:
            s_prev = s_prev + jnp.where(first, NEG_BIG, 0.0)
        s = jnp.where(from_prev, s_prev, s_all[blk:, g * blk:(g + 1) * blk])
        m = jnp.maximum(jnp.max(s, axis=0, keepdims=True), sink)
        p = jnp.exp2(s - m)
        r_den.append(1.0 / (jnp.sum(p, axis=0, keepdims=True) + jnp.exp2(sink - m)))
        pb = p.astype(BF16)
        p_all.append(jnp.concatenate([jnp.where(from_prev, pb, zero_p), jnp.where(from_prev, zero_p, pb)], axis=0))
    return jnp.concatenate(p_all, axis=1), r_den


def _attention_output(unit, probs, r_den, o_ref):
    b, cur, kv, _, v_cat = unit
    blk, hd_dim = ATT_BLOCK, ATT_HEAD_DIM
    o_all = _dot(v_cat, probs)
    for g in range(ATT_GROUP):
        hd = kv * ATT_GROUP + g
        o_ref[hd * hd_dim:(hd + 1) * hd_dim, cur] = (o_all[:, g * blk:(g + 1) * blk] * r_den[g]).astype(BF16)


def _bf16_part(x):
    bits = pltpu.bitcast(x, jnp.uint32) & jnp.uint32(0xFFFF0000)
    return pltpu.bitcast(bits, F32)


def _cumsum_rows(tril, x):
    lane = lax.broadcasted_iota(jnp.int32, x.shape, 1)
    w = 2 * M_HEADS
    h1 = _bf16_part(x)
    r1 = x - h1
    h2 = _bf16_part(r1)
    h3 = r1 - h2
    packed = jnp.where(lane < w, h1, jnp.where(lane < 2 * w, pltpu.roll(h2, w, 1), pltpu.roll(h3, 2 * w, 1)))
    c = _dot(tril, packed.astype(BF16))
    return c + pltpu.roll(c, LANES - w, 1) + pltpu.roll(c, LANES - 2 * w, 1)


def _split3(x):
    h1 = _bf16_part(x)
    r1 = x - h1
    h2 = _bf16_part(r1)
    return h1.astype(BF16), h2.astype(BF16), (r1 - h2).astype(BF16)


def _mlstm_gates_and_scores(q_ref, k_ref, gcol_ref, grow_ref):
    L = M_CHUNK
    src = lax.broadcasted_iota(jnp.int32, (L, L), 0)
    dst = lax.broadcasted_iota(jnp.int32, (L, L), 1)
    tril = jnp.where(dst <= src, 1.0, 0.0).astype(BF16)
    triu = jnp.where(src <= dst, 1.0, 0.0).astype(BF16)
    zeros = jnp.zeros((M_QK_DIM, L), BF16)
    chunks = []
    for c in range(q_ref.shape[1] // L):
        cur = slice(c * L, (c + 1) * L)
        gcol = gcol_ref[cur, :]
        grow = grow_ref[:, cur]
        bc = _cumsum_rows(tril, gcol)
        br = sum(_dot(t, triu) for t in _split3(grow))
        pairs = []
        for j in range(M_HEADS // 2):
            k2 = k_ref[cur, j * LANES:(j + 1) * LANES]
            q2 = q_ref[j * LANES:(j + 1) * LANES, cur]
            q_pad = jnp.concatenate([jnp.concatenate([q2[:M_QK_DIM], zeros], axis=0),
                                     jnp.concatenate([zeros, q2[M_QK_DIM:]], axis=0)], axis=1)
            pairs.append(dict(k2=k2, q_pad=q_pad, s_t=_dot(k2, q_pad), heads=[{}, {}]))
        chunks.append(dict(cur=cur, gcol=gcol, grow=grow, bc=bc, br=br, pairs=pairs))
    return chunks


def _mlstm_weights(chunks, v_ref, m_ref, n_state_rows):
    L = M_CHUNK
    src = lax.broadcasted_iota(jnp.int32, (L, L), 0)
    dst = lax.broadcasted_iota(jnp.int32, (L, L), 1)
    causal = src <= dst
    zeros = jnp.zeros((L, L), BF16)
    m_run = [m_ref[h:h + 1, :] for h in range(M_HEADS)]
    for ch in chunks:
        cur, gcol, grow, bc, br = ch["cur"], ch["gcol"], ch["grow"], ch["bc"], ch["br"]
        r_col = gcol - pltpu.roll(bc, LANES - M_HEADS, 1)
        for j, pr in enumerate(ch["pairs"]):
            for hh, hd in enumerate(pr["heads"]):
                h = 2 * j + hh
                vh = v_ref[h * M_V_DIM:(h + 1) * M_V_DIM, cur]
                b_row = br[M_HEADS + h:M_HEADS + h + 1, :]
                li_row = grow[h:h + 1, :]
                m_prev = m_run[h]
                log_d = jnp.where(causal, jnp.broadcast_to(r_col[:, h:h + 1], (L, L)) + b_row, NEG_BIG)
                m_t = jnp.maximum(b_row + m_prev, jnp.max(log_d, axis=0, keepdims=True))
                w_t = jnp.exp2(log_d - m_t) * pr["s_t"][:, hh * L:(hh + 1) * L]
                g = b_row[:, L - 1:L]
                w_end = g - b_row + li_row
                m_loc = jnp.max(w_end, axis=1, keepdims=True)
                e_row = jnp.exp2(w_end - m_loc).astype(BF16)
                m_new = jnp.maximum(g + m_prev, m_loc)
                hd.update(
                    vh=vh, m_t=m_t, w_bf=w_t.astype(BF16), w_sum=jnp.sum(w_t, axis=0, keepdims=True),
                    a_int=jnp.exp2(b_row + m_prev - m_t),
                    lhs=jnp.concatenate([vh * e_row, jnp.broadcast_to(e_row, (n_state_rows - M_V_DIM, L))], axis=0),
                    decay=jnp.exp2(g + m_prev - m_new), gain=jnp.exp2(m_loc - m_new))
                m_run[h] = m_new
            ha, hb = pr["heads"]
            pr["v_cat"] = jnp.concatenate([ha["vh"], hb["vh"]], axis=1)
            pr["w_diag"] = jnp.concatenate([jnp.concatenate([ha["w_bf"], zeros], axis=1),
                                            jnp.concatenate([zeros, hb["w_bf"]], axis=1)], axis=0)
    for h in range(M_HEADS):
        m_ref[h:h + 1, :] = m_run[h]


def _mlstm_intra(chunks):
    for ch in chunks:
        for pr in ch["pairs"]:
            pr["num"] = _dot(pr["v_cat"], pr["w_diag"])
            for hd in pr["heads"]:
                hd["d_state"] = _dot(hd["lhs"], pr["k2"])


def _mlstm_states(chunks, state_ref):
    first_head = lax.broadcasted_iota(jnp.int32, state_ref.shape[1:], 1) < M_QK_DIM
    for j in range(M_HEADS // 2):
        state = state_ref[j]
        for ch in chunks:
            pr = ch["pairs"][j]
            ha, hb = pr["heads"]
            pr["state_in"] = state.astype(BF16)
            state = (jnp.where(first_head, ha["decay"], hb["decay"]) * state
                     + jnp.where(first_head, ha["gain"], hb["gain"]) * jnp.where(first_head, ha["d_state"], hb["d_state"]))
        state_ref[j] = state
    for ch in chunks:
        for pr in ch["pairs"]:
            pr["inter"] = _dot(pr["state_in"], pr["q_pad"])


def _mlstm_outputs(chunks, osig_ref, gain_ref, o_ref):
    L = M_CHUNK
    for ch in chunks:
        cur = ch["cur"]
        for j, pr in enumerate(ch["pairs"]):
            for hh, hd in enumerate(pr["heads"]):
                h = 2 * j + hh
                a_int = hd["a_int"]
                inter = pr["inter"][:, hh * L:(hh + 1) * L]
                num = pr["num"][:, hh * L:(hh + 1) * L] + a_int * inter[:M_V_DIM]
                den = hd["w_sum"] + a_int * inter[M_V_DIM:M_V_DIM + 1]
                d = jnp.maximum(jnp.abs(den), jnp.exp2(-hd["m_t"]))
                scale = lax.rsqrt(jnp.mean(num * num, axis=0, keepdims=True) + EPS * (d * d))
                rows = slice(h * M_V_DIM, (h + 1) * M_V_DIM)
                o_ref[rows, cur] = ((num * scale) * (gain_ref[rows, :] * osig_ref[rows, cur].astype(F32))).astype(BF16)


def _mixers_kernel(sinks_ref, qa_ref, ka_ref, kap_ref, va_ref, vap_ref,
                   qm_ref, km_ref, vm_ref, osig_ref, gcol_ref, grow_ref, gain_ref,
                   att_ref, hm_ref, state_ref, m_ref, *, layer):
    first = pl.program_id(1) == 0

    @pl.when(first)
    def _():
        state_ref[...] = jnp.zeros(state_ref.shape, F32)
        m_ref[...] = jnp.zeros(m_ref.shape, F32)

    units = _attention_units(qa_ref, ka_ref, kap_ref, va_ref, vap_ref)
    scores = [_attention_scores(u, qa_ref) for u in units]
    chunks = _mlstm_gates_and_scores(qm_ref, km_ref, gcol_ref, grow_ref)
    probs = [_attention_softmax(u, s, first, sinks_ref, layer) for u, s in zip(units, scores)]
    _mlstm_weights(chunks, vm_ref, m_ref, state_ref.shape[1])
    for u, (p, r_den) in zip(units, probs):
        _attention_output(u, p, r_den, att_ref)
    _mlstm_intra(chunks)
    _mlstm_states(chunks, state_ref)
    _mlstm_outputs(chunks, osig_ref, gain_ref, hm_ref)


def _mixers(sinks, qa_t, ka, va_t, qm_t, km, vm_t, osig_t, gcol, grow, gain_b, *, layer, seq, tc):
    T = km.shape[0]
    n = seq // tc
    per = tc // ATT_BLOCK
    tile = lambda b, c: b * n + c
    prev_blk = lambda b, c: jnp.maximum(tile(b, c) * per - 1, 0)
    row = lambda w: pl.BlockSpec((tc, w), lambda b, c, *_: (tile(b, c), 0))
    colm = lambda w: pl.BlockSpec((w, tc), lambda b, c, *_: (0, tile(b, c)))
    grid_spec = pltpu.PrefetchScalarGridSpec(
        num_scalar_prefetch=1, grid=(T // seq, n),
        in_specs=[colm(ATT_Q_W), row(ATT_KV_W),
                  pl.BlockSpec((ATT_BLOCK, ATT_KV_W), lambda b, c, *_: (prev_blk(b, c), 0)),
                  colm(ATT_KV_W),
                  pl.BlockSpec((ATT_KV_W, ATT_BLOCK), lambda b, c, *_: (0, prev_blk(b, c))),
                  colm(M_QK_W), row(M_QK_W), colm(M_V_W), colm(M_V_W), row(LANES), colm(2 * M_HEADS),
                  _layer_spec(gain_b, layer)],
        out_specs=(colm(ATT_Q_W), colm(M_V_W)),
        scratch_shapes=[pltpu.VMEM((M_HEADS // 2, M_V_DIM + SUBLANES_BF16, LANES), F32),
                        pltpu.VMEM((2 * M_HEADS, LANES), F32)])
    return pl.pallas_call(
        functools.partial(_mixers_kernel, layer=layer), grid_spec=grid_spec,
        out_shape=(jax.ShapeDtypeStruct((ATT_Q_W, T), BF16), jax.ShapeDtypeStruct((M_V_W, T), BF16)),
        compiler_params=_params("parallel", "arbitrary"), name="mixers",
    )(sinks, qa_t, ka, ka, va_t, va_t, qm_t, km, vm_t, osig_t, gcol, grow, gain_b)


def _post_kernel(x_ref, att_ref, hm_ref, sg_ref, wa_ref, wm_ref, wo_ref, gain_ref, w1_ref, w2_ref, o_ref, *, n_chunks):
    D = x_ref.shape[1]
    a = _dot_tn(att_ref[...], wa_ref[...])
    m = _dot_tn(hm_ref[...], wm_ref[...])
    mixed = sg_ref[:, :D].astype(F32) * a + sg_ref[:, D:].astype(F32) * m
    x = x_ref[...] + _dot(mixed.astype(BF16), wo_ref[...])
    inv = lax.rsqrt(jnp.mean(x * x, axis=-1, keepdims=True) + EPS)
    h = ((x * inv) * gain_ref[...]).astype(BF16)
    fc = w1_ref.shape[1] // n_chunks
    acc = x
    for c in range(n_chunks):
        u = jnp.maximum(_dot(h, w1_ref[:, c * fc:(c + 1) * fc]), 0.0)
        acc = acc + _dot((u * u).astype(BF16), w2_ref[c * fc:(c + 1) * fc, :])
    o_ref[...] = acc


def _post(x, att_t, hm_t, sg, wa, wm, wo, gain, w1, w2, *, layer, tm):
    T, D = x.shape
    row = lambda w: pl.BlockSpec((tm, w), lambda i: (i, 0))
    colm = lambda w: pl.BlockSpec((w, tm), lambda i: (0, i))
    return pl.pallas_call(
        functools.partial(_post_kernel, n_chunks=4), grid=(T // tm,),
        in_specs=[row(D), colm(ATT_Q_W), colm(M_V_W), row(2 * D)] + [
            _layer_spec(a, layer, single_buffer=True) for a in (wa, wm, wo, gain, w1, w2)],
        out_specs=row(D), out_shape=jax.ShapeDtypeStruct((T, D), F32),
        compiler_params=_params("parallel"), name="post",
    )(x, att_t, hm_t, sg, wa, wm, wo, gain, w1, w2)


def _rope_tables(seq):
    pos = jnp.arange(seq, dtype=F32)
    inv_freq = ROPE_THETA ** (-jnp.arange(0, ROPE_DIM, 2, dtype=F32) / ROPE_DIM)
    ang_t = inv_freq[:, None] * pos[None, :]
    d = jnp.arange(LANES) % ATT_HEAD_DIM
    ang = pos[:, None] * inv_freq[d % ROPE_HALF][None, :]
    cos, sin = jnp.cos(ang), jnp.sin(ang)
    rc = jnp.where(d < ROPE_DIM, cos, 1.0)
    rs1 = jnp.where(d < ROPE_HALF, -sin, 0.0)
    rs2 = jnp.where((d >= ROPE_HALF) & (d < ROPE_DIM), sin, 0.0)
    return jnp.cos(ang_t), jnp.sin(ang_t), rc, rs1, rs2


def _prep_w_in_kernel(w_ref, fm_ref, tm_ref):
    w = w_ref[...]
    seg = lambda i: w[IN_OFFS[i]:IN_OFFS[i + 1], :]
    q_a, k_a, v_a, q_m, k_m, v_m, o_m, if_m, gates = (seg(i) for i in range(len(IN_WIDTHS)))
    fm_ref[...] = jnp.concatenate([o_m, q_a, v_a, q_m, v_m], axis=0).astype(BF16)
    if_pad = jnp.zeros((LANES - 2 * M_HEADS, w.shape[1]), F32)
    tm_ref[...] = jnp.concatenate([gates, k_a, k_m, if_m, if_pad], axis=0).T.astype(BF16)


def _prep_w_in(w_in):
    depth, D, d_in = w_in.shape
    return pl.pallas_call(
        _prep_w_in_kernel, grid=(depth, D // PREP_COLS),
        in_specs=[pl.BlockSpec((None, d_in, PREP_COLS), lambda l, c: (l, 0, c))],
        out_specs=(pl.BlockSpec((None, FM_OFFS[-1], PREP_COLS), lambda l, c: (l, 0, c)),
                   pl.BlockSpec((None, PREP_COLS, TM_OFFS[-1]), lambda l, c: (l, c, 0))),
        out_shape=(jax.ShapeDtypeStruct((depth, FM_OFFS[-1], D), BF16),
                   jax.ShapeDtypeStruct((depth, D, TM_OFFS[-1]), BF16)),
        compiler_params=_params("parallel", "parallel"), name="prep_w_in",
    )(jnp.swapaxes(w_in, 1, 2))


def kernel(x, norm_mix, w_in, att_q_norm, att_k_norm, att_sinks, m_gate_bias, m_head_norm,
           w_att_branch, w_m_branch, w_out, norm_ffn, w_ff1, w_ff2):
    B, S, D = x.shape
    T = B * S
    depth = w_in.shape[0]
    tm = min(TOKEN_TILE, S)
    cos_t, sin_t, rc, rs1, rs2 = _rope_tables(S)
    w_fm, w_tm = _prep_w_in(w_in)
    wa, wm, wo = w_att_branch.astype(BF16), w_m_branch.astype(BF16), w_out.astype(BF16)
    w1, w2 = w_ff1.astype(BF16), w_ff2.astype(BF16)
    bias = jnp.pad(m_gate_bias, ((0, 0), (0, LANES - 2 * M_HEADS))).reshape(depth, 1, LANES)
    tm_in = min(IN_PROJ_TILE, S)
    qgain = jnp.broadcast_to(att_q_norm[:, :, None], (depth, ATT_HEAD_DIM, tm_in))
    kgain = jnp.tile(att_k_norm, (1, 2)).reshape(depth, 1, LANES)
    mgain = jnp.broadcast_to(m_head_norm[:, :, None], (depth, M_V_W, LANES))
    gain_mix = norm_mix.reshape(depth, 1, D)
    gain_ffn = norm_ffn.reshape(depth, 1, D)
    sinks = att_sinks.reshape(depth * ATT_HEADS)
    xt = x.reshape(T, D)
    for l in range(depth):
        qa_t, va_t, qm_t, vm_t, osig_t, ka, km, sg, gcol, grow = _in_proj(
            xt, gain_mix, w_fm, w_tm, bias, qgain, kgain, cos_t, sin_t, rc, rs1, rs2, layer=l, seq=S, tm=tm_in)
        att_t, hm_t = _mixers(sinks, qa_t, ka, va_t, qm_t, km, vm_t, osig_t, gcol, grow, mgain, layer=l, seq=S, tc=tm)
        xt = _post(xt, att_t, hm_t, sg, wa, wm, wo, gain_ffn, w1, w2, layer=l, tm=tm)
    return xt.reshape(B, S, D)
```

```python
import functools

import jax
import jax.numpy as jnp
import numpy as np
from jax import lax
from jax.experimental import pallas as pl
from jax.experimental.pallas import tpu as pltpu

D_MODEL = 1024
ATT_HEADS = 8
ATT_KV_HEADS = 2
ATT_GROUP = ATT_HEADS // ATT_KV_HEADS
ATT_HEAD_DIM = 64
ATT_BLOCK = 128
ROPE_DIM = ATT_HEAD_DIM // 4
ROPE_HALF = ROPE_DIM // 2
ROPE_THETA = 500000.0
M_HEADS = 4
M_QK_DIM = 64
M_V_DIM = 128
M_CHUNK = 128
EPS = 1e-6

ATT_Q_W = ATT_HEADS * ATT_HEAD_DIM
ATT_KV_W = ATT_KV_HEADS * ATT_HEAD_DIM
M_QK_W = M_HEADS * M_QK_DIM
M_V_W = M_HEADS * M_V_DIM
IN_WIDTHS = (ATT_Q_W, ATT_KV_W, ATT_KV_W, M_QK_W, M_QK_W, M_V_W, M_V_W, 2 * M_HEADS, 2 * D_MODEL)
IN_OFFS = tuple(int(o) for o in np.concatenate([[0], np.cumsum(IN_WIDTHS)]))
LANES = 128
SUBLANES_BF16 = 16
FM_WIDTHS = (M_V_W, ATT_Q_W, ATT_KV_W, M_QK_W, M_V_W)
FM_OFFS = tuple(int(o) for o in np.concatenate([[0], np.cumsum(FM_WIDTHS)]))
TM_WIDTHS = (2 * D_MODEL, ATT_KV_W, M_QK_W, LANES)
TM_OFFS = tuple(int(o) for o in np.concatenate([[0], np.cumsum(TM_WIDTHS)]))

NEG_BIG = -1e30
LOG2_E = 1.4426950408889634
VMEM_LIMIT = 56 * 1024 * 1024
TOKEN_TILE = 512
MIXERS_TILE = 1024
IN_PROJ_TILE = 1024
PREP_COLS = 256

BF16 = jnp.bfloat16
F32 = jnp.float32


def _dot(a, b):
    return jnp.dot(a, b, preferred_element_type=F32)


def _dot_nt(a, b):
    return lax.dot_general(a, b, (((1,), (1,)), ((), ())), preferred_element_type=F32)


def _dot_tn(a, b):
    return lax.dot_general(a, b, (((0,), (0,)), ((), ())), preferred_element_type=F32)


def _sigmoid_of_twice(half_x):
    return 0.5 * jnp.tanh(half_x) + 0.5


def _log_sigmoid(x):
    return jnp.minimum(x, 0.0) - jnp.log(1.0 + jnp.exp(-jnp.abs(x)))


def _layer_spec(arr, layer, single_buffer=False):
    mode = dict(pipeline_mode=pl.Buffered(1)) if single_buffer else {}
    return pl.BlockSpec((None,) + arr.shape[1:], lambda *_: (layer,) + (0,) * (arr.ndim - 1), **mode)


def _params(*sem):
    return pltpu.CompilerParams(dimension_semantics=sem, vmem_limit_bytes=VMEM_LIMIT)


def _in_proj_kernel(x_ref, gain_ref, w_fm_ref, w_tm_ref, bias_ref, qgain_ref, kgain_ref,
                    cos_ref, sin_ref, rc_ref, rs1_ref, rs2_ref,
                    qa_ref, va_ref, qm_ref, vm_ref, om_ref, ka_ref, km_ref, sg_ref, gcol_ref, grow_ref):
    x = x_ref[...]
    inv = lax.rsqrt(jnp.mean(x * x, axis=-1, keepdims=True) + EPS)
    h = ((x * inv) * gain_ref[...]).astype(BF16)
    tm = x.shape[0]

    sg_ref[...] = _sigmoid_of_twice(_dot(h, w_tm_ref[:, TM_OFFS[0]:TM_OFFS[1]])).astype(BF16)
    om_ref[...] = _sigmoid_of_twice(_dot_nt(w_fm_ref[FM_OFFS[0]:FM_OFFS[1], :], h)).astype(BF16)

    zq = _dot_nt(w_fm_ref[FM_OFFS[1]:FM_OFFS[2], :], h)
    cos, sin = cos_ref[...], sin_ref[...]
    qgain = qgain_ref[...] * (ATT_HEAD_DIM ** -0.5 * LOG2_E)
    for hd in range(ATT_HEADS):
        z = zq[hd * ATT_HEAD_DIM:(hd + 1) * ATT_HEAD_DIM, :]
        inv_h = lax.rsqrt(jnp.mean(z * z, axis=0, keepdims=True) + EPS)
        y = (z * inv_h) * qgain
        x1, x2 = y[:ROPE_HALF], y[ROPE_HALF:ROPE_DIM]
        y = jnp.concatenate([x1 * cos - x2 * sin, x2 * cos + x1 * sin, y[ROPE_DIM:]], axis=0)
        qa_ref[hd * ATT_HEAD_DIM:(hd + 1) * ATT_HEAD_DIM, :] = y.astype(BF16)

    zn = _dot(h, w_tm_ref[:, TM_OFFS[1]:TM_OFFS[4]])
    zr = _dot_nt(w_fm_ref[FM_OFFS[2]:FM_OFFS[5], :], h)
    va_ref[...] = zr[:ATT_KV_W, :].astype(BF16)
    qm_ref[...] = (zr[ATT_KV_W:ATT_KV_W + M_QK_W, :] * (M_QK_DIM ** -0.5)).astype(BF16)
    vm_ref[...] = zr[ATT_KV_W + M_QK_W:, :].astype(BF16)

    lane = lax.broadcasted_iota(jnp.int32, (tm, LANES), 1)
    low = lane < ATT_HEAD_DIM
    zk = zn[:, :ATT_KV_W]
    s = zk * zk
    r0 = jnp.sum(jnp.where(low, s, 0.0), axis=-1, keepdims=True)
    r1 = jnp.sum(jnp.where(low, 0.0, s), axis=-1, keepdims=True)
    inv_k = lax.rsqrt(jnp.where(low, r0, r1) * (1.0 / ATT_HEAD_DIM) + EPS)
    y = (zk * inv_k) * kgain_ref[...]
    up = pltpu.roll(y, LANES - ROPE_HALF, 1)
    dn = pltpu.roll(y, ROPE_HALF, 1)
    ka_ref[...] = (y * rc_ref[...] + up * rs1_ref[...] + dn * rs2_ref[...]).astype(BF16)
    km_ref[...] = zn[:, ATT_KV_W:ATT_KV_W + M_QK_W].astype(BF16)
    zi = zn[:, ATT_KV_W + M_QK_W:] + bias_ref[...]
    g = jnp.where(lane < M_HEADS, zi, jnp.where(lane < 2 * M_HEADS, _log_sigmoid(zi), 0.0)) * LOG2_E
    gcol_ref[...] = g
    grow_ref[...] = g.T[:2 * M_HEADS, :]


def _in_proj(x, gain, w_fm, w_tm, bias, qgain, kgain, cos_t, sin_t, rc, rs1, rs2, *, layer, seq, tm):
    T, D = x.shape
    n_pos = seq // tm
    row = lambda w: pl.BlockSpec((tm, w), lambda i: (i, 0))
    colm = lambda w: pl.BlockSpec((w, tm), lambda i: (0, i))
    pos = pl.BlockSpec((tm, LANES), lambda i: (i % n_pos, 0))
    pos_t = pl.BlockSpec((ROPE_HALF, tm), lambda i: (0, i % n_pos))
    fm = lambda w: jax.ShapeDtypeStruct((w, T), BF16)
    out_shapes = (fm(ATT_Q_W), fm(ATT_KV_W), fm(M_QK_W), fm(M_V_W), fm(M_V_W),
                  jax.ShapeDtypeStruct((T, ATT_KV_W), BF16),
                  jax.ShapeDtypeStruct((T, M_QK_W), BF16),
                  jax.ShapeDtypeStruct((T, 2 * D), BF16),
                  jax.ShapeDtypeStruct((T, LANES), F32),
                  jax.ShapeDtypeStruct((2 * M_HEADS, T), F32))
    out_specs = (colm(ATT_Q_W), colm(ATT_KV_W), colm(M_QK_W), colm(M_V_W), colm(M_V_W),
                 row(ATT_KV_W), row(M_QK_W), row(2 * D), row(LANES), colm(2 * M_HEADS))
    in_specs = [row(D)] + [_layer_spec(a, layer, single_buffer=True)
                           for a in (gain, w_fm, w_tm, bias, qgain, kgain)] + [
        pos_t, pos_t, pos, pos, pos]
    return pl.pallas_call(
        _in_proj_kernel, grid=(T // tm,), in_specs=in_specs, out_specs=out_specs, out_shape=out_shapes,
        compiler_params=_params("parallel"), name="in_proj",
    )(x, gain, w_fm, w_tm, bias, qgain, kgain, cos_t, sin_t, rc, rs1, rs2)


def _attention_units(q_ref, k_ref, kp_ref, v_ref, vp_ref):
    blk, hd_dim = ATT_BLOCK, ATT_HEAD_DIM
    units = []
    for b in range(q_ref.shape[1] // blk):
        cur = slice(b * blk, (b + 1) * blk)
        old = slice((b - 1) * blk, b * blk)
        k_cat = jnp.concatenate([kp_ref[...] if b == 0 else k_ref[old, :], k_ref[cur, :]], axis=0)
        v_prev = vp_ref[...] if b == 0 else v_ref[:, old]
        v_cur = v_ref[:, cur]
        for kv in range(ATT_KV_HEADS):
            rows = slice(kv * hd_dim, (kv + 1) * hd_dim)
            v_cat = jnp.concatenate([v_prev[rows, :], v_cur[rows, :]], axis=1)
            units.append((b, cur, kv, k_cat, v_cat))
    return units


def _attention_scores(unit, q_ref):
    b, cur, kv, k_cat, _ = unit
    hd_dim = ATT_HEAD_DIM
    zeros = jnp.zeros((hd_dim, ATT_BLOCK), BF16)
    q_pad = []
    for g in range(ATT_GROUP):
        hd = kv * ATT_GROUP + g
        qh = q_ref[hd * hd_dim:(hd + 1) * hd_dim, cur]
        q_pad.append(jnp.concatenate([qh, zeros] if kv == 0 else [zeros, qh], axis=0))
    return _dot(k_cat, jnp.concatenate(q_pad, axis=1))


def _attention_softmax(unit, s_all, first, sinks_ref, layer):
    b, cur, kv, _, _ = unit
    blk = ATT_BLOCK
    key = lax.broadcasted_iota(jnp.int32, (blk, blk), 0)
    qry = lax.broadcasted_iota(jnp.int32, (blk, blk), 1)
    from_prev = key > qry
    zero_p = jnp.zeros((blk, blk), BF16)
    p_all, r_den = [], []
    for g in range(ATT_GROUP):
        sink = sinks_ref[layer * ATT_HEADS + kv * ATT_GROUP + g] * LOG2_E
        s_prev = s_all[:blk, g * blk:(g + 1) * blk]
        if b == 0:
            s_prev = s_prev + jnp.where(first, NEG_BIG, 0.0)
        s = jnp.where(from_prev, s_prev, s_all[blk:, g * blk:(g + 1) * blk])
        m = jnp.maximum(jnp.max(s, axis=0, keepdims=True), sink)
        p = jnp.exp2(s - m)
        r_den.append(1.0 / (jnp.sum(p, axis=0, keepdims=True) + jnp.exp2(sink - m)))
        pb = p.astype(BF16)
        p_all.append(jnp.concatenate([jnp.where(from_prev, pb, zero_p), jnp.where(from_prev, zero_p, pb)], axis=0))
    return jnp.concatenate(p_all, axis=1), r_den


def _attention_output(unit, probs, r_den, o_ref):
    b, cur, kv, _, v_cat = unit
    blk, hd_dim = ATT_BLOCK, ATT_HEAD_DIM
    o_all = _dot(v_cat, probs)
    for g in range(ATT_GROUP):
        hd = kv * ATT_GROUP + g
        o_ref[hd * hd_dim:(hd + 1) * hd_dim, cur] = (o_all[:, g * blk:(g + 1) * blk] * r_den[g]).astype(BF16)


def _bf16_part(x):
    bits = pltpu.bitcast(x, jnp.uint32) & jnp.uint32(0xFFFF0000)
    return pltpu.bitcast(bits, F32)


def _cumsum_rows(tril, x):
    lane = lax.broadcasted_iota(jnp.int32, x.shape, 1)
    w = 2 * M_HEADS
    h1 = _bf16_part(x)
    r1 = x - h1
    h2 = _bf16_part(r1)
    h3 = r1 - h2
    packed = jnp.where(lane < w, h1, jnp.where(lane < 2 * w, pltpu.roll(h2, w, 1), pltpu.roll(h3, 2 * w, 1)))
    c = _dot(tril, packed.astype(BF16))
    return c + pltpu.roll(c, LANES - w, 1) + pltpu.roll(c, LANES - 2 * w, 1)


def _split3(x):
    h1 = _bf16_part(x)
    r1 = x - h1
    h2 = _bf16_part(r1)
    return h1.astype(BF16), h2.astype(BF16), (r1 - h2).astype(BF16)


def _mlstm_gates_and_scores(q_ref, k_ref, gcol_ref, grow_ref):
    L = M_CHUNK
    src = lax.broadcasted_iota(jnp.int32, (L, L), 0)
    dst = lax.broadcasted_iota(jnp.int32, (L, L), 1)
    tril = jnp.where(dst <= src, 1.0, 0.0).astype(BF16)
    triu = jnp.where(src <= dst, 1.0, 0.0).astype(BF16)
    zeros = jnp.zeros((M_QK_DIM, L), BF16)
    chunks = []
    for c in range(q_ref.shape[1] // L):
        cur = slice(c * L, (c + 1) * L)
        gcol = gcol_ref[cur, :]
        grow = grow_ref[:, cur]
        bc = _cumsum_rows(tril, gcol)
        br = sum(_dot(t, triu) for t in _split3(grow))
        pairs = []
        for j in range(M_HEADS // 2):
            k2 = k_ref[cur, j * LANES:(j + 1) * LANES]
            q2 = q_ref[j * LANES:(j + 1) * LANES, cur]
            q_pad = jnp.concatenate([jnp.concatenate([q2[:M_QK_DIM], zeros], axis=0),
                                     jnp.concatenate([zeros, q2[M_QK_DIM:]], axis=0)], axis=1)
            pairs.append(dict(k2=k2, q_pad=q_pad, s_t=_dot(k2, q_pad), heads=[{}, {}]))
        chunks.append(dict(cur=cur, gcol=gcol, grow=grow, bc=bc, br=br, pairs=pairs))
    return chunks


def _mlstm_weights(chunks, v_ref, m_ref, n_state_rows):
    L = M_CHUNK
    src = lax.broadcasted_iota(jnp.int32, (L, L), 0)
    dst = lax.broadcasted_iota(jnp.int32, (L, L), 1)
    causal = src <= dst
    zeros = jnp.zeros((L, L), BF16)
    m_run = [m_ref[h:h + 1, :] for h in range(M_HEADS)]
    for ch in chunks:
        cur, gcol, grow, bc, br = ch["cur"], ch["gcol"], ch["grow"], ch["bc"], ch["br"]
        r_col = gcol - pltpu.roll(bc, LANES - M_HEADS, 1)
        for j, pr in enumerate(ch["pairs"]):
            for hh, hd in enumerate(pr["heads"]):
                h = 2 * j + hh
                vh = v_ref[h * M_V_DIM:(h + 1) * M_V_DIM, cur]
                b_row = br[M_HEADS + h:M_HEADS + h + 1, :]
                li_row = grow[h:h + 1, :]
                m_prev = m_run[h]
                log_d = jnp.where(causal, jnp.broadcast_to(r_col[:, h:h + 1], (L, L)) + b_row, NEG_BIG)
                m_t = jnp.maximum(b_row + m_prev, jnp.max(log_d, axis=0, keepdims=True))
                w_t = jnp.exp2(log_d - m_t) * pr["s_t"][:, hh * L:(hh + 1) * L]
                g = b_row[:, L - 1:L]
                w_end = g - b_row + li_row
                m_loc = jnp.max(w_end, axis=1, keepdims=True)
                e_row = jnp.exp2(w_end - m_loc).astype(BF16)
                m_new = jnp.maximum(g + m_prev, m_loc)
                hd.update(
                    vh=vh, m_t=m_t, w_bf=w_t.astype(BF16), w_sum=jnp.sum(w_t, axis=0, keepdims=True),
                    a_int=jnp.exp2(b_row + m_prev - m_t),
                    lhs=jnp.concatenate([vh * e_row, jnp.broadcast_to(e_row, (n_state_rows - M_V_DIM, L))], axis=0),
                    decay=jnp.exp2(g + m_prev - m_new), gain=jnp.exp2(m_loc - m_new))
                m_run[h] = m_new
            ha, hb = pr["heads"]
            pr["v_cat"] = jnp.concatenate([ha["vh"], hb["vh"]], axis=1)
            pr["w_diag"] = jnp.concatenate([jnp.concatenate([ha["w_bf"], zeros], axis=1),
                                            jnp.concatenate([zeros, hb["w_bf"]], axis=1)], axis=0)
    for h in range(M_HEADS):
        m_ref[h:h + 1, :] = m_run[h]


def _mlstm_intra(chunks):
    for ch in chunks:
        for pr in ch["pairs"]:
            pr["num"] = _dot(pr["v_cat"], pr["w_diag"])
            for hd in pr["heads"]:
                hd["d_state"] = _dot(hd["lhs"], pr["k2"])


def _mlstm_states(chunks, state_ref):
    first_head = lax.broadcasted_iota(jnp.int32, state_ref.shape[1:], 1) < M_QK_DIM
    for j in range(M_HEADS // 2):
        state = state_ref[j]
        for ch in chunks:
            pr = ch["pairs"][j]
            ha, hb = pr["heads"]
            pr["state_in"] = state.astype(BF16)
            state = (jnp.where(first_head, ha["decay"], hb["decay"]) * state
                     + jnp.where(first_head, ha["gain"], hb["gain"]) * jnp.where(first_head, ha["d_state"], hb["d_state"]))
        state_ref[j] = state
    for ch in chunks:
        for pr in ch["pairs"]:
            pr["inter"] = _dot(pr["state_in"], pr["q_pad"])


def _mlstm_outputs(chunks, osig_ref, gain_ref, o_ref):
    L = M_CHUNK
    for ch in chunks:
        cur = ch["cur"]
        for j, pr in enumerate(ch["pairs"]):
            for hh, hd in enumerate(pr["heads"]):
                h = 2 * j + hh
                a_int = hd["a_int"]
                inter = pr["inter"][:, hh * L:(hh + 1) * L]
                num = pr["num"][:, hh * L:(hh + 1) * L] + a_int * inter[:M_V_DIM]
                den = hd["w_sum"] + a_int * inter[M_V_DIM:M_V_DIM + 1]
                d = jnp.maximum(jnp.abs(den), jnp.exp2(-hd["m_t"]))
                scale = lax.rsqrt(jnp.mean(num * num, axis=0, keepdims=True) + EPS * (d * d))
                rows = slice(h * M_V_DIM, (h + 1) * M_V_DIM)
                o_ref[rows, cur] = ((num * scale) * (gain_ref[rows, :] * osig_ref[rows, cur].astype(F32))).astype(BF16)


def _mixers_kernel(sinks_ref, qa_ref, ka_ref, kap_ref, va_ref, vap_ref,
                   qm_ref, km_ref, vm_ref, osig_ref, gcol_ref, grow_ref, gain_ref,
                   att_ref, hm_ref, state_ref, m_ref, *, layer):
    first = pl.program_id(1) == 0

    @pl.when(first)
    def _():
        state_ref[...] = jnp.zeros(state_ref.shape, F32)
        m_ref[...] = jnp.zeros(m_ref.shape, F32)

    units = _attention_units(qa_ref, ka_ref, kap_ref, va_ref, vap_ref)
    scores = [_attention_scores(u, qa_ref) for u in units]
    chunks = _mlstm_gates_and_scores(qm_ref, km_ref, gcol_ref, grow_ref)
    probs = [_attention_softmax(u, s, first, sinks_ref, layer) for u, s in zip(units, scores)]
    _mlstm_weights(chunks, vm_ref, m_ref, state_ref.shape[1])
    for u, (p, r_den) in zip(units, probs):
        _attention_output(u, p, r_den, att_ref)
    _mlstm_intra(chunks)
    _mlstm_states(chunks, state_ref)
    _mlstm_outputs(chunks, osig_ref, gain_ref, hm_ref)


def _mixers(sinks, qa_t, ka, va_t, qm_t, km, vm_t, osig_t, gcol, grow, gain_b, *, layer, seq, tc):
    T = km.shape[0]
    n = seq // tc
    per = tc // ATT_BLOCK
    tile = lambda b, c: b * n + c
    prev_blk = lambda b, c: jnp.maximum(tile(b, c) * per - 1, 0)
    row = lambda w: pl.BlockSpec((tc, w), lambda b, c, *_: (tile(b, c), 0))
    colm = lambda w: pl.BlockSpec((w, tc), lambda b, c, *_: (0, tile(b, c)))
    grid_spec = pltpu.PrefetchScalarGridSpec(
        num_scalar_prefetch=1, grid=(T // seq, n),
        in_specs=[colm(ATT_Q_W), row(ATT_KV_W),
                  pl.BlockSpec((ATT_BLOCK, ATT_KV_W), lambda b, c, *_: (prev_blk(b, c), 0)),
                  colm(ATT_KV_W),
                  pl.BlockSpec((ATT_KV_W, ATT_BLOCK), lambda b, c, *_: (0, prev_blk(b, c))),
                  colm(M_QK_W), row(M_QK_W), colm(M_V_W), colm(M_V_W), row(LANES), colm(2 * M_HEADS),
                  _layer_spec(gain_b, layer)],
        out_specs=(colm(ATT_Q_W), colm(M_V_W)),
        scratch_shapes=[pltpu.VMEM((M_HEADS // 2, M_V_DIM + SUBLANES_BF16, LANES), F32),
                        pltpu.VMEM((2 * M_HEADS, LANES), F32)])
    return pl.pallas_call(
        functools.partial(_mixers_kernel, layer=layer), grid_spec=grid_spec,
        out_shape=(jax.ShapeDtypeStruct((ATT_Q_W, T), BF16), jax.ShapeDtypeStruct((M_V_W, T), BF16)),
        compiler_params=_params("parallel", "arbitrary"), name="mixers",
    )(sinks, qa_t, ka, ka, va_t, va_t, qm_t, km, vm_t, osig_t, gcol, grow, gain_b)


def _post_kernel(x_ref, att_ref, hm_ref, sg_ref, wa_ref, wm_ref, wo_ref, gain_ref, w1_ref, w2_ref, o_ref, *, n_chunks):
    D = x_ref.shape[1]
    a = _dot_tn(att_ref[...], wa_ref[...])
    m = _dot_tn(hm_ref[...], wm_ref[...])
    mixed = sg_ref[:, :D].astype(F32) * a + sg_ref[:, D:].astype(F32) * m
    x = x_ref[...] + _dot(mixed.astype(BF16), wo_ref[...])
    inv = lax.rsqrt(jnp.mean(x * x, axis=-1, keepdims=True) + EPS)
    h = ((x * inv) * gain_ref[...]).astype(BF16)
    fc = w1_ref.shape[1] // n_chunks
    acc = x
    for c in range(n_chunks):
        u = jnp.maximum(_dot(h, w1_ref[:, c * fc:(c + 1) * fc]), 0.0)
        acc = acc + _dot((u * u).astype(BF16), w2_ref[c * fc:(c + 1) * fc, :])
    o_ref[...] = acc


def _post(x, att_t, hm_t, sg, wa, wm, wo, gain, w1, w2, *, layer, tm):
    T, D = x.shape
    row = lambda w: pl.BlockSpec((tm, w), lambda i: (i, 0))
    colm = lambda w: pl.BlockSpec((w, tm), lambda i: (0, i))
    return pl.pallas_call(
        functools.partial(_post_kernel, n_chunks=4), grid=(T // tm,),
        in_specs=[row(D), colm(ATT_Q_W), colm(M_V_W), row(2 * D)] + [
            _layer_spec(a, layer, single_buffer=True) for a in (wa, wm, wo, gain, w1, w2)],
        out_specs=row(D), out_shape=jax.ShapeDtypeStruct((T, D), F32),
        compiler_params=_params("parallel"), name="post",
    )(x, att_t, hm_t, sg, wa, wm, wo, gain, w1, w2)


def _rope_tables(seq):
    pos = jnp.arange(seq, dtype=F32)
    inv_freq = ROPE_THETA ** (-jnp.arange(0, ROPE_DIM, 2, dtype=F32) / ROPE_DIM)
    ang_t = inv_freq[:, None] * pos[None, :]
    d = jnp.arange(LANES) % ATT_HEAD_DIM
    ang = pos[:, None] * inv_freq[d % ROPE_HALF][None, :]
    cos, sin = jnp.cos(ang), jnp.sin(ang)
    rc = jnp.where(d < ROPE_DIM, cos, 1.0)
    rs1 = jnp.where(d < ROPE_HALF, -sin, 0.0)
    rs2 = jnp.where((d >= ROPE_HALF) & (d < ROPE_DIM), sin, 0.0)
    return jnp.cos(ang_t), jnp.sin(ang_t), rc, rs1, rs2


def _prep_w_in_kernel(w_ref, fm_ref, tm_ref):
    w = w_ref[...]
    seg = lambda i: w[IN_OFFS[i]:IN_OFFS[i + 1], :]
    q_a, k_a, v_a, q_m, k_m, v_m, o_m, if_m, gates = (seg(i) for i in range(len(IN_WIDTHS)))
    fm_ref[...] = jnp.concatenate([0.5 * o_m, q_a, v_a, q_m, v_m], axis=0).astype(BF16)
    if_pad = jnp.zeros((LANES - 2 * M_HEADS, w.shape[1]), F32)
    tm_ref[...] = jnp.concatenate([0.5 * gates, k_a, k_m, if_m, if_pad], axis=0).T.astype(BF16)


def _prep_w_in(w_in):
    depth, D, d_in = w_in.shape
    return pl.pallas_call(
        _prep_w_in_kernel, grid=(depth, D // PREP_COLS),
        in_specs=[pl.BlockSpec((None, d_in, PREP_COLS), lambda l, c: (l, 0, c))],
        out_specs=(pl.BlockSpec((None, FM_OFFS[-1], PREP_COLS), lambda l, c: (l, 0, c)),
                   pl.BlockSpec((None, PREP_COLS, TM_OFFS[-1]), lambda l, c: (l, c, 0))),
        out_shape=(jax.ShapeDtypeStruct((depth, FM_OFFS[-1], D), BF16),
                   jax.ShapeDtypeStruct((depth, D, TM_OFFS[-1]), BF16)),
        compiler_params=_params("parallel", "parallel"), name="prep_w_in",
    )(jnp.swapaxes(w_in, 1, 2))


def kernel(x, norm_mix, w_in, att_q_norm, att_k_norm, att_sinks, m_gate_bias, m_head_norm,
           w_att_branch, w_m_branch, w_out, norm_ffn, w_ff1, w_ff2):
    B, S, D = x.shape
    T = B * S
    depth = w_in.shape[0]
    tm = min(TOKEN_TILE, S)
    cos_t, sin_t, rc, rs1, rs2 = _rope_tables(S)
    w_fm, w_tm = _prep_w_in(w_in)
    wa, wm, wo = w_att_branch.astype(BF16), w_m_branch.astype(BF16), w_out.astype(BF16)
    w1, w2 = w_ff1.astype(BF16), w_ff2.astype(BF16)
    bias = jnp.pad(m_gate_bias, ((0, 0), (0, LANES - 2 * M_HEADS))).reshape(depth, 1, LANES)
    tm_in = min(IN_PROJ_TILE, S)
    qgain = jnp.broadcast_to(att_q_norm[:, :, None], (depth, ATT_HEAD_DIM, tm_in))
    kgain = jnp.tile(att_k_norm, (1, 2)).reshape(depth, 1, LANES)
    mgain = jnp.broadcast_to(m_head_norm[:, :, None], (depth, M_V_W, LANES))
    gain_mix = norm_mix.reshape(depth, 1, D)
    gain_ffn = norm_ffn.reshape(depth, 1, D)
    sinks = att_sinks.reshape(depth * ATT_HEADS)
    xt = x.reshape(T, D)
    for l in range(depth):
        qa_t, va_t, qm_t, vm_t, osig_t, ka, km, sg, gcol, grow = _in_proj(
            xt, gain_mix, w_fm, w_tm, bias, qgain, kgain, cos_t, sin_t, rc, rs1, rs2, layer=l, seq=S, tm=tm_in)
        att_t, hm_t = _mixers(sinks, qa_t, ka, va_t, qm_t, km, vm_t, osig_t, gcol, grow, mgain, layer=l, seq=S, tc=min(MIXERS_TILE, S))
        xt = _post(xt, att_t, hm_t, sg, wa, wm, wo, gain_ffn, w1, w2, layer=l, tm=tm)
    return xt.reshape(B, S, D)
```

```python
import functools

import jax
import jax.numpy as jnp
import numpy as np
from jax import lax
from jax.experimental import pallas as pl
from jax.experimental.pallas import tpu as pltpu

D_MODEL = 1024
ATT_HEADS = 8
ATT_KV_HEADS = 2
ATT_GROUP = ATT_HEADS // ATT_KV_HEADS
ATT_HEAD_DIM = 64
ATT_BLOCK = 128
ROPE_DIM = ATT_HEAD_DIM // 4
ROPE_HALF = ROPE_DIM // 2
ROPE_THETA = 500000.0
M_HEADS = 4
M_QK_DIM = 64
M_V_DIM = 128
M_CHUNK = 128
EPS = 1e-6

ATT_Q_W = ATT_HEADS * ATT_HEAD_DIM
ATT_KV_W = ATT_KV_HEADS * ATT_HEAD_DIM
M_QK_W = M_HEADS * M_QK_DIM
M_V_W = M_HEADS * M_V_DIM
IN_WIDTHS = (ATT_Q_W, ATT_KV_W, ATT_KV_W, M_QK_W, M_QK_W, M_V_W, M_V_W, 2 * M_HEADS, 2 * D_MODEL)
IN_OFFS = tuple(int(o) for o in np.concatenate([[0], np.cumsum(IN_WIDTHS)]))
LANES = 128
SUBLANES_BF16 = 16
FM_WIDTHS = (M_V_W, ATT_Q_W, ATT_KV_W, M_QK_W, M_V_W)
FM_OFFS = tuple(int(o) for o in np.concatenate([[0], np.cumsum(FM_WIDTHS)]))
TM_WIDTHS = (2 * D_MODEL, ATT_KV_W, M_QK_W, LANES)
TM_OFFS = tuple(int(o) for o in np.concatenate([[0], np.cumsum(TM_WIDTHS)]))

NEG_BIG = -1e30
LOG2_E = 1.4426950408889634
VMEM_LIMIT = 56 * 1024 * 1024
TOKEN_TILE = 512
POST_SUB = 256
MIXERS_TILE = 1024
IN_PROJ_TILE = 1024
PREP_COLS = 256

BF16 = jnp.bfloat16
F32 = jnp.float32


def _dot(a, b):
    return jnp.dot(a, b, preferred_element_type=F32)


def _dot_nt(a, b):
    return lax.dot_general(a, b, (((1,), (1,)), ((), ())), preferred_element_type=F32)


def _dot_tn(a, b):
    return lax.dot_general(a, b, (((0,), (0,)), ((), ())), preferred_element_type=F32)


def _sigmoid_of_twice(half_x):
    return 0.5 * jnp.tanh(half_x) + 0.5


def _log_sigmoid(x):
    return jnp.minimum(x, 0.0) - jnp.log(1.0 + jnp.exp(-jnp.abs(x)))


def _layer_spec(arr, layer, single_buffer=False):
    mode = dict(pipeline_mode=pl.Buffered(1)) if single_buffer else {}
    return pl.BlockSpec((None,) + arr.shape[1:], lambda *_: (layer,) + (0,) * (arr.ndim - 1), **mode)


def _params(*sem):
    return pltpu.CompilerParams(dimension_semantics=sem, vmem_limit_bytes=VMEM_LIMIT)


def _in_proj_kernel(x_ref, gain_ref, w_fm_ref, w_tm_ref, bias_ref, qgain_ref, kgain_ref,
                    cos_ref, sin_ref, rc_ref, rs1_ref, rs2_ref,
                    qa_ref, va_ref, qm_ref, vm_ref, om_ref, ka_ref, km_ref, sg_ref, gcol_ref, grow_ref):
    x = x_ref[...]
    inv = lax.rsqrt(jnp.mean(x * x, axis=-1, keepdims=True) + EPS)
    h = ((x * inv) * gain_ref[...]).astype(BF16)
    tm = x.shape[0]

    sg_ref[...] = _sigmoid_of_twice(_dot(h, w_tm_ref[:, TM_OFFS[0]:TM_OFFS[1]])).astype(BF16)
    om_ref[...] = _sigmoid_of_twice(_dot_nt(w_fm_ref[FM_OFFS[0]:FM_OFFS[1], :], h)).astype(BF16)

    zq = _dot_nt(w_fm_ref[FM_OFFS[1]:FM_OFFS[2], :], h)
    cos, sin = cos_ref[...], sin_ref[...]
    qgain = qgain_ref[...] * (ATT_HEAD_DIM ** -0.5 * LOG2_E)
    for hd in range(ATT_HEADS):
        z = zq[hd * ATT_HEAD_DIM:(hd + 1) * ATT_HEAD_DIM, :]
        inv_h = lax.rsqrt(jnp.mean(z * z, axis=0, keepdims=True) + EPS)
        y = (z * inv_h) * qgain
        x1, x2 = y[:ROPE_HALF], y[ROPE_HALF:ROPE_DIM]
        y = jnp.concatenate([x1 * cos - x2 * sin, x2 * cos + x1 * sin, y[ROPE_DIM:]], axis=0)
        qa_ref[hd * ATT_HEAD_DIM:(hd + 1) * ATT_HEAD_DIM, :] = y.astype(BF16)

    zn = _dot(h, w_tm_ref[:, TM_OFFS[1]:TM_OFFS[4]])
    zr = _dot_nt(w_fm_ref[FM_OFFS[2]:FM_OFFS[5], :], h)
    va_ref[...] = zr[:ATT_KV_W, :].astype(BF16)
    qm_ref[...] = (zr[ATT_KV_W:ATT_KV_W + M_QK_W, :] * (M_QK_DIM ** -0.5)).astype(BF16)
    vm_ref[...] = zr[ATT_KV_W + M_QK_W:, :].astype(BF16)

    lane = lax.broadcasted_iota(jnp.int32, (tm, LANES), 1)
    low = lane < ATT_HEAD_DIM
    zk = zn[:, :ATT_KV_W]
    s = zk * zk
    r0 = jnp.sum(jnp.where(low, s, 0.0), axis=-1, keepdims=True)
    r1 = jnp.sum(jnp.where(low, 0.0, s), axis=-1, keepdims=True)
    inv_k = lax.rsqrt(jnp.where(low, r0, r1) * (1.0 / ATT_HEAD_DIM) + EPS)
    y = (zk * inv_k) * kgain_ref[...]
    up = pltpu.roll(y, LANES - ROPE_HALF, 1)
    dn = pltpu.roll(y, ROPE_HALF, 1)
    ka_ref[...] = (y * rc_ref[...] + up * rs1_ref[...] + dn * rs2_ref[...]).astype(BF16)
    km_ref[...] = zn[:, ATT_KV_W:ATT_KV_W + M_QK_W].astype(BF16)
    zi = zn[:, ATT_KV_W + M_QK_W:] + bias_ref[...]
    g = jnp.where(lane < M_HEADS, zi, jnp.where(lane < 2 * M_HEADS, _log_sigmoid(zi), 0.0)) * LOG2_E
    gcol_ref[...] = g
    grow_ref[...] = g.T[:2 * M_HEADS, :]


def _in_proj(x, gain, w_fm, w_tm, bias, qgain, kgain, cos_t, sin_t, rc, rs1, rs2, *, layer, seq, tm):
    T, D = x.shape
    n_pos = seq // tm
    row = lambda w: pl.BlockSpec((tm, w), lambda i: (i, 0))
    colm = lambda w: pl.BlockSpec((w, tm), lambda i: (0, i))
    pos = pl.BlockSpec((tm, LANES), lambda i: (i % n_pos, 0))
    pos_t = pl.BlockSpec((ROPE_HALF, tm), lambda i: (0, i % n_pos))
    fm = lambda w: jax.ShapeDtypeStruct((w, T), BF16)
    out_shapes = (fm(ATT_Q_W), fm(ATT_KV_W), fm(M_QK_W), fm(M_V_W), fm(M_V_W),
                  jax.ShapeDtypeStruct((T, ATT_KV_W), BF16),
                  jax.ShapeDtypeStruct((T, M_QK_W), BF16),
                  jax.ShapeDtypeStruct((T, 2 * D), BF16),
                  jax.ShapeDtypeStruct((T, LANES), F32),
                  jax.ShapeDtypeStruct((2 * M_HEADS, T), F32))
    out_specs = (colm(ATT_Q_W), colm(ATT_KV_W), colm(M_QK_W), colm(M_V_W), colm(M_V_W),
                 row(ATT_KV_W), row(M_QK_W), row(2 * D), row(LANES), colm(2 * M_HEADS))
    in_specs = [row(D)] + [_layer_spec(a, layer, single_buffer=True)
                           for a in (gain, w_fm, w_tm, bias, qgain, kgain)] + [
        pos_t, pos_t, pos, pos, pos]
    return pl.pallas_call(
        _in_proj_kernel, grid=(T // tm,), in_specs=in_specs, out_specs=out_specs, out_shape=out_shapes,
        compiler_params=_params("parallel"), name="in_proj",
    )(x, gain, w_fm, w_tm, bias, qgain, kgain, cos_t, sin_t, rc, rs1, rs2)


def _attention_units(q_ref, k_ref, kp_ref, v_ref, vp_ref):
    blk, hd_dim = ATT_BLOCK, ATT_HEAD_DIM
    units = []
    for b in range(q_ref.shape[1] // blk):
        cur = slice(b * blk, (b + 1) * blk)
        old = slice((b - 1) * blk, b * blk)
        k_cat = jnp.concatenate([kp_ref[...] if b == 0 else k_ref[old, :], k_ref[cur, :]], axis=0)
        v_prev = vp_ref[...] if b == 0 else v_ref[:, old]
        v_cur = v_ref[:, cur]
        for kv in range(ATT_KV_HEADS):
            rows = slice(kv * hd_dim, (kv + 1) * hd_dim)
            v_cat = jnp.concatenate([v_prev[rows, :], v_cur[rows, :]], axis=1)
            units.append((b, cur, kv, k_cat, v_cat))
    return units


def _attention_scores(unit, q_ref):
    b, cur, kv, k_cat, _ = unit
    hd_dim = ATT_HEAD_DIM
    zeros = jnp.zeros((hd_dim, ATT_BLOCK), BF16)
    q_pad = []
    for g in range(ATT_GROUP):
        hd = kv * ATT_GROUP + g
        qh = q_ref[hd * hd_dim:(hd + 1) * hd_dim, cur]
        q_pad.append(jnp.concatenate([qh, zeros] if kv == 0 else [zeros, qh], axis=0))
    return _dot(k_cat, jnp.concatenate(q_pad, axis=1))


def _attention_softmax(unit, s_all, first, sinks_ref, layer):
    b, cur, kv, _, _ = unit
    blk = ATT_BLOCK
    key = lax.broadcasted_iota(jnp.int32, (blk, blk), 0)
    qry = lax.broadcasted_iota(jnp.int32, (blk, blk), 1)
    from_prev = key > qry
    zero_p = jnp.zeros((blk, blk), BF16)
    p_all, r_den = [], []
    for g in range(ATT_GROUP):
        sink = sinks_ref[layer * ATT_HEADS + kv * ATT_GROUP + g] * LOG2_E
        s_prev = s_all[:blk, g * blk:(g + 1) * blk]
        if b == 0:
            s_prev = s_prev + jnp.where(first, NEG_BIG, 0.0)
        s = jnp.where(from_prev, s_prev, s_all[blk:, g * blk:(g + 1) * blk])
        m = jnp.maximum(jnp.max(s, axis=0, keepdims=True), sink)
        p = jnp.exp2(s - m)
        r_den.append(1.0 / (jnp.sum(p, axis=0, keepdims=True) + jnp.exp2(sink - m)))
        pb = p.astype(BF16)
        p_all.append(jnp.concatenate([jnp.where(from_prev, pb, zero_p), jnp.where(from_prev, zero_p, pb)], axis=0))
    return jnp.concatenate(p_all, axis=1), r_den


def _attention_output(unit, probs, r_den, o_ref):
    b, cur, kv, _, v_cat = unit
    blk, hd_dim = ATT_BLOCK, ATT_HEAD_DIM
    o_all = _dot(v_cat, probs)
    for g in range(ATT_GROUP):
        hd = kv * ATT_GROUP + g
        o_ref[hd * hd_dim:(hd + 1) * hd_dim, cur] = (o_all[:, g * blk:(g + 1) * blk] * r_den[g]).astype(BF16)


def _bf16_part(x):
    bits = pltpu.bitcast(x, jnp.uint32) & jnp.uint32(0xFFFF0000)
    return pltpu.bitcast(bits, F32)


def _cumsum_rows(tril, x):
    lane = lax.broadcasted_iota(jnp.int32, x.shape, 1)
    w = 2 * M_HEADS
    h1 = _bf16_part(x)
    r1 = x - h1
    h2 = _bf16_part(r1)
    h3 = r1 - h2
    packed = jnp.where(lane < w, h1, jnp.where(lane < 2 * w, pltpu.roll(h2, w, 1), pltpu.roll(h3, 2 * w, 1)))
    c = _dot(tril, packed.astype(BF16))
    return c + pltpu.roll(c, LANES - w, 1) + pltpu.roll(c, LANES - 2 * w, 1)


def _split3(x):
    h1 = _bf16_part(x)
    r1 = x - h1
    h2 = _bf16_part(r1)
    return h1.astype(BF16), h2.astype(BF16), (r1 - h2).astype(BF16)


def _mlstm_gates_and_scores(q_ref, k_ref, gcol_ref, grow_ref):
    L = M_CHUNK
    src = lax.broadcasted_iota(jnp.int32, (L, L), 0)
    dst = lax.broadcasted_iota(jnp.int32, (L, L), 1)
    tril = jnp.where(dst <= src, 1.0, 0.0).astype(BF16)
    triu = jnp.where(src <= dst, 1.0, 0.0).astype(BF16)
    zeros = jnp.zeros((M_QK_DIM, L), BF16)
    chunks = []
    for c in range(q_ref.shape[1] // L):
        cur = slice(c * L, (c + 1) * L)
        gcol = gcol_ref[cur, :]
        grow = grow_ref[:, cur]
        bc = _cumsum_rows(tril, gcol)
        br = sum(_dot(t, triu) for t in _split3(grow))
        pairs = []
        for j in range(M_HEADS // 2):
            k2 = k_ref[cur, j * LANES:(j + 1) * LANES]
            q2 = q_ref[j * LANES:(j + 1) * LANES, cur]
            q_pad = jnp.concatenate([jnp.concatenate([q2[:M_QK_DIM], zeros], axis=0),
                                     jnp.concatenate([zeros, q2[M_QK_DIM:]], axis=0)], axis=1)
            pairs.append(dict(k2=k2, q_pad=q_pad, s_t=_dot(k2, q_pad), heads=[{}, {}]))
        chunks.append(dict(cur=cur, gcol=gcol, grow=grow, bc=bc, br=br, pairs=pairs))
    return chunks


def _mlstm_weights(chunks, v_ref, m_ref, n_state_rows):
    L = M_CHUNK
    src = lax.broadcasted_iota(jnp.int32, (L, L), 0)
    dst = lax.broadcasted_iota(jnp.int32, (L, L), 1)
    causal = src <= dst
    zeros = jnp.zeros((L, L), BF16)
    m_run = [m_ref[h:h + 1, :] for h in range(M_HEADS)]
    for ch in chunks:
        cur, gcol, grow, bc, br = ch["cur"], ch["gcol"], ch["grow"], ch["bc"], ch["br"]
        r_col = gcol - pltpu.roll(bc, LANES - M_HEADS, 1)
        for j, pr in enumerate(ch["pairs"]):
            for hh, hd in enumerate(pr["heads"]):
                h = 2 * j + hh
                vh = v_ref[h * M_V_DIM:(h + 1) * M_V_DIM, cur]
                b_row = br[M_HEADS + h:M_HEADS + h + 1, :]
                li_row = grow[h:h + 1, :]
                m_prev = m_run[h]
                log_d = jnp.where(causal, jnp.broadcast_to(r_col[:, h:h + 1], (L, L)) + b_row, NEG_BIG)
                m_t = jnp.maximum(b_row + m_prev, jnp.max(log_d, axis=0, keepdims=True))
                w_t = jnp.exp2(log_d - m_t) * pr["s_t"][:, hh * L:(hh + 1) * L]
                g = b_row[:, L - 1:L]
                w_end = g - b_row + li_row
                m_loc = jnp.max(w_end, axis=1, keepdims=True)
                e_row = jnp.exp2(w_end - m_loc).astype(BF16)
                m_new = jnp.maximum(g + m_prev, m_loc)
                hd.update(
                    vh=vh, m_t=m_t, w_bf=w_t.astype(BF16), w_sum=jnp.sum(w_t, axis=0, keepdims=True),
                    a_int=jnp.exp2(b_row + m_prev - m_t),
                    lhs=jnp.concatenate([vh * e_row, jnp.broadcast_to(e_row, (n_state_rows - M_V_DIM, L))], axis=0),
                    decay=jnp.exp2(g + m_prev - m_new), gain=jnp.exp2(m_loc - m_new))
                m_run[h] = m_new
            ha, hb = pr["heads"]
            pr["v_cat"] = jnp.concatenate([ha["vh"], hb["vh"]], axis=1)
            pr["w_diag"] = jnp.concatenate([jnp.concatenate([ha["w_bf"], zeros], axis=1),
                                            jnp.concatenate([zeros, hb["w_bf"]], axis=1)], axis=0)
    for h in range(M_HEADS):
        m_ref[h:h + 1, :] = m_run[h]


def _mlstm_intra(chunks):
    for ch in chunks:
        for pr in ch["pairs"]:
            pr["num"] = _dot(pr["v_cat"], pr["w_diag"])
            for hd in pr["heads"]:
                hd["d_state"] = _dot(hd["lhs"], pr["k2"])


def _mlstm_states(chunks, state_ref):
    first_head = lax.broadcasted_iota(jnp.int32, state_ref.shape[1:], 1) < M_QK_DIM
    for j in range(M_HEADS // 2):
        state = state_ref[j]
        for ch in chunks:
            pr = ch["pairs"][j]
            ha, hb = pr["heads"]
            pr["state_in"] = state.astype(BF16)
            state = (jnp.where(first_head, ha["decay"], hb["decay"]) * state
                     + jnp.where(first_head, ha["gain"], hb["gain"]) * jnp.where(first_head, ha["d_state"], hb["d_state"]))
        state_ref[j] = state
    for ch in chunks:
        for pr in ch["pairs"]:
            pr["inter"] = _dot(pr["state_in"], pr["q_pad"])


def _mlstm_outputs(chunks, osig_ref, gain_ref, o_ref):
    L = M_CHUNK
    for ch in chunks:
        cur = ch["cur"]
        for j, pr in enumerate(ch["pairs"]):
            for hh, hd in enumerate(pr["heads"]):
                h = 2 * j + hh
                a_int = hd["a_int"]
                inter = pr["inter"][:, hh * L:(hh + 1) * L]
                num = pr["num"][:, hh * L:(hh + 1) * L] + a_int * inter[:M_V_DIM]
                den = hd["w_sum"] + a_int * inter[M_V_DIM:M_V_DIM + 1]
                d = jnp.maximum(jnp.abs(den), jnp.exp2(-hd["m_t"]))
                scale = lax.rsqrt(jnp.mean(num * num, axis=0, keepdims=True) + EPS * (d * d))
                rows = slice(h * M_V_DIM, (h + 1) * M_V_DIM)
                o_ref[rows, cur] = ((num * scale) * (gain_ref[rows, :] * osig_ref[rows, cur].astype(F32))).astype(BF16)


def _mixers_kernel(sinks_ref, qa_ref, ka_ref, kap_ref, va_ref, vap_ref,
                   qm_ref, km_ref, vm_ref, osig_ref, gcol_ref, grow_ref, gain_ref,
                   att_ref, hm_ref, state_ref, m_ref, *, layer):
    first = pl.program_id(1) == 0

    @pl.when(first)
    def _():
        state_ref[...] = jnp.zeros(state_ref.shape, F32)
        m_ref[...] = jnp.zeros(m_ref.shape, F32)

    units = _attention_units(qa_ref, ka_ref, kap_ref, va_ref, vap_ref)
    scores = [_attention_scores(u, qa_ref) for u in units]
    chunks = _mlstm_gates_and_scores(qm_ref, km_ref, gcol_ref, grow_ref)
    probs = [_attention_softmax(u, s, first, sinks_ref, layer) for u, s in zip(units, scores)]
    _mlstm_weights(chunks, vm_ref, m_ref, state_ref.shape[1])
    for u, (p, r_den) in zip(units, probs):
        _attention_output(u, p, r_den, att_ref)
    _mlstm_intra(chunks)
    _mlstm_states(chunks, state_ref)
    _mlstm_outputs(chunks, osig_ref, gain_ref, hm_ref)


def _mixers(sinks, qa_t, ka, va_t, qm_t, km, vm_t, osig_t, gcol, grow, gain_b, *, layer, seq, tc):
    T = km.shape[0]
    n = seq // tc
    per = tc // ATT_BLOCK
    tile = lambda b, c: b * n + c
    prev_blk = lambda b, c: jnp.maximum(tile(b, c) * per - 1, 0)
    row = lambda w: pl.BlockSpec((tc, w), lambda b, c, *_: (tile(b, c), 0))
    colm = lambda w: pl.BlockSpec((w, tc), lambda b, c, *_: (0, tile(b, c)))
    grid_spec = pltpu.PrefetchScalarGridSpec(
        num_scalar_prefetch=1, grid=(T // seq, n),
        in_specs=[colm(ATT_Q_W), row(ATT_KV_W),
                  pl.BlockSpec((ATT_BLOCK, ATT_KV_W), lambda b, c, *_: (prev_blk(b, c), 0)),
                  colm(ATT_KV_W),
                  pl.BlockSpec((ATT_KV_W, ATT_BLOCK), lambda b, c, *_: (0, prev_blk(b, c))),
                  colm(M_QK_W), row(M_QK_W), colm(M_V_W), colm(M_V_W), row(LANES), colm(2 * M_HEADS),
                  _layer_spec(gain_b, layer)],
        out_specs=(colm(ATT_Q_W), colm(M_V_W)),
        scratch_shapes=[pltpu.VMEM((M_HEADS // 2, M_V_DIM + SUBLANES_BF16, LANES), F32),
                        pltpu.VMEM((2 * M_HEADS, LANES), F32)])
    return pl.pallas_call(
        functools.partial(_mixers_kernel, layer=layer), grid_spec=grid_spec,
        out_shape=(jax.ShapeDtypeStruct((ATT_Q_W, T), BF16), jax.ShapeDtypeStruct((M_V_W, T), BF16)),
        compiler_params=_params("parallel", "arbitrary"), name="mixers",
    )(sinks, qa_t, ka, ka, va_t, va_t, qm_t, km, vm_t, osig_t, gcol, grow, gain_b)


def _post_kernel(x_ref, att_ref, hm_ref, sg_ref, wa_ref, wm_ref, wo_ref, gain_ref, w1_ref, w2_ref, o_ref, *, n_chunks):
    tm, D = x_ref.shape
    subs = [slice(r * POST_SUB, (r + 1) * POST_SUB) for r in range(tm // POST_SUB)]
    branches = [(_dot_tn(att_ref[:, s], wa_ref[...]), _dot_tn(hm_ref[:, s], wm_ref[...])) for s in subs]
    xs = []
    for s, (a, m) in zip(subs, branches):
        mixed = sg_ref[s, :D].astype(F32) * a + sg_ref[s, D:].astype(F32) * m
        xs.append(x_ref[s, :] + _dot(mixed.astype(BF16), wo_ref[...]))
    fc = w1_ref.shape[1] // n_chunks
    for s, x in zip(subs, xs):
        inv = lax.rsqrt(jnp.mean(x * x, axis=-1, keepdims=True) + EPS)
        h = ((x * inv) * gain_ref[...]).astype(BF16)
        acc = x
        for c in range(n_chunks):
            u = jnp.maximum(_dot(h, w1_ref[:, c * fc:(c + 1) * fc]), 0.0)
            acc = acc + _dot((u * u).astype(BF16), w2_ref[c * fc:(c + 1) * fc, :])
        o_ref[s, :] = acc


def _post(x, att_t, hm_t, sg, wa, wm, wo, gain, w1, w2, *, layer, tm):
    T, D = x.shape
    row = lambda w: pl.BlockSpec((tm, w), lambda i: (i, 0))
    colm = lambda w: pl.BlockSpec((w, tm), lambda i: (0, i))
    return pl.pallas_call(
        functools.partial(_post_kernel, n_chunks=4), grid=(T // tm,),
        in_specs=[row(D), colm(ATT_Q_W), colm(M_V_W), row(2 * D)] + [
            _layer_spec(a, layer, single_buffer=True) for a in (wa, wm, wo, gain, w1, w2)],
        out_specs=row(D), out_shape=jax.ShapeDtypeStruct((T, D), F32),
        compiler_params=_params("parallel"), name="post",
    )(x, att_t, hm_t, sg, wa, wm, wo, gain, w1, w2)


def _rope_tables(seq):
    pos = jnp.arange(seq, dtype=F32)
    inv_freq = ROPE_THETA ** (-jnp.arange(0, ROPE_DIM, 2, dtype=F32) / ROPE_DIM)
    ang_t = inv_freq[:, None] * pos[None, :]
    d = jnp.arange(LANES) % ATT_HEAD_DIM
    ang = pos[:, None] * inv_freq[d % ROPE_HALF][None, :]
    cos, sin = jnp.cos(ang), jnp.sin(ang)
    rc = jnp.where(d < ROPE_DIM, cos, 1.0)
    rs1 = jnp.where(d < ROPE_HALF, -sin, 0.0)
    rs2 = jnp.where((d >= ROPE_HALF) & (d < ROPE_DIM), sin, 0.0)
    return jnp.cos(ang_t), jnp.sin(ang_t), rc, rs1, rs2


def _prep_w_in_kernel(w_ref, fm_ref, tm_ref):
    w = w_ref[...]
    seg = lambda i: w[IN_OFFS[i]:IN_OFFS[i + 1], :]
    q_a, k_a, v_a, q_m, k_m, v_m, o_m, if_m, gates = (seg(i) for i in range(len(IN_WIDTHS)))
    fm_ref[...] = jnp.concatenate([0.5 * o_m, q_a, v_a, q_m, v_m], axis=0).astype(BF16)
    if_pad = jnp.zeros((LANES - 2 * M_HEADS, w.shape[1]), F32)
    tm_ref[...] = jnp.concatenate([0.5 * gates, k_a, k_m, if_m, if_pad], axis=0).T.astype(BF16)


def _prep_w_in(w_in):
    depth, D, d_in = w_in.shape
    return pl.pallas_call(
        _prep_w_in_kernel, grid=(depth, D // PREP_COLS),
        in_specs=[pl.BlockSpec((None, d_in, PREP_COLS), lambda l, c: (l, 0, c))],
        out_specs=(pl.BlockSpec((None, FM_OFFS[-1], PREP_COLS), lambda l, c: (l, 0, c)),
                   pl.BlockSpec((None, PREP_COLS, TM_OFFS[-1]), lambda l, c: (l, c, 0))),
        out_shape=(jax.ShapeDtypeStruct((depth, FM_OFFS[-1], D), BF16),
                   jax.ShapeDtypeStruct((depth, D, TM_OFFS[-1]), BF16)),
        compiler_params=_params("parallel", "parallel"), name="prep_w_in",
    )(jnp.swapaxes(w_in, 1, 2))


def kernel(x, norm_mix, w_in, att_q_norm, att_k_norm, att_sinks, m_gate_bias, m_head_norm,
           w_att_branch, w_m_branch, w_out, norm_ffn, w_ff1, w_ff2):
    B, S, D = x.shape
    T = B * S
    depth = w_in.shape[0]
    tm = min(TOKEN_TILE, S)
    cos_t, sin_t, rc, rs1, rs2 = _rope_tables(S)
    w_fm, w_tm = _prep_w_in(w_in)
    wa, wm, wo = w_att_branch.astype(BF16), w_m_branch.astype(BF16), w_out.astype(BF16)
    w1, w2 = w_ff1.astype(BF16), w_ff2.astype(BF16)
    bias = jnp.pad(m_gate_bias, ((0, 0), (0, LANES - 2 * M_HEADS))).reshape(depth, 1, LANES)
    tm_in = min(IN_PROJ_TILE, S)
    qgain = jnp.broadcast_to(att_q_norm[:, :, None], (depth, ATT_HEAD_DIM, tm_in))
    kgain = jnp.tile(att_k_norm, (1, 2)).reshape(depth, 1, LANES)
    mgain = jnp.broadcast_to(m_head_norm[:, :, None], (depth, M_V_W, LANES))
    gain_mix = norm_mix.reshape(depth, 1, D)
    gain_ffn = norm_ffn.reshape(depth, 1, D)
    sinks = att_sinks.reshape(depth * ATT_HEADS)
    xt = x.reshape(T, D)
    for l in range(depth):
        qa_t, va_t, qm_t, vm_t, osig_t, ka, km, sg, gcol, grow = _in_proj(
            xt, gain_mix, w_fm, w_tm, bias, qgain, kgain, cos_t, sin_t, rc, rs1, rs2, layer=l, seq=S, tm=tm_in)
        att_t, hm_t = _mixers(sinks, qa_t, ka, va_t, qm_t, km, vm_t, osig_t, gcol, grow, mgain, layer=l, seq=S, tc=min(MIXERS_TILE, S))
        xt = _post(xt, att_t, hm_t, sg, wa, wm, wo, gain_ffn, w1, w2, layer=l, tm=tm)
    return xt.reshape(B, S, D)
```

```python
import functools

import jax
import jax.numpy as jnp
import numpy as np
from jax import lax
from jax.experimental import pallas as pl
from jax.experimental.pallas import tpu as pltpu

D_MODEL = 1024
ATT_HEADS = 8
ATT_KV_HEADS = 2
ATT_GROUP = ATT_HEADS // ATT_KV_HEADS
ATT_HEAD_DIM = 64
ATT_BLOCK = 128
ROPE_DIM = ATT_HEAD_DIM // 4
ROPE_HALF = ROPE_DIM // 2
ROPE_THETA = 500000.0
M_HEADS = 4
M_QK_DIM = 64
M_V_DIM = 128
M_CHUNK = 128
EPS = 1e-6

ATT_Q_W = ATT_HEADS * ATT_HEAD_DIM
ATT_KV_W = ATT_KV_HEADS * ATT_HEAD_DIM
M_QK_W = M_HEADS * M_QK_DIM
M_V_W = M_HEADS * M_V_DIM
IN_WIDTHS = (ATT_Q_W, ATT_KV_W, ATT_KV_W, M_QK_W, M_QK_W, M_V_W, M_V_W, 2 * M_HEADS, 2 * D_MODEL)
IN_OFFS = tuple(int(o) for o in np.concatenate([[0], np.cumsum(IN_WIDTHS)]))
LANES = 128
SUBLANES_BF16 = 16
FM_WIDTHS = (M_V_W, ATT_Q_W, ATT_KV_W, M_QK_W, M_V_W)
FM_OFFS = tuple(int(o) for o in np.concatenate([[0], np.cumsum(FM_WIDTHS)]))
TM_WIDTHS = (2 * D_MODEL, ATT_KV_W, M_QK_W, LANES)
TM_OFFS = tuple(int(o) for o in np.concatenate([[0], np.cumsum(TM_WIDTHS)]))

NEG_BIG = -1e30
LOG2_E = 1.4426950408889634
VMEM_LIMIT = 44 * 1024 * 1024
TOKEN_TILE = 512
POST_SUB = 256
MIXERS_TILE = 1024
IN_PROJ_TILE = 1024
PREP_COLS = 256

BF16 = jnp.bfloat16
F32 = jnp.float32


def _dot(a, b):
    return jnp.dot(a, b, preferred_element_type=F32)


def _dot_nt(a, b):
    return lax.dot_general(a, b, (((1,), (1,)), ((), ())), preferred_element_type=F32)


def _dot_tn(a, b):
    return lax.dot_general(a, b, (((0,), (0,)), ((), ())), preferred_element_type=F32)


def _sigmoid_of_twice(half_x):
    return 0.5 * jnp.tanh(half_x) + 0.5


def _log_sigmoid(x):
    return jnp.minimum(x, 0.0) - jnp.log(1.0 + jnp.exp(-jnp.abs(x)))


def _layer_spec(arr, layer, single_buffer=False):
    mode = dict(pipeline_mode=pl.Buffered(1)) if single_buffer else {}
    return pl.BlockSpec((None,) + arr.shape[1:], lambda *_: (layer,) + (0,) * (arr.ndim - 1), **mode)


def _params(*sem):
    return pltpu.CompilerParams(dimension_semantics=sem, vmem_limit_bytes=VMEM_LIMIT)


def _in_proj_kernel(x_ref, gain_ref, w_fm_ref, w_tm_ref, bias_ref, qgain_ref, kgain_ref,
                    cos_ref, sin_ref, rc_ref, rs1_ref, rs2_ref,
                    qa_ref, va_ref, qm_ref, vm_ref, om_ref, ka_ref, km_ref, sg_ref, gcol_ref, grow_ref):
    x = x_ref[...]
    inv = lax.rsqrt(jnp.mean(x * x, axis=-1, keepdims=True) + EPS)
    h = ((x * inv) * gain_ref[...]).astype(BF16)
    tm = x.shape[0]

    sg_ref[...] = _sigmoid_of_twice(_dot(h, w_tm_ref[:, TM_OFFS[0]:TM_OFFS[1]])).astype(BF16)
    om_ref[...] = _sigmoid_of_twice(_dot_nt(w_fm_ref[FM_OFFS[0]:FM_OFFS[1], :], h)).astype(BF16)

    zq = _dot_nt(w_fm_ref[FM_OFFS[1]:FM_OFFS[2], :], h)
    cos, sin = cos_ref[...], sin_ref[...]
    qgain = qgain_ref[...] * (ATT_HEAD_DIM ** -0.5 * LOG2_E)
    for hd in range(ATT_HEADS):
        z = zq[hd * ATT_HEAD_DIM:(hd + 1) * ATT_HEAD_DIM, :]
        inv_h = lax.rsqrt(jnp.mean(z * z, axis=0, keepdims=True) + EPS)
        y = (z * inv_h) * qgain
        x1, x2 = y[:ROPE_HALF], y[ROPE_HALF:ROPE_DIM]
        y = jnp.concatenate([x1 * cos - x2 * sin, x2 * cos + x1 * sin, y[ROPE_DIM:]], axis=0)
        qa_ref[hd * ATT_HEAD_DIM:(hd + 1) * ATT_HEAD_DIM, :] = y.astype(BF16)

    zn = _dot(h, w_tm_ref[:, TM_OFFS[1]:TM_OFFS[4]])
    zr = _dot_nt(w_fm_ref[FM_OFFS[2]:FM_OFFS[5], :], h)
    va_ref[...] = zr[:ATT_KV_W, :].astype(BF16)
    qm_ref[...] = (zr[ATT_KV_W:ATT_KV_W + M_QK_W, :] * (M_QK_DIM ** -0.5)).astype(BF16)
    vm_ref[...] = zr[ATT_KV_W + M_QK_W:, :].astype(BF16)

    lane = lax.broadcasted_iota(jnp.int32, (tm, LANES), 1)
    low = lane < ATT_HEAD_DIM
    zk = zn[:, :ATT_KV_W]
    s = zk * zk
    r0 = jnp.sum(jnp.where(low, s, 0.0), axis=-1, keepdims=True)
    r1 = jnp.sum(jnp.where(low, 0.0, s), axis=-1, keepdims=True)
    inv_k = lax.rsqrt(jnp.where(low, r0, r1) * (1.0 / ATT_HEAD_DIM) + EPS)
    y = (zk * inv_k) * kgain_ref[...]
    up = pltpu.roll(y, LANES - ROPE_HALF, 1)
    dn = pltpu.roll(y, ROPE_HALF, 1)
    ka_ref[...] = (y * rc_ref[...] + up * rs1_ref[...] + dn * rs2_ref[...]).astype(BF16)
    km_ref[...] = zn[:, ATT_KV_W:ATT_KV_W + M_QK_W].astype(BF16)
    zi = zn[:, ATT_KV_W + M_QK_W:] + bias_ref[...]
    g = jnp.where(lane < M_HEADS, zi, jnp.where(lane < 2 * M_HEADS, _log_sigmoid(zi), 0.0)) * LOG2_E
    gcol_ref[...] = g
    grow_ref[...] = g.T[:2 * M_HEADS, :]


def _in_proj(x, gain, w_fm, w_tm, bias, qgain, kgain, cos_t, sin_t, rc, rs1, rs2, *, layer, seq, tm):
    T, D = x.shape
    n_pos = seq // tm
    row = lambda w: pl.BlockSpec((tm, w), lambda i: (i, 0))
    colm = lambda w: pl.BlockSpec((w, tm), lambda i: (0, i))
    pos = pl.BlockSpec((tm, LANES), lambda i: (i % n_pos, 0))
    pos_t = pl.BlockSpec((ROPE_HALF, tm), lambda i: (0, i % n_pos))
    fm = lambda w: jax.ShapeDtypeStruct((w, T), BF16)
    out_shapes = (fm(ATT_Q_W), fm(ATT_KV_W), fm(M_QK_W), fm(M_V_W), fm(M_V_W),
                  jax.ShapeDtypeStruct((T, ATT_KV_W), BF16),
                  jax.ShapeDtypeStruct((T, M_QK_W), BF16),
                  jax.ShapeDtypeStruct((T, 2 * D), BF16),
                  jax.ShapeDtypeStruct((T, LANES), F32),
                  jax.ShapeDtypeStruct((2 * M_HEADS, T), F32))
    out_specs = (colm(ATT_Q_W), colm(ATT_KV_W), colm(M_QK_W), colm(M_V_W), colm(M_V_W),
                 row(ATT_KV_W), row(M_QK_W), row(2 * D), row(LANES), colm(2 * M_HEADS))
    in_specs = [row(D)] + [_layer_spec(a, layer, single_buffer=True)
                           for a in (gain, w_fm, w_tm, bias, qgain, kgain)] + [
        pos_t, pos_t, pos, pos, pos]
    return pl.pallas_call(
        _in_proj_kernel, grid=(T // tm,), in_specs=in_specs, out_specs=out_specs, out_shape=out_shapes,
        compiler_params=_params("parallel"), name="in_proj",
    )(x, gain, w_fm, w_tm, bias, qgain, kgain, cos_t, sin_t, rc, rs1, rs2)


def _attention_units(q_ref, k_ref, kp_ref, v_ref, vp_ref):
    blk, hd_dim = ATT_BLOCK, ATT_HEAD_DIM
    units = []
    for b in range(q_ref.shape[1] // blk):
        cur = slice(b * blk, (b + 1) * blk)
        old = slice((b - 1) * blk, b * blk)
        k_cat = jnp.concatenate([kp_ref[...] if b == 0 else k_ref[old, :], k_ref[cur, :]], axis=0)
        v_prev = vp_ref[...] if b == 0 else v_ref[:, old]
        v_cur = v_ref[:, cur]
        for kv in range(ATT_KV_HEADS):
            rows = slice(kv * hd_dim, (kv + 1) * hd_dim)
            v_cat = jnp.concatenate([v_prev[rows, :], v_cur[rows, :]], axis=1)
            units.append((b, cur, kv, k_cat, v_cat))
    return units


def _attention_scores(unit, q_ref):
    b, cur, kv, k_cat, _ = unit
    hd_dim = ATT_HEAD_DIM
    zeros = jnp.zeros((hd_dim, ATT_BLOCK), BF16)
    q_pad = []
    for g in range(ATT_GROUP):
        hd = kv * ATT_GROUP + g
        qh = q_ref[hd * hd_dim:(hd + 1) * hd_dim, cur]
        q_pad.append(jnp.concatenate([qh, zeros] if kv == 0 else [zeros, qh], axis=0))
    return _dot(k_cat, jnp.concatenate(q_pad, axis=1))


def _attention_softmax(unit, s_all, first, sinks_ref, layer):
    b, cur, kv, _, _ = unit
    blk = ATT_BLOCK
    key = lax.broadcasted_iota(jnp.int32, (blk, blk), 0)
    qry = lax.broadcasted_iota(jnp.int32, (blk, blk), 1)
    from_prev = key > qry
    zero_p = jnp.zeros((blk, blk), BF16)
    p_all, r_den = [], []
    for g in range(ATT_GROUP):
        sink = sinks_ref[layer * ATT_HEADS + kv * ATT_GROUP + g] * LOG2_E
        s_prev = s_all[:blk, g * blk:(g + 1) * blk]
        if b == 0:
            s_prev = s_prev + jnp.where(first, NEG_BIG, 0.0)
        s = jnp.where(from_prev, s_prev, s_all[blk:, g * blk:(g + 1) * blk])
        m = jnp.maximum(jnp.max(s, axis=0, keepdims=True), sink)
        p = jnp.exp2(s - m)
        r_den.append(1.0 / (jnp.sum(p, axis=0, keepdims=True) + jnp.exp2(sink - m)))
        pb = p.astype(BF16)
        p_all.append(jnp.concatenate([jnp.where(from_prev, pb, zero_p), jnp.where(from_prev, zero_p, pb)], axis=0))
    return jnp.concatenate(p_all, axis=1), r_den


def _attention_output(unit, probs, r_den, o_ref):
    b, cur, kv, _, v_cat = unit
    blk, hd_dim = ATT_BLOCK, ATT_HEAD_DIM
    o_all = _dot(v_cat, probs)
    for g in range(ATT_GROUP):
        hd = kv * ATT_GROUP + g
        o_ref[hd * hd_dim:(hd + 1) * hd_dim, cur] = (o_all[:, g * blk:(g + 1) * blk] * r_den[g]).astype(BF16)


def _bf16_part(x):
    bits = pltpu.bitcast(x, jnp.uint32) & jnp.uint32(0xFFFF0000)
    return pltpu.bitcast(bits, F32)


def _cumsum_rows(tril, x):
    lane = lax.broadcasted_iota(jnp.int32, x.shape, 1)
    w = 2 * M_HEADS
    h1 = _bf16_part(x)
    r1 = x - h1
    h2 = _bf16_part(r1)
    h3 = r1 - h2
    packed = jnp.where(lane < w, h1, jnp.where(lane < 2 * w, pltpu.roll(h2, w, 1), pltpu.roll(h3, 2 * w, 1)))
    c = _dot(tril, packed.astype(BF16))
    return c + pltpu.roll(c, LANES - w, 1) + pltpu.roll(c, LANES - 2 * w, 1)


def _split3(x):
    h1 = _bf16_part(x)
    r1 = x - h1
    h2 = _bf16_part(r1)
    return h1.astype(BF16), h2.astype(BF16), (r1 - h2).astype(BF16)


def _mlstm_gates_and_scores(q_ref, k_ref, gcol_ref, grow_ref):
    L = M_CHUNK
    src = lax.broadcasted_iota(jnp.int32, (L, L), 0)
    dst = lax.broadcasted_iota(jnp.int32, (L, L), 1)
    tril = jnp.where(dst <= src, 1.0, 0.0).astype(BF16)
    triu = jnp.where(src <= dst, 1.0, 0.0).astype(BF16)
    zeros = jnp.zeros((M_QK_DIM, L), BF16)
    chunks = []
    for c in range(q_ref.shape[1] // L):
        cur = slice(c * L, (c + 1) * L)
        gcol = gcol_ref[cur, :]
        grow = grow_ref[:, cur]
        bc = _cumsum_rows(tril, gcol)
        br = sum(_dot(t, triu) for t in _split3(grow))
        pairs = []
        for j in range(M_HEADS // 2):
            k2 = k_ref[cur, j * LANES:(j + 1) * LANES]
            q2 = q_ref[j * LANES:(j + 1) * LANES, cur]
            q_pad = jnp.concatenate([jnp.concatenate([q2[:M_QK_DIM], zeros], axis=0),
                                     jnp.concatenate([zeros, q2[M_QK_DIM:]], axis=0)], axis=1)
            pairs.append(dict(k2=k2, q_pad=q_pad, s_t=_dot(k2, q_pad), heads=[{}, {}]))
        chunks.append(dict(cur=cur, gcol=gcol, grow=grow, bc=bc, br=br, pairs=pairs))
    return chunks


def _mlstm_weights(chunks, v_ref, m_ref, n_state_rows):
    L = M_CHUNK
    src = lax.broadcasted_iota(jnp.int32, (L, L), 0)
    dst = lax.broadcasted_iota(jnp.int32, (L, L), 1)
    causal = src <= dst
    zeros = jnp.zeros((L, L), BF16)
    m_run = [m_ref[h:h + 1, :] for h in range(M_HEADS)]
    for ch in chunks:
        cur, gcol, grow, bc, br = ch["cur"], ch["gcol"], ch["grow"], ch["bc"], ch["br"]
        r_col = gcol - pltpu.roll(bc, LANES - M_HEADS, 1)
        for j, pr in enumerate(ch["pairs"]):
            for hh, hd in enumerate(pr["heads"]):
                h = 2 * j + hh
                vh = v_ref[h * M_V_DIM:(h + 1) * M_V_DIM, cur]
                b_row = br[M_HEADS + h:M_HEADS + h + 1, :]
                li_row = grow[h:h + 1, :]
                m_prev = m_run[h]
                log_d = jnp.where(causal, jnp.broadcast_to(r_col[:, h:h + 1], (L, L)) + b_row, NEG_BIG)
                m_t = jnp.maximum(b_row + m_prev, jnp.max(log_d, axis=0, keepdims=True))
                w_t = jnp.exp2(log_d - m_t) * pr["s_t"][:, hh * L:(hh + 1) * L]
                g = b_row[:, L - 1:L]
                w_end = g - b_row + li_row
                m_loc = jnp.max(w_end, axis=1, keepdims=True)
                e_row = jnp.exp2(w_end - m_loc).astype(BF16)
                m_new = jnp.maximum(g + m_prev, m_loc)
                hd.update(
                    vh=vh, m_t=m_t, w_bf=w_t.astype(BF16), w_sum=jnp.sum(w_t, axis=0, keepdims=True),
                    a_int=jnp.exp2(b_row + m_prev - m_t),
                    lhs=jnp.concatenate([vh * e_row, jnp.broadcast_to(e_row, (n_state_rows - M_V_DIM, L))], axis=0),
                    decay=jnp.exp2(g + m_prev - m_new), gain=jnp.exp2(m_loc - m_new))
                m_run[h] = m_new
            ha, hb = pr["heads"]
            pr["v_cat"] = jnp.concatenate([ha["vh"], hb["vh"]], axis=1)
            pr["w_diag"] = jnp.concatenate([jnp.concatenate([ha["w_bf"], zeros], axis=1),
                                            jnp.concatenate([zeros, hb["w_bf"]], axis=1)], axis=0)
    for h in range(M_HEADS):
        m_ref[h:h + 1, :] = m_run[h]


def _mlstm_intra(chunks):
    for ch in chunks:
        for pr in ch["pairs"]:
            pr["num"] = _dot(pr["v_cat"], pr["w_diag"])
            for hd in pr["heads"]:
                hd["d_state"] = _dot(hd["lhs"], pr["k2"])


def _mlstm_states(chunks, state_ref):
    first_head = lax.broadcasted_iota(jnp.int32, state_ref.shape[1:], 1) < M_QK_DIM
    for j in range(M_HEADS // 2):
        state = state_ref[j]
        for ch in chunks:
            pr = ch["pairs"][j]
            ha, hb = pr["heads"]
            pr["state_in"] = state.astype(BF16)
            state = (jnp.where(first_head, ha["decay"], hb["decay"]) * state
                     + jnp.where(first_head, ha["gain"], hb["gain"]) * jnp.where(first_head, ha["d_state"], hb["d_state"]))
        state_ref[j] = state
    for ch in chunks:
        for pr in ch["pairs"]:
            pr["inter"] = _dot(pr["state_in"], pr["q_pad"])


def _mlstm_outputs(chunks, osig_ref, gain_ref, o_ref):
    L = M_CHUNK
    for ch in chunks:
        cur = ch["cur"]
        for j, pr in enumerate(ch["pairs"]):
            for hh, hd in enumerate(pr["heads"]):
                h = 2 * j + hh
                a_int = hd["a_int"]
                inter = pr["inter"][:, hh * L:(hh + 1) * L]
                num = pr["num"][:, hh * L:(hh + 1) * L] + a_int * inter[:M_V_DIM]
                den = hd["w_sum"] + a_int * inter[M_V_DIM:M_V_DIM + 1]
                d = jnp.maximum(jnp.abs(den), jnp.exp2(-hd["m_t"]))
                scale = lax.rsqrt(jnp.mean(num * num, axis=0, keepdims=True) + EPS * (d * d))
                rows = slice(h * M_V_DIM, (h + 1) * M_V_DIM)
                o_ref[rows, cur] = ((num * scale) * (gain_ref[rows, :] * osig_ref[rows, cur].astype(F32))).astype(BF16)


def _mixers_kernel(sinks_ref, qa_ref, ka_ref, kap_ref, va_ref, vap_ref,
                   qm_ref, km_ref, vm_ref, osig_ref, gcol_ref, grow_ref, gain_ref,
                   att_ref, hm_ref, state_ref, m_ref, *, layer):
    first = pl.program_id(1) == 0

    @pl.when(first)
    def _():
        state_ref[...] = jnp.zeros(state_ref.shape, F32)
        m_ref[...] = jnp.zeros(m_ref.shape, F32)

    units = _attention_units(qa_ref, ka_ref, kap_ref, va_ref, vap_ref)
    scores = [_attention_scores(u, qa_ref) for u in units]
    chunks = _mlstm_gates_and_scores(qm_ref, km_ref, gcol_ref, grow_ref)
    probs = [_attention_softmax(u, s, first, sinks_ref, layer) for u, s in zip(units, scores)]
    _mlstm_weights(chunks, vm_ref, m_ref, state_ref.shape[1])
    for u, (p, r_den) in zip(units, probs):
        _attention_output(u, p, r_den, att_ref)
    _mlstm_intra(chunks)
    _mlstm_states(chunks, state_ref)
    _mlstm_outputs(chunks, osig_ref, gain_ref, hm_ref)


def _mixers(sinks, qa_t, ka, va_t, qm_t, km, vm_t, osig_t, gcol, grow, gain_b, *, layer, seq, tc):
    T = km.shape[0]
    n = seq // tc
    per = tc // ATT_BLOCK
    tile = lambda b, c: b * n + c
    prev_blk = lambda b, c: jnp.maximum(tile(b, c) * per - 1, 0)
    row = lambda w: pl.BlockSpec((tc, w), lambda b, c, *_: (tile(b, c), 0))
    colm = lambda w: pl.BlockSpec((w, tc), lambda b, c, *_: (0, tile(b, c)))
    grid_spec = pltpu.PrefetchScalarGridSpec(
        num_scalar_prefetch=1, grid=(T // seq, n),
        in_specs=[colm(ATT_Q_W), row(ATT_KV_W),
                  pl.BlockSpec((ATT_BLOCK, ATT_KV_W), lambda b, c, *_: (prev_blk(b, c), 0)),
                  colm(ATT_KV_W),
                  pl.BlockSpec((ATT_KV_W, ATT_BLOCK), lambda b, c, *_: (0, prev_blk(b, c))),
                  colm(M_QK_W), row(M_QK_W), colm(M_V_W), colm(M_V_W), row(LANES), colm(2 * M_HEADS),
                  _layer_spec(gain_b, layer)],
        out_specs=(colm(ATT_Q_W), colm(M_V_W)),
        scratch_shapes=[pltpu.VMEM((M_HEADS // 2, M_V_DIM + SUBLANES_BF16, LANES), F32),
                        pltpu.VMEM((2 * M_HEADS, LANES), F32)])
    return pl.pallas_call(
        functools.partial(_mixers_kernel, layer=layer), grid_spec=grid_spec,
        out_shape=(jax.ShapeDtypeStruct((ATT_Q_W, T), BF16), jax.ShapeDtypeStruct((M_V_W, T), BF16)),
        compiler_params=_params("parallel", "arbitrary"), name="mixers",
    )(sinks, qa_t, ka, ka, va_t, va_t, qm_t, km, vm_t, osig_t, gcol, grow, gain_b)


def _post_kernel(x_ref, att_ref, hm_ref, sg_ref, wa_ref, wm_ref, wo_ref, gain_ref, w1_ref, w2_ref, o_ref, *, n_chunks):
    tm, D = x_ref.shape
    subs = [slice(r * POST_SUB, (r + 1) * POST_SUB) for r in range(tm // POST_SUB)]
    branches = [(_dot_tn(att_ref[:, s], wa_ref[...]), _dot_tn(hm_ref[:, s], wm_ref[...])) for s in subs]
    xs = []
    for s, (a, m) in zip(subs, branches):
        mixed = sg_ref[s, :D].astype(F32) * a + sg_ref[s, D:].astype(F32) * m
        xs.append(x_ref[s, :] + _dot(mixed.astype(BF16), wo_ref[...]))
    fc = w1_ref.shape[1] // n_chunks
    for s, x in zip(subs, xs):
        inv = lax.rsqrt(jnp.mean(x * x, axis=-1, keepdims=True) + EPS)
        h = ((x * inv) * gain_ref[...]).astype(BF16)
        acc = x
        for c in range(n_chunks):
            u = jnp.maximum(_dot(h, w1_ref[:, c * fc:(c + 1) * fc]), 0.0)
            acc = acc + _dot((u * u).astype(BF16), w2_ref[c * fc:(c + 1) * fc, :])
        o_ref[s, :] = acc


def _post(x, att_t, hm_t, sg, wa, wm, wo, gain, w1, w2, *, layer, tm):
    T, D = x.shape
    row = lambda w: pl.BlockSpec((tm, w), lambda i: (i, 0))
    colm = lambda w: pl.BlockSpec((w, tm), lambda i: (0, i))
    return pl.pallas_call(
        functools.partial(_post_kernel, n_chunks=4), grid=(T // tm,),
        in_specs=[row(D), colm(ATT_Q_W), colm(M_V_W), row(2 * D)] + [
            _layer_spec(a, layer, single_buffer=True) for a in (wa, wm, wo, gain, w1, w2)],
        out_specs=row(D), out_shape=jax.ShapeDtypeStruct((T, D), F32),
        compiler_params=_params("parallel"), name="post",
    )(x, att_t, hm_t, sg, wa, wm, wo, gain, w1, w2)


def _rope_tables(seq):
    pos = jnp.arange(seq, dtype=F32)
    inv_freq = ROPE_THETA ** (-jnp.arange(0, ROPE_DIM, 2, dtype=F32) / ROPE_DIM)
    ang_t = inv_freq[:, None] * pos[None, :]
    d = jnp.arange(LANES) % ATT_HEAD_DIM
    ang = pos[:, None] * inv_freq[d % ROPE_HALF][None, :]
    cos, sin = jnp.cos(ang), jnp.sin(ang)
    rc = jnp.where(d < ROPE_DIM, cos, 1.0)
    rs1 = jnp.where(d < ROPE_HALF, -sin, 0.0)
    rs2 = jnp.where((d >= ROPE_HALF) & (d < ROPE_DIM), sin, 0.0)
    return jnp.cos(ang_t), jnp.sin(ang_t), rc, rs1, rs2


def _prep_w_in_kernel(w_ref, fm_ref, tm_ref):
    w = w_ref[...]
    seg = lambda i: w[IN_OFFS[i]:IN_OFFS[i + 1], :]
    q_a, k_a, v_a, q_m, k_m, v_m, o_m, if_m, gates = (seg(i) for i in range(len(IN_WIDTHS)))
    fm_ref[...] = jnp.concatenate([0.5 * o_m, q_a, v_a, q_m, v_m], axis=0).astype(BF16)
    if_pad = jnp.zeros((LANES - 2 * M_HEADS, w.shape[1]), F32)
    tm_ref[...] = jnp.concatenate([0.5 * gates, k_a, k_m, if_m, if_pad], axis=0).T.astype(BF16)


def _prep_w_in(w_in):
    depth, D, d_in = w_in.shape
    return pl.pallas_call(
        _prep_w_in_kernel, grid=(depth, D // PREP_COLS),
        in_specs=[pl.BlockSpec((None, d_in, PREP_COLS), lambda l, c: (l, 0, c))],
        out_specs=(pl.BlockSpec((None, FM_OFFS[-1], PREP_COLS), lambda l, c: (l, 0, c)),
                   pl.BlockSpec((None, PREP_COLS, TM_OFFS[-1]), lambda l, c: (l, c, 0))),
        out_shape=(jax.ShapeDtypeStruct((depth, FM_OFFS[-1], D), BF16),
                   jax.ShapeDtypeStruct((depth, D, TM_OFFS[-1]), BF16)),
        compiler_params=_params("parallel", "parallel"), name="prep_w_in",
    )(jnp.swapaxes(w_in, 1, 2))


def kernel(x, norm_mix, w_in, att_q_norm, att_k_norm, att_sinks, m_gate_bias, m_head_norm,
           w_att_branch, w_m_branch, w_out, norm_ffn, w_ff1, w_ff2):
    B, S, D = x.shape
    T = B * S
    depth = w_in.shape[0]
    tm = min(TOKEN_TILE, S)
    cos_t, sin_t, rc, rs1, rs2 = _rope_tables(S)
    w_fm, w_tm = _prep_w_in(w_in)
    wa, wm, wo = w_att_branch.astype(BF16), w_m_branch.astype(BF16), w_out.astype(BF16)
    w1, w2 = w_ff1.astype(BF16), w_ff2.astype(BF16)
    bias = jnp.pad(m_gate_bias, ((0, 0), (0, LANES - 2 * M_HEADS))).reshape(depth, 1, LANES)
    tm_in = min(IN_PROJ_TILE, S)
    qgain = jnp.broadcast_to(att_q_norm[:, :, None], (depth, ATT_HEAD_DIM, tm_in))
    kgain = jnp.tile(att_k_norm, (1, 2)).reshape(depth, 1, LANES)
    mgain = jnp.broadcast_to(m_head_norm[:, :, None], (depth, M_V_W, LANES))
    gain_mix = norm_mix.reshape(depth, 1, D)
    gain_ffn = norm_ffn.reshape(depth, 1, D)
    sinks = att_sinks.reshape(depth * ATT_HEADS)
    xt = x.reshape(T, D)
    for l in range(depth):
        qa_t, va_t, qm_t, vm_t, osig_t, ka, km, sg, gcol, grow = _in_proj(
            xt, gain_mix, w_fm, w_tm, bias, qgain, kgain, cos_t, sin_t, rc, rs1, rs2, layer=l, seq=S, tm=tm_in)
        att_t, hm_t = _mixers(sinks, qa_t, ka, va_t, qm_t, km, vm_t, osig_t, gcol, grow, mgain, layer=l, seq=S, tc=min(MIXERS_TILE, S))
        xt = _post(xt, att_t, hm_t, sg, wa, wm, wo, gain_ffn, w1, w2, layer=l, tm=tm)
    return xt.reshape(B, S, D)
```

```python
import functools

import jax
import jax.numpy as jnp
import numpy as np
from jax import lax
from jax.experimental import pallas as pl
from jax.experimental.pallas import tpu as pltpu

D_MODEL = 1024
ATT_HEADS = 8
ATT_KV_HEADS = 2
ATT_GROUP = ATT_HEADS // ATT_KV_HEADS
ATT_HEAD_DIM = 64
ATT_BLOCK = 128
ROPE_DIM = ATT_HEAD_DIM // 4
ROPE_HALF = ROPE_DIM // 2
ROPE_THETA = 500000.0
M_HEADS = 4
M_QK_DIM = 64
M_V_DIM = 128
M_CHUNK = 128
EPS = 1e-6

ATT_Q_W = ATT_HEADS * ATT_HEAD_DIM
ATT_KV_W = ATT_KV_HEADS * ATT_HEAD_DIM
M_QK_W = M_HEADS * M_QK_DIM
M_V_W = M_HEADS * M_V_DIM
IN_WIDTHS = (ATT_Q_W, ATT_KV_W, ATT_KV_W, M_QK_W, M_QK_W, M_V_W, M_V_W, 2 * M_HEADS, 2 * D_MODEL)
IN_OFFS = tuple(int(o) for o in np.concatenate([[0], np.cumsum(IN_WIDTHS)]))
LANES = 128
SUBLANES_BF16 = 16
FM_WIDTHS = (M_V_W, ATT_Q_W, ATT_KV_W, M_QK_W, M_V_W)
FM_OFFS = tuple(int(o) for o in np.concatenate([[0], np.cumsum(FM_WIDTHS)]))
TM_WIDTHS = (2 * D_MODEL, ATT_KV_W, M_QK_W, LANES)
TM_OFFS = tuple(int(o) for o in np.concatenate([[0], np.cumsum(TM_WIDTHS)]))

NEG_BIG = -1e30
LOG2_E = 1.4426950408889634
VMEM_LIMIT = 56 * 1024 * 1024
TOKEN_TILE = 512
POST_SUB = 256
STAGE_ROWS_D = 512
STAGE_ROWS_F = 128
MIXERS_TILE = 1024
IN_PROJ_TILE = 1024
PREP_COLS = 256

BF16 = jnp.bfloat16
F32 = jnp.float32


def _dot(a, b):
    return jnp.dot(a, b, preferred_element_type=F32)


def _dot_nt(a, b):
    return lax.dot_general(a, b, (((1,), (1,)), ((), ())), preferred_element_type=F32)


def _dot_tn(a, b):
    return lax.dot_general(a, b, (((0,), (0,)), ((), ())), preferred_element_type=F32)


def _sigmoid_of_twice(half_x):
    return 0.5 * jnp.tanh(half_x) + 0.5


def _log_sigmoid(x):
    return jnp.minimum(x, 0.0) - jnp.log(1.0 + jnp.exp(-jnp.abs(x)))


def _layer_spec(arr, layer, single_buffer=False):
    mode = dict(pipeline_mode=pl.Buffered(1)) if single_buffer else {}
    return pl.BlockSpec((None,) + arr.shape[1:], lambda *_: (layer,) + (0,) * (arr.ndim - 1), **mode)


def _params(*sem):
    return pltpu.CompilerParams(dimension_semantics=sem, vmem_limit_bytes=VMEM_LIMIT)


def _in_proj_kernel(x_ref, gain_ref, w_fm_ref, w_tm_ref, bias_ref, qgain_ref, kgain_ref,
                    cos_ref, sin_ref, rc_ref, rs1_ref, rs2_ref,
                    qa_ref, va_ref, qm_ref, vm_ref, om_ref, ka_ref, km_ref, sg_ref, gcol_ref, grow_ref):
    x = x_ref[...]
    inv = lax.rsqrt(jnp.mean(x * x, axis=-1, keepdims=True) + EPS)
    h = ((x * inv) * gain_ref[...]).astype(BF16)
    tm = x.shape[0]

    sg_ref[...] = _sigmoid_of_twice(_dot(h, w_tm_ref[:, TM_OFFS[0]:TM_OFFS[1]])).astype(BF16)
    om_ref[...] = _sigmoid_of_twice(_dot_nt(w_fm_ref[FM_OFFS[0]:FM_OFFS[1], :], h)).astype(BF16)

    zq = _dot_nt(w_fm_ref[FM_OFFS[1]:FM_OFFS[2], :], h)
    cos, sin = cos_ref[...], sin_ref[...]
    qgain = qgain_ref[...] * (ATT_HEAD_DIM ** -0.5 * LOG2_E)
    for hd in range(ATT_HEADS):
        z = zq[hd * ATT_HEAD_DIM:(hd + 1) * ATT_HEAD_DIM, :]
        inv_h = lax.rsqrt(jnp.mean(z * z, axis=0, keepdims=True) + EPS)
        y = (z * inv_h) * qgain
        x1, x2 = y[:ROPE_HALF], y[ROPE_HALF:ROPE_DIM]
        y = jnp.concatenate([x1 * cos - x2 * sin, x2 * cos + x1 * sin, y[ROPE_DIM:]], axis=0)
        qa_ref[hd * ATT_HEAD_DIM:(hd + 1) * ATT_HEAD_DIM, :] = y.astype(BF16)

    zn = _dot(h, w_tm_ref[:, TM_OFFS[1]:TM_OFFS[4]])
    zr = _dot_nt(w_fm_ref[FM_OFFS[2]:FM_OFFS[5], :], h)
    va_ref[...] = zr[:ATT_KV_W, :].astype(BF16)
    qm_ref[...] = (zr[ATT_KV_W:ATT_KV_W + M_QK_W, :] * (M_QK_DIM ** -0.5)).astype(BF16)
    vm_ref[...] = zr[ATT_KV_W + M_QK_W:, :].astype(BF16)

    lane = lax.broadcasted_iota(jnp.int32, (tm, LANES), 1)
    low = lane < ATT_HEAD_DIM
    zk = zn[:, :ATT_KV_W]
    s = zk * zk
    r0 = jnp.sum(jnp.where(low, s, 0.0), axis=-1, keepdims=True)
    r1 = jnp.sum(jnp.where(low, 0.0, s), axis=-1, keepdims=True)
    inv_k = lax.rsqrt(jnp.where(low, r0, r1) * (1.0 / ATT_HEAD_DIM) + EPS)
    y = (zk * inv_k) * kgain_ref[...]
    up = pltpu.roll(y, LANES - ROPE_HALF, 1)
    dn = pltpu.roll(y, ROPE_HALF, 1)
    ka_ref[...] = (y * rc_ref[...] + up * rs1_ref[...] + dn * rs2_ref[...]).astype(BF16)
    km_ref[...] = zn[:, ATT_KV_W:ATT_KV_W + M_QK_W].astype(BF16)
    zi = zn[:, ATT_KV_W + M_QK_W:] + bias_ref[...]
    g = jnp.where(lane < M_HEADS, zi, jnp.where(lane < 2 * M_HEADS, _log_sigmoid(zi), 0.0)) * LOG2_E
    gcol_ref[...] = g
    grow_ref[...] = g.T[:2 * M_HEADS, :]


def _in_proj(x, gain, w_fm, w_tm, bias, qgain, kgain, cos_t, sin_t, rc, rs1, rs2, *, layer, seq, tm):
    T, D = x.shape
    n_pos = seq // tm
    row = lambda w: pl.BlockSpec((tm, w), lambda i: (i, 0))
    colm = lambda w: pl.BlockSpec((w, tm), lambda i: (0, i))
    pos = pl.BlockSpec((tm, LANES), lambda i: (i % n_pos, 0))
    pos_t = pl.BlockSpec((ROPE_HALF, tm), lambda i: (0, i % n_pos))
    fm = lambda w: jax.ShapeDtypeStruct((w, T), BF16)
    out_shapes = (fm(ATT_Q_W), fm(ATT_KV_W), fm(M_QK_W), fm(M_V_W), fm(M_V_W),
                  jax.ShapeDtypeStruct((T, ATT_KV_W), BF16),
                  jax.ShapeDtypeStruct((T, M_QK_W), BF16),
                  jax.ShapeDtypeStruct((T, 2 * D), BF16),
                  jax.ShapeDtypeStruct((T, LANES), F32),
                  jax.ShapeDtypeStruct((2 * M_HEADS, T), F32))
    out_specs = (colm(ATT_Q_W), colm(ATT_KV_W), colm(M_QK_W), colm(M_V_W), colm(M_V_W),
                 row(ATT_KV_W), row(M_QK_W), row(2 * D), row(LANES), colm(2 * M_HEADS))
    in_specs = [row(D)] + [_layer_spec(a, layer, single_buffer=True)
                           for a in (gain, w_fm, w_tm, bias, qgain, kgain)] + [
        pos_t, pos_t, pos, pos, pos]
    return pl.pallas_call(
        _in_proj_kernel, grid=(T // tm,), in_specs=in_specs, out_specs=out_specs, out_shape=out_shapes,
        compiler_params=_params("parallel"), name="in_proj",
    )(x, gain, w_fm, w_tm, bias, qgain, kgain, cos_t, sin_t, rc, rs1, rs2)


def _attention_units(q_ref, k_ref, kp_ref, v_ref, vp_ref):
    blk, hd_dim = ATT_BLOCK, ATT_HEAD_DIM
    units = []
    for b in range(q_ref.shape[1] // blk):
        cur = slice(b * blk, (b + 1) * blk)
        old = slice((b - 1) * blk, b * blk)
        k_cat = jnp.concatenate([kp_ref[...] if b == 0 else k_ref[old, :], k_ref[cur, :]], axis=0)
        v_prev = vp_ref[...] if b == 0 else v_ref[:, old]
        v_cur = v_ref[:, cur]
        for kv in range(ATT_KV_HEADS):
            rows = slice(kv * hd_dim, (kv + 1) * hd_dim)
            v_cat = jnp.concatenate([v_prev[rows, :], v_cur[rows, :]], axis=1)
            units.append((b, cur, kv, k_cat, v_cat))
    return units


def _attention_scores(unit, q_ref):
    b, cur, kv, k_cat, _ = unit
    hd_dim = ATT_HEAD_DIM
    zeros = jnp.zeros((hd_dim, ATT_BLOCK), BF16)
    q_pad = []
    for g in range(ATT_GROUP):
        hd = kv * ATT_GROUP + g
        qh = q_ref[hd * hd_dim:(hd + 1) * hd_dim, cur]
        q_pad.append(jnp.concatenate([qh, zeros] if kv == 0 else [zeros, qh], axis=0))
    return _dot(k_cat, jnp.concatenate(q_pad, axis=1))


def _attention_softmax(unit, s_all, first, sinks_ref, layer):
    b, cur, kv, _, _ = unit
    blk = ATT_BLOCK
    key = lax.broadcasted_iota(jnp.int32, (blk, blk), 0)
    qry = lax.broadcasted_iota(jnp.int32, (blk, blk), 1)
    from_prev = key > qry
    zero_p = jnp.zeros((blk, blk), BF16)
    p_all, r_den = [], []
    for g in range(ATT_GROUP):
        sink = sinks_ref[layer * ATT_HEADS + kv * ATT_GROUP + g] * LOG2_E
        s_prev = s_all[:blk, g * blk:(g + 1) * blk]
        if b == 0:
            s_prev = s_prev + jnp.where(first, NEG_BIG, 0.0)
        s = jnp.where(from_prev, s_prev, s_all[blk:, g * blk:(g + 1) * blk])
        m = jnp.maximum(jnp.max(s, axis=0, keepdims=True), sink)
        p = jnp.exp2(s - m)
        r_den.append(1.0 / (jnp.sum(p, axis=0, keepdims=True) + jnp.exp2(sink - m)))
        pb = p.astype(BF16)
        p_all.append(jnp.concatenate([jnp.where(from_prev, pb, zero_p), jnp.where(from_prev, zero_p, pb)], axis=0))
    return jnp.concatenate(p_all, axis=1), r_den


def _attention_output(unit, probs, r_den, o_ref):
    b, cur, kv, _, v_cat = unit
    blk, hd_dim = ATT_BLOCK, ATT_HEAD_DIM
    o_all = _dot(v_cat, probs)
    for g in range(ATT_GROUP):
        hd = kv * ATT_GROUP + g
        o_ref[hd * hd_dim:(hd + 1) * hd_dim, cur] = (o_all[:, g * blk:(g + 1) * blk] * r_den[g]).astype(BF16)


def _bf16_part(x):
    bits = pltpu.bitcast(x, jnp.uint32) & jnp.uint32(0xFFFF0000)
    return pltpu.bitcast(bits, F32)


def _cumsum_rows(tril, x):
    lane = lax.broadcasted_iota(jnp.int32, x.shape, 1)
    w = 2 * M_HEADS
    h1 = _bf16_part(x)
    r1 = x - h1
    h2 = _bf16_part(r1)
    h3 = r1 - h2
    packed = jnp.where(lane < w, h1, jnp.where(lane < 2 * w, pltpu.roll(h2, w, 1), pltpu.roll(h3, 2 * w, 1)))
    c = _dot(tril, packed.astype(BF16))
    return c + pltpu.roll(c, LANES - w, 1) + pltpu.roll(c, LANES - 2 * w, 1)


def _split3(x):
    h1 = _bf16_part(x)
    r1 = x - h1
    h2 = _bf16_part(r1)
    return h1.astype(BF16), h2.astype(BF16), (r1 - h2).astype(BF16)


def _mlstm_gates_and_scores(q_ref, k_ref, gcol_ref, grow_ref):
    L = M_CHUNK
    src = lax.broadcasted_iota(jnp.int32, (L, L), 0)
    dst = lax.broadcasted_iota(jnp.int32, (L, L), 1)
    tril = jnp.where(dst <= src, 1.0, 0.0).astype(BF16)
    triu = jnp.where(src <= dst, 1.0, 0.0).astype(BF16)
    zeros = jnp.zeros((M_QK_DIM, L), BF16)
    chunks = []
    for c in range(q_ref.shape[1] // L):
        cur = slice(c * L, (c + 1) * L)
        gcol = gcol_ref[cur, :]
        grow = grow_ref[:, cur]
        bc = _cumsum_rows(tril, gcol)
        br = sum(_dot(t, triu) for t in _split3(grow))
        pairs = []
        for j in range(M_HEADS // 2):
            k2 = k_ref[cur, j * LANES:(j + 1) * LANES]
            q2 = q_ref[j * LANES:(j + 1) * LANES, cur]
            q_pad = jnp.concatenate([jnp.concatenate([q2[:M_QK_DIM], zeros], axis=0),
                                     jnp.concatenate([zeros, q2[M_QK_DIM:]], axis=0)], axis=1)
            pairs.append(dict(k2=k2, q_pad=q_pad, s_t=_dot(k2, q_pad), heads=[{}, {}]))
        chunks.append(dict(cur=cur, gcol=gcol, grow=grow, bc=bc, br=br, pairs=pairs))
    return chunks


def _mlstm_weights(chunks, v_ref, m_ref, n_state_rows):
    L = M_CHUNK
    src = lax.broadcasted_iota(jnp.int32, (L, L), 0)
    dst = lax.broadcasted_iota(jnp.int32, (L, L), 1)
    causal = src <= dst
    zeros = jnp.zeros((L, L), BF16)
    m_run = [m_ref[h:h + 1, :] for h in range(M_HEADS)]
    for ch in chunks:
        cur, gcol, grow, bc, br = ch["cur"], ch["gcol"], ch["grow"], ch["bc"], ch["br"]
        r_col = gcol - pltpu.roll(bc, LANES - M_HEADS, 1)
        for j, pr in enumerate(ch["pairs"]):
            for hh, hd in enumerate(pr["heads"]):
                h = 2 * j + hh
                vh = v_ref[h * M_V_DIM:(h + 1) * M_V_DIM, cur]
                b_row = br[M_HEADS + h:M_HEADS + h + 1, :]
                li_row = grow[h:h + 1, :]
                m_prev = m_run[h]
                log_d = jnp.where(causal, jnp.broadcast_to(r_col[:, h:h + 1], (L, L)) + b_row, NEG_BIG)
                m_t = jnp.maximum(b_row + m_prev, jnp.max(log_d, axis=0, keepdims=True))
                w_t = jnp.exp2(log_d - m_t) * pr["s_t"][:, hh * L:(hh + 1) * L]
                g = b_row[:, L - 1:L]
                w_end = g - b_row + li_row
                m_loc = jnp.max(w_end, axis=1, keepdims=True)
                e_row = jnp.exp2(w_end - m_loc).astype(BF16)
                m_new = jnp.maximum(g + m_prev, m_loc)
                hd.update(
                    vh=vh, m_t=m_t, w_bf=w_t.astype(BF16), w_sum=jnp.sum(w_t, axis=0, keepdims=True),
                    a_int=jnp.exp2(b_row + m_prev - m_t),
                    lhs=jnp.concatenate([vh * e_row, jnp.broadcast_to(e_row, (n_state_rows - M_V_DIM, L))], axis=0),
                    decay=jnp.exp2(g + m_prev - m_new), gain=jnp.exp2(m_loc - m_new))
                m_run[h] = m_new
            ha, hb = pr["heads"]
            pr["v_cat"] = jnp.concatenate([ha["vh"], hb["vh"]], axis=1)
            pr["w_diag"] = jnp.concatenate([jnp.concatenate([ha["w_bf"], zeros], axis=1),
                                            jnp.concatenate([zeros, hb["w_bf"]], axis=1)], axis=0)
    for h in range(M_HEADS):
        m_ref[h:h + 1, :] = m_run[h]


def _mlstm_intra(chunks):
    for ch in chunks:
        for pr in ch["pairs"]:
            pr["num"] = _dot(pr["v_cat"], pr["w_diag"])
            for hd in pr["heads"]:
                hd["d_state"] = _dot(hd["lhs"], pr["k2"])


def _mlstm_states(chunks, state_ref):
    first_head = lax.broadcasted_iota(jnp.int32, state_ref.shape[1:], 1) < M_QK_DIM
    for j in range(M_HEADS // 2):
        state = state_ref[j]
        for ch in chunks:
            pr = ch["pairs"][j]
            ha, hb = pr["heads"]
            pr["state_in"] = state.astype(BF16)
            state = (jnp.where(first_head, ha["decay"], hb["decay"]) * state
                     + jnp.where(first_head, ha["gain"], hb["gain"]) * jnp.where(first_head, ha["d_state"], hb["d_state"]))
        state_ref[j] = state
    for ch in chunks:
        for pr in ch["pairs"]:
            pr["inter"] = _dot(pr["state_in"], pr["q_pad"])


def _mlstm_outputs(chunks, osig_ref, gain_ref, o_ref):
    L = M_CHUNK
    for ch in chunks:
        cur = ch["cur"]
        for j, pr in enumerate(ch["pairs"]):
            for hh, hd in enumerate(pr["heads"]):
                h = 2 * j + hh
                a_int = hd["a_int"]
                inter = pr["inter"][:, hh * L:(hh + 1) * L]
                num = pr["num"][:, hh * L:(hh + 1) * L] + a_int * inter[:M_V_DIM]
                den = hd["w_sum"] + a_int * inter[M_V_DIM:M_V_DIM + 1]
                d = jnp.maximum(jnp.abs(den), jnp.exp2(-hd["m_t"]))
                scale = lax.rsqrt(jnp.mean(num * num, axis=0, keepdims=True) + EPS * (d * d))
                rows = slice(h * M_V_DIM, (h + 1) * M_V_DIM)
                o_ref[rows, cur] = ((num * scale) * (gain_ref[rows, :] * osig_ref[rows, cur].astype(F32))).astype(BF16)


def _mixers_kernel(sinks_ref, qa_ref, ka_ref, kap_ref, va_ref, vap_ref,
                   qm_ref, km_ref, vm_ref, osig_ref, gcol_ref, grow_ref, gain_ref,
                   att_ref, hm_ref, state_ref, m_ref, *, layer):
    first = pl.program_id(1) == 0

    @pl.when(first)
    def _():
        state_ref[...] = jnp.zeros(state_ref.shape, F32)
        m_ref[...] = jnp.zeros(m_ref.shape, F32)

    units = _attention_units(qa_ref, ka_ref, kap_ref, va_ref, vap_ref)
    scores = [_attention_scores(u, qa_ref) for u in units]
    chunks = _mlstm_gates_and_scores(qm_ref, km_ref, gcol_ref, grow_ref)
    probs = [_attention_softmax(u, s, first, sinks_ref, layer) for u, s in zip(units, scores)]
    _mlstm_weights(chunks, vm_ref, m_ref, state_ref.shape[1])
    for u, (p, r_den) in zip(units, probs):
        _attention_output(u, p, r_den, att_ref)
    _mlstm_intra(chunks)
    _mlstm_states(chunks, state_ref)
    _mlstm_outputs(chunks, osig_ref, gain_ref, hm_ref)


def _mixers(sinks, qa_t, ka, va_t, qm_t, km, vm_t, osig_t, gcol, grow, gain_b, *, layer, seq, tc):
    T = km.shape[0]
    n = seq // tc
    per = tc // ATT_BLOCK
    tile = lambda b, c: b * n + c
    prev_blk = lambda b, c: jnp.maximum(tile(b, c) * per - 1, 0)
    row = lambda w: pl.BlockSpec((tc, w), lambda b, c, *_: (tile(b, c), 0))
    colm = lambda w: pl.BlockSpec((w, tc), lambda b, c, *_: (0, tile(b, c)))
    grid_spec = pltpu.PrefetchScalarGridSpec(
        num_scalar_prefetch=1, grid=(T // seq, n),
        in_specs=[colm(ATT_Q_W), row(ATT_KV_W),
                  pl.BlockSpec((ATT_BLOCK, ATT_KV_W), lambda b, c, *_: (prev_blk(b, c), 0)),
                  colm(ATT_KV_W),
                  pl.BlockSpec((ATT_KV_W, ATT_BLOCK), lambda b, c, *_: (0, prev_blk(b, c))),
                  colm(M_QK_W), row(M_QK_W), colm(M_V_W), colm(M_V_W), row(LANES), colm(2 * M_HEADS),
                  _layer_spec(gain_b, layer)],
        out_specs=(colm(ATT_Q_W), colm(M_V_W)),
        scratch_shapes=[pltpu.VMEM((M_HEADS // 2, M_V_DIM + SUBLANES_BF16, LANES), F32),
                        pltpu.VMEM((2 * M_HEADS, LANES), F32)])
    return pl.pallas_call(
        functools.partial(_mixers_kernel, layer=layer), grid_spec=grid_spec,
        out_shape=(jax.ShapeDtypeStruct((ATT_Q_W, T), BF16), jax.ShapeDtypeStruct((M_V_W, T), BF16)),
        compiler_params=_params("parallel", "arbitrary"), name="mixers",
    )(sinks, qa_t, ka, ka, va_t, va_t, qm_t, km, vm_t, osig_t, gcol, grow, gain_b)


def _stage_weights(layer, pairs, stages, sems):
    jobs, uses = [], [0] * len(stages)
    for src, dst in pairs:
        sid = [i for i, st in enumerate(stages) if st.shape[2] == dst.shape[1]][0]
        rows = stages[sid].shape[1]
        for r0 in range(0, dst.shape[0], rows):
            slot = uses[sid] % 2
            uses[sid] += 1
            copy = pltpu.make_async_copy(src.at[layer, pl.ds(r0, rows), :], stages[sid].at[slot],
                                         sems.at[2 * sid + slot])
            jobs.append((copy, stages[sid], slot, dst, r0, rows))
    jobs[0][0].start()
    for k, (copy, stage, slot, dst, r0, rows) in enumerate(jobs):
        if k + 1 < len(jobs):
            jobs[k + 1][0].start()
        copy.wait()
        dst[r0:r0 + rows, :] = stage[slot].astype(BF16)


def _post_kernel(x_ref, att_ref, hm_ref, sg_ref, gain_ref, wa_hbm, wm_hbm, wo_hbm, w1_hbm, w2_hbm, o_ref,
                 wa_ref, wm_ref, wo_ref, w1_ref, w2_ref, stage_d_ref, stage_f_ref, sems, *, layer, n_chunks):
    @pl.when(pl.program_id(0) == 0)
    def _():
        _stage_weights(layer, [(wa_hbm, wa_ref), (wm_hbm, wm_ref), (wo_hbm, wo_ref), (w2_hbm, w2_ref),
                               (w1_hbm, w1_ref)], [stage_d_ref, stage_f_ref], sems)

    tm, D = x_ref.shape
    subs = [slice(r * POST_SUB, (r + 1) * POST_SUB) for r in range(tm // POST_SUB)]
    branches = [(_dot_tn(att_ref[:, s], wa_ref[...]), _dot_tn(hm_ref[:, s], wm_ref[...])) for s in subs]
    xs = []
    for s, (a, m) in zip(subs, branches):
        mixed = sg_ref[s, :D].astype(F32) * a + sg_ref[s, D:].astype(F32) * m
        xs.append(x_ref[s, :] + _dot(mixed.astype(BF16), wo_ref[...]))
    fc = w1_ref.shape[1] // n_chunks
    for s, x in zip(subs, xs):
        inv = lax.rsqrt(jnp.mean(x * x, axis=-1, keepdims=True) + EPS)
        h = ((x * inv) * gain_ref[...]).astype(BF16)
        acc = x
        for c in range(n_chunks):
            u = jnp.maximum(_dot(h, w1_ref[:, c * fc:(c + 1) * fc]), 0.0)
            acc = acc + _dot((u * u).astype(BF16), w2_ref[c * fc:(c + 1) * fc, :])
        o_ref[s, :] = acc


def _post(x, att_t, hm_t, sg, wa, wm, wo, gain, w1, w2, *, layer, tm):
    T, D = x.shape
    row = lambda w: pl.BlockSpec((tm, w), lambda i: (i, 0))
    colm = lambda w: pl.BlockSpec((w, tm), lambda i: (0, i))
    weights = (wa, wm, wo, w1, w2)
    hbm = pl.BlockSpec(memory_space=pl.ANY)
    return pl.pallas_call(
        functools.partial(_post_kernel, layer=layer, n_chunks=4), grid=(T // tm,),
        in_specs=[row(D), colm(ATT_Q_W), colm(M_V_W), row(2 * D), _layer_spec(gain, layer, single_buffer=True)]
        + [hbm] * len(weights),
        out_specs=row(D), out_shape=jax.ShapeDtypeStruct((T, D), F32),
        scratch_shapes=[pltpu.VMEM(w.shape[1:], BF16) for w in weights] + [
            pltpu.VMEM((2, STAGE_ROWS_D, D), F32), pltpu.VMEM((2, STAGE_ROWS_F, w1.shape[2]), F32),
            pltpu.SemaphoreType.DMA((4,))],
        compiler_params=_params("arbitrary"), name="post",
    )(x, att_t, hm_t, sg, gain, wa, wm, wo, w1, w2)


def _rope_tables(seq):
    pos = jnp.arange(seq, dtype=F32)
    inv_freq = ROPE_THETA ** (-jnp.arange(0, ROPE_DIM, 2, dtype=F32) / ROPE_DIM)
    ang_t = inv_freq[:, None] * pos[None, :]
    d = jnp.arange(LANES) % ATT_HEAD_DIM
    ang = pos[:, None] * inv_freq[d % ROPE_HALF][None, :]
    cos, sin = jnp.cos(ang), jnp.sin(ang)
    rc = jnp.where(d < ROPE_DIM, cos, 1.0)
    rs1 = jnp.where(d < ROPE_HALF, -sin, 0.0)
    rs2 = jnp.where((d >= ROPE_HALF) & (d < ROPE_DIM), sin, 0.0)
    return jnp.cos(ang_t), jnp.sin(ang_t), rc, rs1, rs2


def _prep_w_in_kernel(w_ref, fm_ref, tm_ref):
    w = w_ref[...]
    seg = lambda i: w[IN_OFFS[i]:IN_OFFS[i + 1], :]
    q_a, k_a, v_a, q_m, k_m, v_m, o_m, if_m, gates = (seg(i) for i in range(len(IN_WIDTHS)))
    fm_ref[...] = jnp.concatenate([0.5 * o_m, q_a, v_a, q_m, v_m], axis=0).astype(BF16)
    if_pad = jnp.zeros((LANES - 2 * M_HEADS, w.shape[1]), F32)
    tm_ref[...] = jnp.concatenate([0.5 * gates, k_a, k_m, if_m, if_pad], axis=0).T.astype(BF16)


def _prep_w_in(w_in):
    depth, D, d_in = w_in.shape
    return pl.pallas_call(
        _prep_w_in_kernel, grid=(depth, D // PREP_COLS),
        in_specs=[pl.BlockSpec((None, d_in, PREP_COLS), lambda l, c: (l, 0, c))],
        out_specs=(pl.BlockSpec((None, FM_OFFS[-1], PREP_COLS), lambda l, c: (l, 0, c)),
                   pl.BlockSpec((None, PREP_COLS, TM_OFFS[-1]), lambda l, c: (l, c, 0))),
        out_shape=(jax.ShapeDtypeStruct((depth, FM_OFFS[-1], D), BF16),
                   jax.ShapeDtypeStruct((depth, D, TM_OFFS[-1]), BF16)),
        compiler_params=_params("parallel", "parallel"), name="prep_w_in",
    )(jnp.swapaxes(w_in, 1, 2))


def kernel(x, norm_mix, w_in, att_q_norm, att_k_norm, att_sinks, m_gate_bias, m_head_norm,
           w_att_branch, w_m_branch, w_out, norm_ffn, w_ff1, w_ff2):
    B, S, D = x.shape
    T = B * S
    depth = w_in.shape[0]
    tm = min(TOKEN_TILE, S)
    cos_t, sin_t, rc, rs1, rs2 = _rope_tables(S)
    w_fm, w_tm = _prep_w_in(w_in)
    wa, wm, wo, w1, w2 = w_att_branch, w_m_branch, w_out, w_ff1, w_ff2
    bias = jnp.pad(m_gate_bias, ((0, 0), (0, LANES - 2 * M_HEADS))).reshape(depth, 1, LANES)
    tm_in = min(IN_PROJ_TILE, S)
    qgain = jnp.broadcast_to(att_q_norm[:, :, None], (depth, ATT_HEAD_DIM, tm_in))
    kgain = jnp.tile(att_k_norm, (1, 2)).reshape(depth, 1, LANES)
    mgain = jnp.broadcast_to(m_head_norm[:, :, None], (depth, M_V_W, LANES))
    gain_mix = norm_mix.reshape(depth, 1, D)
    gain_ffn = norm_ffn.reshape(depth, 1, D)
    sinks = att_sinks.reshape(depth * ATT_HEADS)
    xt = x.reshape(T, D)
    for l in range(depth):
        qa_t, va_t, qm_t, vm_t, osig_t, ka, km, sg, gcol, grow = _in_proj(
            xt, gain_mix, w_fm, w_tm, bias, qgain, kgain, cos_t, sin_t, rc, rs1, rs2, layer=l, seq=S, tm=tm_in)
        att_t, hm_t = _mixers(sinks, qa_t, ka, va_t, qm_t, km, vm_t, osig_t, gcol, grow, mgain, layer=l, seq=S, tc=min(MIXERS_TILE, S))
        xt = _post(xt, att_t, hm_t, sg, wa, wm, wo, gain_ffn, w1, w2, layer=l, tm=tm)
    return xt.reshape(B, S, D)
```

```python
import functools

import jax
import jax.numpy as jnp
import numpy as np
from jax import lax
from jax.experimental import pallas as pl
from jax.experimental.pallas import tpu as pltpu

D_MODEL = 1024
ATT_HEADS = 8
ATT_KV_HEADS = 2
ATT_GROUP = ATT_HEADS // ATT_KV_HEADS
ATT_HEAD_DIM = 64
ATT_BLOCK = 128
ROPE_DIM = ATT_HEAD_DIM // 4
ROPE_HALF = ROPE_DIM // 2
ROPE_THETA = 500000.0
M_HEADS = 4
M_QK_DIM = 64
M_V_DIM = 128
M_CHUNK = 128
EPS = 1e-6

ATT_Q_W = ATT_HEADS * ATT_HEAD_DIM
ATT_KV_W = ATT_KV_HEADS * ATT_HEAD_DIM
M_QK_W = M_HEADS * M_QK_DIM
M_V_W = M_HEADS * M_V_DIM
IN_WIDTHS = (ATT_Q_W, ATT_KV_W, ATT_KV_W, M_QK_W, M_QK_W, M_V_W, M_V_W, 2 * M_HEADS, 2 * D_MODEL)
IN_OFFS = tuple(int(o) for o in np.concatenate([[0], np.cumsum(IN_WIDTHS)]))
LANES = 128
SUBLANES_BF16 = 16
FM_WIDTHS = (M_V_W, ATT_Q_W, ATT_KV_W, M_QK_W, M_V_W)
FM_OFFS = tuple(int(o) for o in np.concatenate([[0], np.cumsum(FM_WIDTHS)]))
TM_WIDTHS = (2 * D_MODEL, ATT_KV_W, M_QK_W, LANES)
TM_OFFS = tuple(int(o) for o in np.concatenate([[0], np.cumsum(TM_WIDTHS)]))

NEG_BIG = -1e30
LOG2_E = 1.4426950408889634
VMEM_LIMIT = 56 * 1024 * 1024
TOKEN_TILE = 512
POST_SUB = 256
STAGE_ROWS_D = 512
STAGE_ROWS_F = 128
MIXERS_TILE = 1024
IN_PROJ_TILE = 1024
PREP_COLS = 256

BF16 = jnp.bfloat16
F32 = jnp.float32


def _dot(a, b):
    return jnp.dot(a, b, preferred_element_type=F32)


def _dot_nt(a, b):
    return lax.dot_general(a, b, (((1,), (1,)), ((), ())), preferred_element_type=F32)


def _dot_tn(a, b):
    return lax.dot_general(a, b, (((0,), (0,)), ((), ())), preferred_element_type=F32)


def _sigmoid_of_twice(half_x):
    return 0.5 * jnp.tanh(half_x) + 0.5


def _log_sigmoid(x):
    return jnp.minimum(x, 0.0) - jnp.log(1.0 + jnp.exp(-jnp.abs(x)))


def _layer_spec(arr, layer, single_buffer=False):
    mode = dict(pipeline_mode=pl.Buffered(1)) if single_buffer else {}
    return pl.BlockSpec((None,) + arr.shape[1:], lambda *_: (layer,) + (0,) * (arr.ndim - 1), **mode)


def _params(*sem):
    return pltpu.CompilerParams(dimension_semantics=sem, vmem_limit_bytes=VMEM_LIMIT)


def _regroup_w_in(w_ref, fm_ref, tm_ref):
    w = w_ref[...]
    seg = lambda i: w[IN_OFFS[i]:IN_OFFS[i + 1], :]
    q_a, k_a, v_a, q_m, k_m, v_m, o_m, if_m, gates = (seg(i) for i in range(len(IN_WIDTHS)))
    fm_ref[...] = jnp.concatenate([0.5 * o_m, q_a, v_a, q_m, v_m], axis=0).astype(BF16)
    if_pad = jnp.zeros((LANES - 2 * M_HEADS, w.shape[1]), F32)
    tm_ref[...] = jnp.concatenate([0.5 * gates, k_a, k_m, if_m, if_pad], axis=0).T.astype(BF16)


def _stage_w_in(layer, w_hbm, fm_ref, tm_ref, stage_ref, sems):
    cols = stage_ref.shape[2]
    copies = [pltpu.make_async_copy(w_hbm.at[layer, :, pl.ds(c * cols, cols)], stage_ref.at[c % 2], sems.at[c % 2])
              for c in range(w_hbm.shape[2] // cols)]
    copies[0].start()
    for c, copy in enumerate(copies):
        if c + 1 < len(copies):
            copies[c + 1].start()
        copy.wait()
        _regroup_w_in(stage_ref.at[c % 2], fm_ref.at[:, pl.ds(c * cols, cols)], tm_ref.at[pl.ds(c * cols, cols), :])


def _in_proj_kernel(x_ref, gain_ref, bias_ref, qgain_ref, kgain_ref,
                    cos_ref, sin_ref, rc_ref, rs1_ref, rs2_ref, w_hbm,
                    qa_ref, va_ref, qm_ref, vm_ref, om_ref, ka_ref, km_ref, sg_ref, gcol_ref, grow_ref,
                    w_fm_ref, w_tm_ref, stage_ref, sems, *, layer):
    @pl.when(pl.program_id(0) == 0)
    def _():
        _stage_w_in(layer, w_hbm, w_fm_ref, w_tm_ref, stage_ref, sems)

    x = x_ref[...]
    inv = lax.rsqrt(jnp.mean(x * x, axis=-1, keepdims=True) + EPS)
    h = ((x * inv) * gain_ref[...]).astype(BF16)
    tm = x.shape[0]

    sg_ref[...] = _sigmoid_of_twice(_dot(h, w_tm_ref[:, TM_OFFS[0]:TM_OFFS[1]])).astype(BF16)
    om_ref[...] = _sigmoid_of_twice(_dot_nt(w_fm_ref[FM_OFFS[0]:FM_OFFS[1], :], h)).astype(BF16)

    zq = _dot_nt(w_fm_ref[FM_OFFS[1]:FM_OFFS[2], :], h)
    cos, sin = cos_ref[...], sin_ref[...]
    qgain = qgain_ref[...] * (ATT_HEAD_DIM ** -0.5 * LOG2_E)
    for hd in range(ATT_HEADS):
        z = zq[hd * ATT_HEAD_DIM:(hd + 1) * ATT_HEAD_DIM, :]
        inv_h = lax.rsqrt(jnp.mean(z * z, axis=0, keepdims=True) + EPS)
        y = (z * inv_h) * qgain
        x1, x2 = y[:ROPE_HALF], y[ROPE_HALF:ROPE_DIM]
        y = jnp.concatenate([x1 * cos - x2 * sin, x2 * cos + x1 * sin, y[ROPE_DIM:]], axis=0)
        qa_ref[hd * ATT_HEAD_DIM:(hd + 1) * ATT_HEAD_DIM, :] = y.astype(BF16)

    zn = _dot(h, w_tm_ref[:, TM_OFFS[1]:TM_OFFS[4]])
    zr = _dot_nt(w_fm_ref[FM_OFFS[2]:FM_OFFS[5], :], h)
    va_ref[...] = zr[:ATT_KV_W, :].astype(BF16)
    qm_ref[...] = (zr[ATT_KV_W:ATT_KV_W + M_QK_W, :] * (M_QK_DIM ** -0.5)).astype(BF16)
    vm_ref[...] = zr[ATT_KV_W + M_QK_W:, :].astype(BF16)

    lane = lax.broadcasted_iota(jnp.int32, (tm, LANES), 1)
    low = lane < ATT_HEAD_DIM
    zk = zn[:, :ATT_KV_W]
    s = zk * zk
    r0 = jnp.sum(jnp.where(low, s, 0.0), axis=-1, keepdims=True)
    r1 = jnp.sum(jnp.where(low, 0.0, s), axis=-1, keepdims=True)
    inv_k = lax.rsqrt(jnp.where(low, r0, r1) * (1.0 / ATT_HEAD_DIM) + EPS)
    y = (zk * inv_k) * kgain_ref[...]
    up = pltpu.roll(y, LANES - ROPE_HALF, 1)
    dn = pltpu.roll(y, ROPE_HALF, 1)
    ka_ref[...] = (y * rc_ref[...] + up * rs1_ref[...] + dn * rs2_ref[...]).astype(BF16)
    km_ref[...] = zn[:, ATT_KV_W:ATT_KV_W + M_QK_W].astype(BF16)
    zi = zn[:, ATT_KV_W + M_QK_W:] + bias_ref[...]
    g = jnp.where(lane < M_HEADS, zi, jnp.where(lane < 2 * M_HEADS, _log_sigmoid(zi), 0.0)) * LOG2_E
    gcol_ref[...] = g
    grow_ref[...] = g.T[:2 * M_HEADS, :]


def _in_proj(x, gain, w_in_t, bias, qgain, kgain, cos_t, sin_t, rc, rs1, rs2, *, layer, seq, tm):
    T, D = x.shape
    n_pos = seq // tm
    row = lambda w: pl.BlockSpec((tm, w), lambda i: (i, 0))
    colm = lambda w: pl.BlockSpec((w, tm), lambda i: (0, i))
    pos = pl.BlockSpec((tm, LANES), lambda i: (i % n_pos, 0))
    pos_t = pl.BlockSpec((ROPE_HALF, tm), lambda i: (0, i % n_pos))
    fm = lambda w: jax.ShapeDtypeStruct((w, T), BF16)
    out_shapes = (fm(ATT_Q_W), fm(ATT_KV_W), fm(M_QK_W), fm(M_V_W), fm(M_V_W),
                  jax.ShapeDtypeStruct((T, ATT_KV_W), BF16),
                  jax.ShapeDtypeStruct((T, M_QK_W), BF16),
                  jax.ShapeDtypeStruct((T, 2 * D), BF16),
                  jax.ShapeDtypeStruct((T, LANES), F32),
                  jax.ShapeDtypeStruct((2 * M_HEADS, T), F32))
    out_specs = (colm(ATT_Q_W), colm(ATT_KV_W), colm(M_QK_W), colm(M_V_W), colm(M_V_W),
                 row(ATT_KV_W), row(M_QK_W), row(2 * D), row(LANES), colm(2 * M_HEADS))
    in_specs = [row(D)] + [_layer_spec(a, layer, single_buffer=True) for a in (gain, bias, qgain, kgain)] + [
        pos_t, pos_t, pos, pos, pos, pl.BlockSpec(memory_space=pl.ANY)]
    d_in = w_in_t.shape[1]
    return pl.pallas_call(
        functools.partial(_in_proj_kernel, layer=layer), grid=(T // tm,),
        in_specs=in_specs, out_specs=out_specs, out_shape=out_shapes,
        scratch_shapes=[pltpu.VMEM((FM_OFFS[-1], D), BF16), pltpu.VMEM((D, TM_OFFS[-1]), BF16),
                        pltpu.VMEM((2, d_in, PREP_COLS), F32), pltpu.SemaphoreType.DMA((2,))],
        compiler_params=_params("arbitrary"), name="in_proj",
    )(x, gain, bias, qgain, kgain, cos_t, sin_t, rc, rs1, rs2, w_in_t)


def _attention_units(q_ref, k_ref, kp_ref, v_ref, vp_ref):
    blk, hd_dim = ATT_BLOCK, ATT_HEAD_DIM
    units = []
    for b in range(q_ref.shape[1] // blk):
        cur = slice(b * blk, (b + 1) * blk)
        old = slice((b - 1) * blk, b * blk)
        k_cat = jnp.concatenate([kp_ref[...] if b == 0 else k_ref[old, :], k_ref[cur, :]], axis=0)
        v_prev = vp_ref[...] if b == 0 else v_ref[:, old]
        v_cur = v_ref[:, cur]
        for kv in range(ATT_KV_HEADS):
            rows = slice(kv * hd_dim, (kv + 1) * hd_dim)
            v_cat = jnp.concatenate([v_prev[rows, :], v_cur[rows, :]], axis=1)
            units.append((b, cur, kv, k_cat, v_cat))
    return units


def _attention_scores(unit, q_ref):
    b, cur, kv, k_cat, _ = unit
    hd_dim = ATT_HEAD_DIM
    zeros = jnp.zeros((hd_dim, ATT_BLOCK), BF16)
    q_pad = []
    for g in range(ATT_GROUP):
        hd = kv * ATT_GROUP + g
        qh = q_ref[hd * hd_dim:(hd + 1) * hd_dim, cur]
        q_pad.append(jnp.concatenate([qh, zeros] if kv == 0 else [zeros, qh], axis=0))
    return _dot(k_cat, jnp.concatenate(q_pad, axis=1))


def _attention_softmax(unit, s_all, first, sinks_ref, layer):
    b, cur, kv, _, _ = unit
    blk = ATT_BLOCK
    key = lax.broadcasted_iota(jnp.int32, (blk, blk), 0)
    qry = lax.broadcasted_iota(jnp.int32, (blk, blk), 1)
    from_prev = key > qry
    zero_p = jnp.zeros((blk, blk), BF16)
    p_all, r_den = [], []
    for g in range(ATT_GROUP):
        sink = sinks_ref[layer * ATT_HEADS + kv * ATT_GROUP + g] * LOG2_E
        s_prev = s_all[:blk, g * blk:(g + 1) * blk]
        if b == 0:
            s_prev = s_prev + jnp.where(first, NEG_BIG, 0.0)
        s = jnp.where(from_prev, s_prev, s_all[blk:, g * blk:(g + 1) * blk])
        m = jnp.maximum(jnp.max(s, axis=0, keepdims=True), sink)
        p = jnp.exp2(s - m)
        r_den.append(1.0 / (jnp.sum(p, axis=0, keepdims=True) + jnp.exp2(sink - m)))
        pb = p.astype(BF16)
        p_all.append(jnp.concatenate([jnp.where(from_prev, pb, zero_p), jnp.where(from_prev, zero_p, pb)], axis=0))
    return jnp.concatenate(p_all, axis=1), r_den


def _attention_output(unit, probs, r_den, o_ref):
    b, cur, kv, _, v_cat = unit
    blk, hd_dim = ATT_BLOCK, ATT_HEAD_DIM
    o_all = _dot(v_cat, probs)
    for g in range(ATT_GROUP):
        hd = kv * ATT_GROUP + g
        o_ref[hd * hd_dim:(hd + 1) * hd_dim, cur] = (o_all[:, g * blk:(g + 1) * blk] * r_den[g]).astype(BF16)


def _bf16_part(x):
    bits = pltpu.bitcast(x, jnp.uint32) & jnp.uint32(0xFFFF0000)
    return pltpu.bitcast(bits, F32)


def _cumsum_rows(tril, x):
    lane = lax.broadcasted_iota(jnp.int32, x.shape, 1)
    w = 2 * M_HEADS
    h1 = _bf16_part(x)
    r1 = x - h1
    h2 = _bf16_part(r1)
    h3 = r1 - h2
    packed = jnp.where(lane < w, h1, jnp.where(lane < 2 * w, pltpu.roll(h2, w, 1), pltpu.roll(h3, 2 * w, 1)))
    c = _dot(tril, packed.astype(BF16))
    return c + pltpu.roll(c, LANES - w, 1) + pltpu.roll(c, LANES - 2 * w, 1)


def _split3(x):
    h1 = _bf16_part(x)
    r1 = x - h1
    h2 = _bf16_part(r1)
    return h1.astype(BF16), h2.astype(BF16), (r1 - h2).astype(BF16)


def _mlstm_gates_and_scores(q_ref, k_ref, gcol_ref, grow_ref):
    L = M_CHUNK
    src = lax.broadcasted_iota(jnp.int32, (L, L), 0)
    dst = lax.broadcasted_iota(jnp.int32, (L, L), 1)
    tril = jnp.where(dst <= src, 1.0, 0.0).astype(BF16)
    triu = jnp.where(src <= dst, 1.0, 0.0).astype(BF16)
    zeros = jnp.zeros((M_QK_DIM, L), BF16)
    chunks = []
    for c in range(q_ref.shape[1] // L):
        cur = slice(c * L, (c + 1) * L)
        gcol = gcol_ref[cur, :]
        grow = grow_ref[:, cur]
        bc = _cumsum_rows(tril, gcol)
        br = sum(_dot(t, triu) for t in _split3(grow))
        pairs = []
        for j in range(M_HEADS // 2):
            k2 = k_ref[cur, j * LANES:(j + 1) * LANES]
            q2 = q_ref[j * LANES:(j + 1) * LANES, cur]
            q_pad = jnp.concatenate([jnp.concatenate([q2[:M_QK_DIM], zeros], axis=0),
                                     jnp.concatenate([zeros, q2[M_QK_DIM:]], axis=0)], axis=1)
            pairs.append(dict(k2=k2, q_pad=q_pad, s_t=_dot(k2, q_pad), heads=[{}, {}]))
        chunks.append(dict(cur=cur, gcol=gcol, grow=grow, bc=bc, br=br, pairs=pairs))
    return chunks


def _mlstm_weights(chunks, v_ref, m_ref, n_state_rows):
    L = M_CHUNK
    src = lax.broadcasted_iota(jnp.int32, (L, L), 0)
    dst = lax.broadcasted_iota(jnp.int32, (L, L), 1)
    causal = src <= dst
    zeros = jnp.zeros((L, L), BF16)
    m_run = [m_ref[h:h + 1, :] for h in range(M_HEADS)]
    for ch in chunks:
        cur, gcol, grow, bc, br = ch["cur"], ch["gcol"], ch["grow"], ch["bc"], ch["br"]
        r_col = gcol - pltpu.roll(bc, LANES - M_HEADS, 1)
        for j, pr in enumerate(ch["pairs"]):
            for hh, hd in enumerate(pr["heads"]):
                h = 2 * j + hh
                vh = v_ref[h * M_V_DIM:(h + 1) * M_V_DIM, cur]
                b_row = br[M_HEADS + h:M_HEADS + h + 1, :]
                li_row = grow[h:h + 1, :]
                m_prev = m_run[h]
                log_d = jnp.where(causal, jnp.broadcast_to(r_col[:, h:h + 1], (L, L)) + b_row, NEG_BIG)
                m_t = jnp.maximum(b_row + m_prev, jnp.max(log_d, axis=0, keepdims=True))
                w_t = jnp.exp2(log_d - m_t) * pr["s_t"][:, hh * L:(hh + 1) * L]
                g = b_row[:, L - 1:L]
                w_end = g - b_row + li_row
                m_loc = jnp.max(w_end, axis=1, keepdims=True)
                e_row = jnp.exp2(w_end - m_loc).astype(BF16)
                m_new = jnp.maximum(g + m_prev, m_loc)
                hd.update(
                    vh=vh, m_t=m_t, w_bf=w_t.astype(BF16), w_sum=jnp.sum(w_t, axis=0, keepdims=True),
                    a_int=jnp.exp2(b_row + m_prev - m_t),
                    lhs=jnp.concatenate([vh * e_row, jnp.broadcast_to(e_row, (n_state_rows - M_V_DIM, L))], axis=0),
                    decay=jnp.exp2(g + m_prev - m_new), gain=jnp.exp2(m_loc - m_new))
                m_run[h] = m_new
            ha, hb = pr["heads"]
            pr["v_cat"] = jnp.concatenate([ha["vh"], hb["vh"]], axis=1)
            pr["w_diag"] = jnp.concatenate([jnp.concatenate([ha["w_bf"], zeros], axis=1),
                                            jnp.concatenate([zeros, hb["w_bf"]], axis=1)], axis=0)
    for h in range(M_HEADS):
        m_ref[h:h + 1, :] = m_run[h]


def _mlstm_intra(chunks):
    for ch in chunks:
        for pr in ch["pairs"]:
            pr["num"] = _dot(pr["v_cat"], pr["w_diag"])
            for hd in pr["heads"]:
                hd["d_state"] = _dot(hd["lhs"], pr["k2"])


def _mlstm_states(chunks, state_ref):
    first_head = lax.broadcasted_iota(jnp.int32, state_ref.shape[1:], 1) < M_QK_DIM
    for j in range(M_HEADS // 2):
        state = state_ref[j]
        for ch in chunks:
            pr = ch["pairs"][j]
            ha, hb = pr["heads"]
            pr["state_in"] = state.astype(BF16)
            state = (jnp.where(first_head, ha["decay"], hb["decay"]) * state
                     + jnp.where(first_head, ha["gain"], hb["gain"]) * jnp.where(first_head, ha["d_state"], hb["d_state"]))
        state_ref[j] = state
    for ch in chunks:
        for pr in ch["pairs"]:
            pr["inter"] = _dot(pr["state_in"], pr["q_pad"])


def _mlstm_outputs(chunks, osig_ref, gain_ref, o_ref):
    L = M_CHUNK
    for ch in chunks:
        cur = ch["cur"]
        for j, pr in enumerate(ch["pairs"]):
            for hh, hd in enumerate(pr["heads"]):
                h = 2 * j + hh
                a_int = hd["a_int"]
                inter = pr["inter"][:, hh * L:(hh + 1) * L]
                num = pr["num"][:, hh * L:(hh + 1) * L] + a_int * inter[:M_V_DIM]
                den = hd["w_sum"] + a_int * inter[M_V_DIM:M_V_DIM + 1]
                d = jnp.maximum(jnp.abs(den), jnp.exp2(-hd["m_t"]))
                scale = lax.rsqrt(jnp.mean(num * num, axis=0, keepdims=True) + EPS * (d * d))
                rows = slice(h * M_V_DIM, (h + 1) * M_V_DIM)
                o_ref[rows, cur] = ((num * scale) * (gain_ref[rows, :] * osig_ref[rows, cur].astype(F32))).astype(BF16)


def _mixers_kernel(sinks_ref, qa_ref, ka_ref, kap_ref, va_ref, vap_ref,
                   qm_ref, km_ref, vm_ref, osig_ref, gcol_ref, grow_ref, gain_ref,
                   att_ref, hm_ref, state_ref, m_ref, *, layer):
    first = pl.program_id(1) == 0

    @pl.when(first)
    def _():
        state_ref[...] = jnp.zeros(state_ref.shape, F32)
        m_ref[...] = jnp.zeros(m_ref.shape, F32)

    units = _attention_units(qa_ref, ka_ref, kap_ref, va_ref, vap_ref)
    scores = [_attention_scores(u, qa_ref) for u in units]
    chunks = _mlstm_gates_and_scores(qm_ref, km_ref, gcol_ref, grow_ref)
    probs = [_attention_softmax(u, s, first, sinks_ref, layer) for u, s in zip(units, scores)]
    _mlstm_weights(chunks, vm_ref, m_ref, state_ref.shape[1])
    for u, (p, r_den) in zip(units, probs):
        _attention_output(u, p, r_den, att_ref)
    _mlstm_intra(chunks)
    _mlstm_states(chunks, state_ref)
    _mlstm_outputs(chunks, osig_ref, gain_ref, hm_ref)


def _mixers(sinks, qa_t, ka, va_t, qm_t, km, vm_t, osig_t, gcol, grow, gain_b, *, layer, seq, tc):
    T = km.shape[0]
    n = seq // tc
    per = tc // ATT_BLOCK
    tile = lambda b, c: b * n + c
    prev_blk = lambda b, c: jnp.maximum(tile(b, c) * per - 1, 0)
    row = lambda w: pl.BlockSpec((tc, w), lambda b, c, *_: (tile(b, c), 0))
    colm = lambda w: pl.BlockSpec((w, tc), lambda b, c, *_: (0, tile(b, c)))
    grid_spec = pltpu.PrefetchScalarGridSpec(
        num_scalar_prefetch=1, grid=(T // seq, n),
        in_specs=[colm(ATT_Q_W), row(ATT_KV_W),
                  pl.BlockSpec((ATT_BLOCK, ATT_KV_W), lambda b, c, *_: (prev_blk(b, c), 0)),
                  colm(ATT_KV_W),
                  pl.BlockSpec((ATT_KV_W, ATT_BLOCK), lambda b, c, *_: (0, prev_blk(b, c))),
                  colm(M_QK_W), row(M_QK_W), colm(M_V_W), colm(M_V_W), row(LANES), colm(2 * M_HEADS),
                  _layer_spec(gain_b, layer)],
        out_specs=(colm(ATT_Q_W), colm(M_V_W)),
        scratch_shapes=[pltpu.VMEM((M_HEADS // 2, M_V_DIM + SUBLANES_BF16, LANES), F32),
                        pltpu.VMEM((2 * M_HEADS, LANES), F32)])
    return pl.pallas_call(
        functools.partial(_mixers_kernel, layer=layer), grid_spec=grid_spec,
        out_shape=(jax.ShapeDtypeStruct((ATT_Q_W, T), BF16), jax.ShapeDtypeStruct((M_V_W, T), BF16)),
        compiler_params=_params("parallel", "arbitrary"), name="mixers",
    )(sinks, qa_t, ka, ka, va_t, va_t, qm_t, km, vm_t, osig_t, gcol, grow, gain_b)


def _stage_weights(layer, pairs, stages, sems):
    jobs, uses = [], [0] * len(stages)
    for src, dst in pairs:
        sid = [i for i, st in enumerate(stages) if st.shape[2] == dst.shape[1]][0]
        rows = stages[sid].shape[1]
        for r0 in range(0, dst.shape[0], rows):
            slot = uses[sid] % 2
            uses[sid] += 1
            copy = pltpu.make_async_copy(src.at[layer, pl.ds(r0, rows), :], stages[sid].at[slot],
                                         sems.at[2 * sid + slot])
            jobs.append((copy, stages[sid], slot, dst, r0, rows))
    jobs[0][0].start()
    for k, (copy, stage, slot, dst, r0, rows) in enumerate(jobs):
        if k + 1 < len(jobs):
            jobs[k + 1][0].start()
        copy.wait()
        dst[r0:r0 + rows, :] = stage[slot].astype(BF16)


def _post_kernel(x_ref, att_ref, hm_ref, sg_ref, gain_ref, wa_hbm, wm_hbm, wo_hbm, w1_hbm, w2_hbm, o_ref,
                 wa_ref, wm_ref, wo_ref, w1_ref, w2_ref, stage_d_ref, stage_f_ref, sems, *, layer, n_chunks):
    @pl.when(pl.program_id(0) == 0)
    def _():
        _stage_weights(layer, [(wa_hbm, wa_ref), (wm_hbm, wm_ref), (wo_hbm, wo_ref), (w2_hbm, w2_ref),
                               (w1_hbm, w1_ref)], [stage_d_ref, stage_f_ref], sems)

    tm, D = x_ref.shape
    subs = [slice(r * POST_SUB, (r + 1) * POST_SUB) for r in range(tm // POST_SUB)]
    branches = [(_dot_tn(att_ref[:, s], wa_ref[...]), _dot_tn(hm_ref[:, s], wm_ref[...])) for s in subs]
    xs = []
    for s, (a, m) in zip(subs, branches):
        mixed = sg_ref[s, :D].astype(F32) * a + sg_ref[s, D:].astype(F32) * m
        xs.append(x_ref[s, :] + _dot(mixed.astype(BF16), wo_ref[...]))
    fc = w1_ref.shape[1] // n_chunks
    for s, x in zip(subs, xs):
        inv = lax.rsqrt(jnp.mean(x * x, axis=-1, keepdims=True) + EPS)
        h = ((x * inv) * gain_ref[...]).astype(BF16)
        acc = x
        for c in range(n_chunks):
            u = jnp.maximum(_dot(h, w1_ref[:, c * fc:(c + 1) * fc]), 0.0)
            acc = acc + _dot((u * u).astype(BF16), w2_ref[c * fc:(c + 1) * fc, :])
        o_ref[s, :] = acc


def _post(x, att_t, hm_t, sg, wa, wm, wo, gain, w1, w2, *, layer, tm):
    T, D = x.shape
    row = lambda w: pl.BlockSpec((tm, w), lambda i: (i, 0))
    colm = lambda w: pl.BlockSpec((w, tm), lambda i: (0, i))
    weights = (wa, wm, wo, w1, w2)
    hbm = pl.BlockSpec(memory_space=pl.ANY)
    return pl.pallas_call(
        functools.partial(_post_kernel, layer=layer, n_chunks=4), grid=(T // tm,),
        in_specs=[row(D), colm(ATT_Q_W), colm(M_V_W), row(2 * D), _layer_spec(gain, layer, single_buffer=True)]
        + [hbm] * len(weights),
        out_specs=row(D), out_shape=jax.ShapeDtypeStruct((T, D), F32),
        scratch_shapes=[pltpu.VMEM(w.shape[1:], BF16) for w in weights] + [
            pltpu.VMEM((2, STAGE_ROWS_D, D), F32), pltpu.VMEM((2, STAGE_ROWS_F, w1.shape[2]), F32),
            pltpu.SemaphoreType.DMA((4,))],
        compiler_params=_params("arbitrary"), name="post",
    )(x, att_t, hm_t, sg, gain, wa, wm, wo, w1, w2)


def _rope_tables(seq):
    pos = jnp.arange(seq, dtype=F32)
    inv_freq = ROPE_THETA ** (-jnp.arange(0, ROPE_DIM, 2, dtype=F32) / ROPE_DIM)
    ang_t = inv_freq[:, None] * pos[None, :]
    d = jnp.arange(LANES) % ATT_HEAD_DIM
    ang = pos[:, None] * inv_freq[d % ROPE_HALF][None, :]
    cos, sin = jnp.cos(ang), jnp.sin(ang)
    rc = jnp.where(d < ROPE_DIM, cos, 1.0)
    rs1 = jnp.where(d < ROPE_HALF, -sin, 0.0)
    rs2 = jnp.where((d >= ROPE_HALF) & (d < ROPE_DIM), sin, 0.0)
    return jnp.cos(ang_t), jnp.sin(ang_t), rc, rs1, rs2


def kernel(x, norm_mix, w_in, att_q_norm, att_k_norm, att_sinks, m_gate_bias, m_head_norm,
           w_att_branch, w_m_branch, w_out, norm_ffn, w_ff1, w_ff2):
    B, S, D = x.shape
    T = B * S
    depth = w_in.shape[0]
    tm = min(TOKEN_TILE, S)
    cos_t, sin_t, rc, rs1, rs2 = _rope_tables(S)
    w_in_t = jnp.swapaxes(w_in, 1, 2)
    wa, wm, wo, w1, w2 = w_att_branch, w_m_branch, w_out, w_ff1, w_ff2
    bias = jnp.pad(m_gate_bias, ((0, 0), (0, LANES - 2 * M_HEADS))).reshape(depth, 1, LANES)
    tm_in = min(IN_PROJ_TILE, S)
    qgain = jnp.broadcast_to(att_q_norm[:, :, None], (depth, ATT_HEAD_DIM, tm_in))
    kgain = jnp.tile(att_k_norm, (1, 2)).reshape(depth, 1, LANES)
    mgain = jnp.broadcast_to(m_head_norm[:, :, None], (depth, M_V_W, LANES))
    gain_mix = norm_mix.reshape(depth, 1, D)
    gain_ffn = norm_ffn.reshape(depth, 1, D)
    sinks = att_sinks.reshape(depth * ATT_HEADS)
    xt = x.reshape(T, D)
    for l in range(depth):
        qa_t, va_t, qm_t, vm_t, osig_t, ka, km, sg, gcol, grow = _in_proj(
            xt, gain_mix, w_in_t, bias, qgain, kgain, cos_t, sin_t, rc, rs1, rs2, layer=l, seq=S, tm=tm_in)
        att_t, hm_t = _mixers(sinks, qa_t, ka, va_t, qm_t, km, vm_t, osig_t, gcol, grow, mgain, layer=l, seq=S, tc=min(MIXERS_TILE, S))
        xt = _post(xt, att_t, hm_t, sg, wa, wm, wo, gain_ffn, w1, w2, layer=l, tm=tm)
    return xt.reshape(B, S, D)
```
